```python
import math
import jax
import jax.numpy as jnp
from jax import lax
import numpy as np

D_MODEL = 1024
BATCH = 8
SEQ = 2048
DEPTH = 4
DEC_BATCH = 32
DEC_SEQ = 4
PAST_LEN = 16384
PAGE_SIZE = 128

N_BRANCH = 4
BR_W = D_MODEL // N_BRANCH
DH = 64
QBLK = 128
H_NSA = 4
CMP_BLK = 32
SEL_BLK = 64
CMP_PER_SEL = SEL_BLK // CMP_BLK
TOP_N = 16
N_LOCAL = 2
WINDOW = 512
H_MLA = 4
D_CQ = 256
D_C = 256
D_NOPE = 64
D_ROPE = 32
D_V = 64
ROPE_BASE = 10000.0
MLA_SCALE = (D_NOPE + D_ROPE) ** -0.5
H_FOX = 4
KV_FOX = 2
G_FOX = H_FOX // KV_FOX
FORGET_BIAS_INIT = 3.0
H_DIFF = 4
KV_DIFF = 2
G_DIFF = H_DIFF // KV_DIFF
DD = 32
N_BUCKETS = 32
MAX_DIST = 128
DEEPNORM_ALPHA = (2 * DEPTH) ** 0.25
DEEPNORM_BETA = (8 * DEPTH) ** -0.25
NEG_INF = -1e30
FORCE = 1e9

IN_SIZES = (
    H_NSA * DH, 2 * DH, 2 * DH, 2 * DH, 3 * H_NSA, BR_W,
    D_CQ, D_C, D_ROPE, BR_W,
    H_FOX * DH, KV_FOX * DH, KV_FOX * DH, H_FOX, BR_W,
    H_DIFF * 2 * DD, KV_DIFF * 2 * DD, KV_DIFF * 2 * DD, BR_W,
    N_BRANCH * D_MODEL,
)
N_IN = sum(IN_SIZES)

kernel_name = 'hybrid_nsa_mla_fox_diff_decoder_step'


def _split(h, sizes):
    out, start = [], 0
    for n in sizes:
        out.append(h[..., start:start + n])
        start += n
    return out


def _rmsnorm(x, g, eps=1e-6):
    xf = x.astype(jnp.float32)
    y = xf * lax.rsqrt(jnp.mean(xf * xf, -1, keepdims=True) + eps)
    return (y * g).astype(x.dtype)


def _layernorm(x, g, b, eps=1e-5):
    xf = x.astype(jnp.float32)
    mu = jnp.mean(xf, -1, keepdims=True)
    var = jnp.mean(jnp.square(xf - mu), -1, keepdims=True)
    return ((xf - mu) * lax.rsqrt(var + eps) * g + b).astype(x.dtype)


def _masked_softmax(s, mask):
    s = jnp.where(mask, s.astype(jnp.float32), NEG_INF)
    e = jnp.where(mask, jnp.exp(s - jnp.max(s, -1, keepdims=True)), 0.0)
    return e / jnp.maximum(jnp.sum(e, -1, keepdims=True), 1e-30)


def _t5_bucket(dist):
    n = jnp.maximum(dist, 0)
    exact = N_BUCKETS // 2
    nf = jnp.maximum(n, 1).astype(jnp.float32)
    large = exact + (jnp.log(nf / exact) / math.log(MAX_DIST / exact) * (N_BUCKETS - exact)).astype(jnp.int32)
    return jnp.where(n < exact, n, jnp.minimum(large, N_BUCKETS - 1))


def _rope(x, pos):
    half = D_ROPE // 2
    inv = ROPE_BASE ** (-jnp.arange(half, dtype=jnp.float32) / half)
    ang = pos.astype(jnp.float32)[:, None] * inv
    ang = ang.reshape(ang.shape[0], *([1] * (x.ndim - 3)), half)
    cos, sin = jnp.cos(ang), jnp.sin(ang)
    x1 = x[..., :half].astype(jnp.float32)
    x2 = x[..., half:].astype(jnp.float32)
    return jnp.concatenate([x1 * cos - x2 * sin, x2 * cos + x1 * sin], -1).astype(x.dtype)


def _sweep(fn, q_pos, *q_arrays):
    sq = q_pos.shape[0]
    if sq > QBLK and sq % QBLK == 0:
        nb = sq // QBLK
        blocks = tuple(jnp.moveaxis(a.reshape(a.shape[0], nb, QBLK, *a.shape[2:]), 1, 0) for a in q_arrays)
        out = lax.map(lambda args: fn(args[0], *args[1]), (q_pos.reshape(nb, QBLK), blocks))
        out = jnp.moveaxis(out, 0, 1)
        return out.reshape(out.shape[0], sq, *out.shape[3:])
    return fn(q_pos, *q_arrays)


def _block_rows(rows):
    B, T = rows.shape[:2]
    n_blk = -(-T // SEL_BLK)
    pad = ((0, 0), (0, n_blk * SEL_BLK - T)) + ((0, 0),) * (rows.ndim - 2)
    return jnp.pad(rows, pad).reshape(B, n_blk, SEL_BLK, *rows.shape[2:])


def _take_blocks(blocks, idx):
    return jax.vmap(lambda blk, i: blk[i])(blocks, idx)


def _local_block_gather(rows):
    blocks = _block_rows(rows)
    return lambda idx: _take_blocks(blocks, idx)


def _paged_block_gather(pool, page_table, new_rows):
    B = new_rows.shape[0]
    bpp = PAGE_SIZE // SEL_BLK
    n_past = page_table.shape[1] * bpp
    pool_blocks = pool.reshape(-1, SEL_BLK, *pool.shape[2:])
    tail = _block_rows(new_rows)
    n_tail = tail.shape[1]

    def gather(idx):
        ip = jnp.clip(idx, 0, n_past - 1)
        page = jnp.take_along_axis(page_table, (ip // bpp).reshape(B, -1), axis=1).reshape(ip.shape)
        from_pool = pool_blocks[page * bpp + ip % bpp]
        from_tail = _take_blocks(tail, jnp.clip(idx - n_past, 0, n_tail - 1))
        return jnp.where((idx < n_past)[..., None, None, None], from_pool, from_tail)
    return gather


def _nsa_compressed(q, qpos, rows, cmp_pos, cmp_w1, cmp_w2, tbl):
    B, T = rows.shape[:2]
    Q = q.shape[1]
    n_cmp = T // CMP_BLK
    blocks = rows[:, :n_cmp * CMP_BLK].reshape(B, n_cmp, CMP_BLK, 2, DH)
    hid = jnp.einsum('bnlcd,cldh->bnch', blocks + jnp.swapaxes(cmp_pos, 0, 1),
                     cmp_w1.reshape(2, CMP_BLK, DH, DH))
    kv_c = jnp.einsum('bnch,che->bnce', jax.nn.silu(hid), cmp_w2)
    ends = (jnp.arange(n_cmp) + 1) * CMP_BLK - 1
    dist = qpos[:, None] - ends[None, :]
    s = jnp.einsum('bqhd,bnd->bhqn', q, kv_c[:, :, 0]).astype(jnp.float32)
    s = s + jnp.moveaxis(tbl[_t5_bucket(dist)], -1, 0)
    p = _masked_softmax(s, dist >= 0)
    o = jnp.einsum('bhqn,bnd->bqhd', p.astype(q.dtype), kv_c[:, :, 1])
    n_sel = -(-T // SEL_BLK)
    imp = jnp.pad(jnp.sum(p, axis=1), ((0, 0), (0, 0), (0, n_sel * CMP_PER_SEL - n_cmp)))
    imp = imp.reshape(B, Q, n_sel, CMP_PER_SEL).sum(-1)
    blk = jnp.arange(n_sel)[None, :]
    cur = (qpos // SEL_BLK)[:, None]
    valid = blk <= cur
    forced = valid & ((blk == 0) | (cur - blk < N_LOCAL))
    score = jnp.where(valid, jnp.where(forced, FORCE, imp), -FORCE)
    _, idx = lax.top_k(score, min(TOP_N, n_sel))
    return o, idx


def _nsa_selected(qpos, q, idx, gather, tbl):
    kv = gather(idx)
    kpos = idx[..., None] * SEL_BLK + jnp.arange(SEL_BLK)
    dist = qpos[None, :, None, None] - kpos
    s = jnp.einsum('bqhd,bqnld->bqhnl', q, kv[..., 0, :]).astype(jnp.float32)
    s = s + jnp.moveaxis(tbl[_t5_bucket(dist)], -1, 2)
    B, Q, H, n, L = s.shape
    p = _masked_softmax(s.reshape(B, Q, H, n * L), (dist >= 0).reshape(B, Q, 1, n * L))
    return jnp.einsum('bqhnl,bqnld->bqhd', p.reshape(B, Q, H, n, L).astype(q.dtype), kv[..., 1, :])


def _window_attend(q, kv, qpos, kpos, tbl):
    dist = qpos[:, None] - kpos[None, :]
    s = jnp.einsum('bqhd,bkd->bhqk', q, kv[:, :, 0]).astype(jnp.float32)
    s = s + jnp.moveaxis(tbl[_t5_bucket(dist)], -1, 0)
    mask = (dist >= 0) & (dist <= WINDOW) & (kpos >= 0)[None, :]
    p = _masked_softmax(s, mask)
    return jnp.einsum('bhqk,bkd->bqhd', p.astype(q.dtype), kv[:, :, 1])


def _window_banded(q, kv, pos, tbl):
    B, S = q.shape[:2]
    qb = QBLK if S % QBLK == 0 else S
    nb = S // qb
    padded = jnp.pad(kv, ((0, 0), (WINDOW, 0), (0, 0), (0, 0)))
    kidx = jnp.arange(nb)[:, None] * qb + jnp.arange(WINDOW + qb)[None, :]
    kv_b = padded[:, kidx]
    kpos_b = pos[0] + kidx - WINDOW
    q_b = q.reshape(B, nb, qb, H_NSA, DH)
    o = jax.vmap(_window_attend, in_axes=(1, 1, 0, 0, None), out_axes=1)(
        q_b, kv_b, pos.reshape(nb, qb), kpos_b, tbl)
    return o.reshape(B, S, H_NSA, DH)


def _mla_attend(qpos, q_lat, q_pe, lat, kpe, kpos):
    s = (jnp.einsum('bqhc,bkc->bhqk', q_lat, lat) + jnp.einsum('bqhr,bkr->bhqk', q_pe, kpe)).astype(jnp.float32)
    p = _masked_softmax(s, kpos[None, :] <= qpos[:, None])
    return jnp.einsum('bhqk,bkc->bqhc', p.astype(lat.dtype), lat)


def _fox_attend(qpos, q, r_q, kv, r_k, kpos):
    B, Q = q.shape[:2]
    T = kv.shape[1]
    qg = q.reshape(B, Q, KV_FOX, G_FOX, DH)
    s = jnp.einsum('bqkgd,btkd->bkgqt', qg, kv[:, :, 0]).astype(jnp.float32)
    rk = r_k.reshape(B, T, KV_FOX, G_FOX).transpose(0, 2, 3, 1)[:, :, :, None, :]
    rq = r_q.reshape(B, Q, KV_FOX, G_FOX).transpose(0, 2, 3, 1)[..., None]
    p = _masked_softmax(s + rk - rq, kpos[None, :] <= qpos[:, None])
    o = jnp.einsum('bkgqt,btkd->bqkgd', p.astype(kv.dtype), kv[:, :, 1])
    return o.reshape(B, Q, H_FOX, DH)


def _diff_attend(qpos, q, kv, kpos, tbl, lam):
    B, Q = q.shape[:2]
    T = kv.shape[1]
    k = kv[:, :, 0].reshape(B, T, KV_DIFF, 2, DD)
    v = kv[:, :, 1]
    qg = q.reshape(B, Q, KV_DIFF, G_DIFF, 2, DD)
    s = jnp.einsum('bqkgid,btkid->ibkgqt', qg, k).astype(jnp.float32)
    dist = qpos[:, None] - kpos[None, :]
    bias = jnp.moveaxis(tbl[_t5_bucket(dist)], -1, 0).reshape(KV_DIFF, G_DIFF, Q, T)
    p = _masked_softmax(s + bias, dist >= 0)
    a = (p[0] - lam * p[1]).astype(v.dtype)
    o = jnp.einsum('bkgqt,btkv->bqkgv', a, v)
    return o.reshape(B, Q, H_DIFF, 2 * DD)


def _gather_pages(pool, page_table):
    g = pool[page_table]
    return g.reshape(g.shape[0], -1, *g.shape[3:])


def _layer(x, pos, lidx, past, rel_bias, w_in, cmp_pos, cmp_w1, cmp_w2, mla_gq, mla_gkv,
           mla_wuq, mla_wuk, mla_wuv, fox_bf, diff_lam, diff_g, w_branch, w_out, ln_g, ln_b):
    B, S, _ = x.shape
    dt = x.dtype
    (a_q, a_cmp, a_sel, a_win, a_gate, a_z,
     b_cq, b_ckv, b_kpe, b_z,
     c_q, c_k, c_v, c_f, c_z,
     d_q, d_k, d_v, d_z, merge_logit) = _split(x @ w_in, IN_SIZES)
    tbl_a, tbl_d = rel_bias[:, :H_NSA], rel_bias[:, H_NSA:]

    def with_past(name, new):
        return new if past is None else jnp.concatenate([past[name], new], axis=1)

    q_a = a_q.reshape(B, S, H_NSA, DH) * DH ** -0.5
    cmp_new = a_cmp.reshape(B, S, 2, DH)
    sel_new = a_sel.reshape(B, S, 2, DH)
    win_new = a_win.reshape(B, S, 2, DH)
    o_cmp, sel_idx = _nsa_compressed(q_a, pos, with_past('nsa_cmp', cmp_new), cmp_pos, cmp_w1, cmp_w2, tbl_a)
    if past is None:
        gather = _local_block_gather(sel_new)
        win_all = win_new
        o_win = _window_banded(q_a, win_new, pos, tbl_a)
    else:
        gather = _paged_block_gather(past['nsa_sel_pool'], past['page_table'], sel_new)
        win_all = jnp.concatenate([past['nsa_win'], win_new], axis=1)
        kpos_w = pos[0] - past['nsa_win'].shape[1] + jnp.arange(win_all.shape[1])
        o_win = _window_attend(q_a, win_all, pos, kpos_w, tbl_a)
    win_state = win_all[:, win_all.shape[1] - min(WINDOW, win_all.shape[1]):]
    o_sel = _sweep(lambda qp, qb, ib: _nsa_selected(qp, qb, ib, gather, tbl_a), pos, q_a, sel_idx)
    g_a = jax.nn.sigmoid(a_gate.reshape(B, S, H_NSA, 3))
    o_a = (g_a[..., 0:1] * o_cmp + g_a[..., 1:2] * o_sel + g_a[..., 2:3] * o_win).reshape(B, S, BR_W)

    cq = _rmsnorm(b_cq, mla_gq)
    qf = (cq @ mla_wuq).reshape(B, S, H_MLA, D_NOPE + D_ROPE)
    q_lat = jnp.einsum('bshn,chn->bshc', qf[..., :D_NOPE], mla_wuk) * MLA_SCALE
    q_pe = _rope(qf[..., D_NOPE:], pos) * MLA_SCALE
    lat_new = _rmsnorm(b_ckv, mla_gkv)
    kpe_new = _rope(b_kpe, pos)
    lat, kpe = with_past('mla_lat', lat_new), with_past('mla_kpe', kpe_new)
    kpos_all = jnp.arange(lat.shape[1])
    o_lat = _sweep(lambda qp, ql, qr: _mla_attend(qp, ql, qr, lat, kpe, kpos_all), pos, q_lat, q_pe)
    o_b = jnp.einsum('bshc,chv->bshv', o_lat, mla_wuv).reshape(B, S, BR_W)

    q_c = c_q.reshape(B, S, H_FOX, DH) * DH ** -0.5
    fkv_new = jnp.stack([c_k.reshape(B, S, KV_FOX, DH), c_v.reshape(B, S, KV_FOX, DH)], axis=2)
    logf_new = jax.nn.log_sigmoid((c_f + fox_bf).astype(jnp.float32))
    fkv = with_past('fox_kv', fkv_new)
    logf = with_past('fox_logf', logf_new).astype(jnp.float32)
    r = lax.cumsum(logf, axis=1, reverse=True) - logf
    o_c = _sweep(lambda qp, qb, rq: _fox_attend(qp, qb, rq, fkv, r, kpos_all),
                 pos, q_c, r[:, r.shape[1] - S:]).reshape(B, S, BR_W)

    q_d = d_q.reshape(B, S, H_DIFF, 2, DD) * DD ** -0.5
    dkv_new = jnp.stack([d_k.reshape(B, S, KV_DIFF, 2 * DD), d_v.reshape(B, S, KV_DIFF, 2 * DD)], axis=2)
    dkv = with_past('diff_kv', dkv_new)
    lam_init = 0.8 - 0.6 * math.exp(-0.3 * lidx)
    lam = (jnp.exp(jnp.sum(diff_lam[0] * diff_lam[1]).astype(jnp.float32))
           - jnp.exp(jnp.sum(diff_lam[2] * diff_lam[3]).astype(jnp.float32)) + lam_init)
    o_d = _sweep(lambda qp, qb: _diff_attend(qp, qb, dkv, kpos_all, tbl_d, lam), pos, q_d)
    o_d = (_rmsnorm(o_d, diff_g, 1e-5) * (1.0 - lam_init)).reshape(B, S, BR_W)

    outs = jnp.stack([o_a * jax.nn.silu(a_z), o_b * jax.nn.silu(b_z),
                      o_c * jax.nn.silu(c_z), o_d * jax.nn.silu(d_z)], axis=2)
    branch = jnp.einsum('bsnc,ncd->bsnd', outs, w_branch)
    gates = jax.nn.sigmoid(merge_logit.reshape(B, S, N_BRANCH, D_MODEL))
    y = jnp.sum(gates * branch, axis=2) @ w_out
    x_new = _layernorm(DEEPNORM_ALPHA * x + y, ln_g, ln_b)
    states = (cmp_new, sel_new, win_state, lat_new, kpe_new, fkv_new, logf_new.astype(dt), dkv_new)
    return x_new, states


def setup_inputs(seed: int = 0) -> dict:
    key = jax.random.key(seed)
    ks = jax.random.split(key, 32)
    f32 = jnp.float32

    def nrm(k, shape, scale=1.0):
        return scale * jax.random.normal(k, shape, f32)

    n_pages = PAST_LEN // PAGE_SIZE
    n_pool = (DEC_BATCH * n_pages * 5) // 4
    win_buf = min(WINDOW, PAST_LEN)
    pg = (DEPTH, n_pool, PAGE_SIZE)
    page_table = jax.random.permutation(ks[0], n_pool)[:DEC_BATCH * n_pages].reshape(DEC_BATCH, n_pages).astype(jnp.int32)
    return {
        'x_prompt': nrm(ks[1], (BATCH, SEQ, D_MODEL)),
        'x_sample': nrm(ks[2], (DEC_BATCH, DEC_SEQ, D_MODEL)),
        'cache_nsa_cmp_kv': nrm(ks[3], pg + (2, DH)),
        'cache_nsa_sel_kv': nrm(ks[4], pg + (2, DH)),
        'state_nsa_win_kv': nrm(ks[5], (DEPTH, DEC_BATCH, win_buf, 2, DH)),
        'cache_mla_latent': nrm(ks[6], pg + (D_C,)),
        'cache_mla_kpe': nrm(ks[7], pg + (D_ROPE,)),
        'cache_fox_kv': nrm(ks[8], pg + (2, KV_FOX, DH)),
        'cache_fox_logf': jax.nn.log_sigmoid(FORGET_BIAS_INIT + nrm(ks[9], pg + (H_FOX,))),
        'cache_diff_kv': nrm(ks[10], pg + (2, KV_DIFF, 2 * DD)),
        'page_table': page_table,
        'rel_bias': nrm(ks[11], (N_BUCKETS, H_NSA + H_DIFF), 0.5),
        'w_in': nrm(ks[12], (DEPTH, D_MODEL, N_IN), D_MODEL ** -0.5),
        'nsa_cmp_pos': nrm(ks[13], (DEPTH, 2, CMP_BLK, DH), 0.1),
        'nsa_cmp_w1': nrm(ks[14], (DEPTH, 2, CMP_BLK * DH, DH), (CMP_BLK * DH) ** -0.5),
        'nsa_cmp_w2': nrm(ks[15], (DEPTH, 2, DH, DH), DH ** -0.5),
        'mla_q_norm': 1.0 + nrm(ks[16], (DEPTH, D_CQ), 0.01),
        'mla_kv_norm': 1.0 + nrm(ks[17], (DEPTH, D_C), 0.01),
        'mla_w_uq': nrm(ks[18], (DEPTH, D_CQ, H_MLA * (D_NOPE + D_ROPE)), D_CQ ** -0.5),
        'mla_w_uk': nrm(ks[19], (DEPTH, D_C, H_MLA, D_NOPE), D_C ** -0.5),
        'mla_w_uv': nrm(ks[20], (DEPTH, D_C, H_MLA, D_V), D_C ** -0.5),
        'fox_forget_bias': FORGET_BIAS_INIT + nrm(ks[21], (DEPTH, H_FOX), 0.5),
        'diff_lambda': nrm(ks[22], (DEPTH, 4, DD), 0.1),
        'diff_subln': 1.0 + nrm(ks[23], (DEPTH, 2 * DD), 0.01),
        'w_branch': nrm(ks[24], (DEPTH, N_BRANCH, BR_W, D_MODEL), DEEPNORM_BETA * BR_W ** -0.5),
        'w_out': nrm(ks[25], (DEPTH, D_MODEL, D_MODEL), DEEPNORM_BETA * D_MODEL ** -0.5),
        'ln_g': 1.0 + nrm(ks[26], (DEPTH, D_MODEL), 0.01),
        'ln_b': nrm(ks[27], (DEPTH, D_MODEL), 0.01),
    }


def reference(x_prompt, x_sample, cache_nsa_cmp_kv, cache_nsa_sel_kv, state_nsa_win_kv, cache_mla_latent,
              cache_mla_kpe, cache_fox_kv, cache_fox_logf, cache_diff_kv, page_table, rel_bias, w_in,
              nsa_cmp_pos, nsa_cmp_w1, nsa_cmp_w2, mla_q_norm, mla_kv_norm, mla_w_uq, mla_w_uk, mla_w_uv,
              fox_forget_bias, diff_lambda, diff_subln, w_branch, w_out, ln_g, ln_b):
    past_len = page_table.shape[1] * PAGE_SIZE
    pos_p = jnp.arange(x_prompt.shape[1])
    pos_s = past_len + jnp.arange(x_sample.shape[1])
    xp, xs = x_prompt, x_sample
    st_p, st_s = [], []
    for l in range(DEPTH):
        lw = (w_in[l], nsa_cmp_pos[l], nsa_cmp_w1[l], nsa_cmp_w2[l], mla_q_norm[l], mla_kv_norm[l],
              mla_w_uq[l], mla_w_uk[l], mla_w_uv[l], fox_forget_bias[l], diff_lambda[l], diff_subln[l],
              w_branch[l], w_out[l], ln_g[l], ln_b[l])
        xp, sp = _layer(xp, pos_p, l, None, rel_bias, *lw)
        past = {
            'nsa_cmp': _gather_pages(cache_nsa_cmp_kv[l], page_table),
            'nsa_sel_pool': cache_nsa_sel_kv[l],
            'page_table': page_table,
            'nsa_win': state_nsa_win_kv[l],
            'mla_lat': _gather_pages(cache_mla_latent[l], page_table),
            'mla_kpe': _gather_pages(cache_mla_kpe[l], page_table),
            'fox_kv': _gather_pages(cache_fox_kv[l], page_table),
            'fox_logf': _gather_pages(cache_fox_logf[l], page_table),
            'diff_kv': _gather_pages(cache_diff_kv[l], page_table),
        }
        xs, ss = _layer(xs, pos_s, l, past, rel_bias, *lw)
        st_p.append(sp)
        st_s.append(ss)
    (p_cmp, p_sel, p_win, p_lat, p_kpe, p_fkv, p_flf, p_dkv) = [jnp.stack(z) for z in zip(*st_p)]
    (s_cmp, s_sel, s_win, s_lat, s_kpe, s_fkv, s_flf, s_dkv) = [jnp.stack(z) for z in zip(*st_s)]
    return (xp, xs, p_cmp, s_cmp, p_sel, s_sel, p_win, s_win, p_lat, s_lat, p_kpe, s_kpe,
            p_fkv, s_fkv, p_flf, s_flf, p_dkv, s_dkv)
```

```python
import math
import jax
import jax.numpy as jnp
from jax import lax
from jax.experimental import pallas as pl
from jax.experimental.pallas import tpu as pltpu


D_MODEL = 1024
DEPTH = 4
PAGE_SIZE = 128

N_BRANCH = 4
BR_W = D_MODEL // N_BRANCH
DH = 64
QBLK = 128
H_NSA = 4
CMP_BLK = 32
SEL_BLK = 64
CMP_PER_SEL = SEL_BLK // CMP_BLK
TOP_N = 16
N_LOCAL = 2
WINDOW = 512
H_MLA = 4
D_CQ = 256
D_C = 256
D_NOPE = 64
D_ROPE = 32
D_V = 64
ROPE_BASE = 10000.0
MLA_SCALE = (D_NOPE + D_ROPE) ** -0.5
H_FOX = 4
KV_FOX = 2
G_FOX = H_FOX // KV_FOX
H_DIFF = 4
KV_DIFF = 2
G_DIFF = H_DIFF // KV_DIFF
DD = 32
N_BUCKETS = 32
MAX_DIST = 128
DEEPNORM_ALPHA = (2 * DEPTH) ** 0.25
NEG_INF = -1e30
FORCE = 1e9

IN_SIZES = (
    H_NSA * DH, 2 * DH, 2 * DH, 2 * DH, 3 * H_NSA, BR_W,
    D_CQ, D_C, D_ROPE, BR_W,
    H_FOX * DH, KV_FOX * DH, KV_FOX * DH, H_FOX, BR_W,
    H_DIFF * 2 * DD, KV_DIFF * 2 * DD, KV_DIFF * 2 * DD, BR_W,
    N_BRANCH * D_MODEL,
)
N_IN = sum(IN_SIZES)


def _mm_kernel(x_ref, w_ref, o_ref):
    o_ref[...] = jnp.dot(x_ref[...].astype(jnp.bfloat16), w_ref[...].astype(jnp.bfloat16),
                         preferred_element_type=jnp.float32)


def _pmatmul(x, w):
    lead = x.shape[:-1]
    K = x.shape[-1]
    N = w.shape[-1]
    x2 = x.reshape(-1, K)
    M = x2.shape[0]
    tn = 256
    n_pad = -(-N // tn) * tn
    if n_pad != N:
        w = jnp.pad(w, ((0, 0), (0, n_pad - N)))
    tm = min(M, 1024)
    out = pl.pallas_call(
        _mm_kernel,
        grid=(M // tm, n_pad // tn),
        in_specs=[pl.BlockSpec((tm, K), lambda i, j: (i, 0)),
                  pl.BlockSpec((K, tn), lambda i, j: (0, j))],
        out_specs=pl.BlockSpec((tm, tn), lambda i, j: (i, j)),
        out_shape=jax.ShapeDtypeStruct((M, n_pad), jnp.float32),
        name="matmul",
    )(x2, w)
    return out[:, :N].reshape(*lead, N)


def _split(h, sizes):
    out, start = [], 0
    for n in sizes:
        out.append(h[..., start:start + n])
        start += n
    return out


def _rmsnorm(x, g, eps=1e-6):
    xf = x.astype(jnp.float32)
    y = xf * lax.rsqrt(jnp.mean(xf * xf, -1, keepdims=True) + eps)
    return (y * g).astype(x.dtype)


def _layernorm(x, g, b, eps=1e-5):
    xf = x.astype(jnp.float32)
    mu = jnp.mean(xf, -1, keepdims=True)
    var = jnp.mean(jnp.square(xf - mu), -1, keepdims=True)
    return ((xf - mu) * lax.rsqrt(var + eps) * g + b).astype(x.dtype)


def _masked_softmax(s, mask):
    s = jnp.where(mask, s.astype(jnp.float32), NEG_INF)
    e = jnp.where(mask, jnp.exp(s - jnp.max(s, -1, keepdims=True)), 0.0)
    return e / jnp.maximum(jnp.sum(e, -1, keepdims=True), 1e-30)


def _t5_bucket(dist):
    n = jnp.maximum(dist, 0)
    exact = N_BUCKETS // 2
    nf = jnp.maximum(n, 1).astype(jnp.float32)
    large = exact + (jnp.log(nf / exact) / math.log(MAX_DIST / exact) * (N_BUCKETS - exact)).astype(jnp.int32)
    return jnp.where(n < exact, n, jnp.minimum(large, N_BUCKETS - 1))


def _rope(x, pos):
    half = D_ROPE // 2
    inv = ROPE_BASE ** (-jnp.arange(half, dtype=jnp.float32) / half)
    ang = pos.astype(jnp.float32)[:, None] * inv
    ang = ang.reshape(ang.shape[0], *([1] * (x.ndim - 3)), half)
    cos, sin = jnp.cos(ang), jnp.sin(ang)
    x1 = x[..., :half].astype(jnp.float32)
    x2 = x[..., half:].astype(jnp.float32)
    return jnp.concatenate([x1 * cos - x2 * sin, x2 * cos + x1 * sin], -1).astype(x.dtype)


def _sweep(fn, q_pos, *q_arrays):
    sq = q_pos.shape[0]
    if sq > QBLK and sq % QBLK == 0:
        nb = sq // QBLK
        blocks = tuple(jnp.moveaxis(a.reshape(a.shape[0], nb, QBLK, *a.shape[2:]), 1, 0) for a in q_arrays)
        out = lax.map(lambda args: fn(args[0], *args[1]), (q_pos.reshape(nb, QBLK), blocks))
        out = jnp.moveaxis(out, 0, 1)
        return out.reshape(out.shape[0], sq, *out.shape[3:])
    return fn(q_pos, *q_arrays)


def _block_rows(rows):
    B, T = rows.shape[:2]
    n_blk = -(-T // SEL_BLK)
    pad = ((0, 0), (0, n_blk * SEL_BLK - T)) + ((0, 0),) * (rows.ndim - 2)
    return jnp.pad(rows, pad).reshape(B, n_blk, SEL_BLK, *rows.shape[2:])


def _take_blocks(blocks, idx):
    return jax.vmap(lambda blk, i: blk[i])(blocks, idx)


def _local_block_gather(rows):
    blocks = _block_rows(rows)
    return lambda idx: _take_blocks(blocks, idx)


def _paged_block_gather(pool, page_table, new_rows):
    B = new_rows.shape[0]
    bpp = PAGE_SIZE // SEL_BLK
    n_past = page_table.shape[1] * bpp
    pool_blocks = pool.reshape(-1, SEL_BLK, *pool.shape[2:])
    tail = _block_rows(new_rows)
    n_tail = tail.shape[1]

    def gather(idx):
        ip = jnp.clip(idx, 0, n_past - 1)
        page = jnp.take_along_axis(page_table, (ip // bpp).reshape(B, -1), axis=1).reshape(ip.shape)
        from_pool = pool_blocks[page * bpp + ip % bpp]
        from_tail = _take_blocks(tail, jnp.clip(idx - n_past, 0, n_tail - 1))
        return jnp.where((idx < n_past)[..., None, None, None], from_pool, from_tail)
    return gather


def _nsa_compressed(q, qpos, rows, cmp_pos, cmp_w1, cmp_w2, tbl):
    B, T = rows.shape[:2]
    Q = q.shape[1]
    n_cmp = T // CMP_BLK
    blocks = rows[:, :n_cmp * CMP_BLK].reshape(B, n_cmp, CMP_BLK, 2, DH)
    hid = jnp.einsum('bnlcd,cldh->bnch', blocks + jnp.swapaxes(cmp_pos, 0, 1),
                     cmp_w1.reshape(2, CMP_BLK, DH, DH))
    kv_c = jnp.einsum('bnch,che->bnce', jax.nn.silu(hid), cmp_w2)
    ends = (jnp.arange(n_cmp) + 1) * CMP_BLK - 1
    dist = qpos[:, None] - ends[None, :]
    s = jnp.einsum('bqhd,bnd->bhqn', q, kv_c[:, :, 0]).astype(jnp.float32)
    s = s + jnp.moveaxis(tbl[_t5_bucket(dist)], -1, 0)
    p = _masked_softmax(s, dist >= 0)
    o = jnp.einsum('bhqn,bnd->bqhd', p.astype(q.dtype), kv_c[:, :, 1])
    n_sel = -(-T // SEL_BLK)
    imp = jnp.pad(jnp.sum(p, axis=1), ((0, 0), (0, 0), (0, n_sel * CMP_PER_SEL - n_cmp)))
    imp = imp.reshape(B, Q, n_sel, CMP_PER_SEL).sum(-1)
    blk = jnp.arange(n_sel)[None, :]
    cur = (qpos // SEL_BLK)[:, None]
    valid = blk <= cur
    forced = valid & ((blk == 0) | (cur - blk < N_LOCAL))
    score = jnp.where(valid, jnp.where(forced, FORCE, imp), -FORCE)
    _, idx = lax.top_k(score, min(TOP_N, n_sel))
    return o, idx


def _nsa_selected(qpos, q, idx, gather, tbl):
    kv = gather(idx)
    kpos = idx[..., None] * SEL_BLK + jnp.arange(SEL_BLK)
    dist = qpos[None, :, None, None] - kpos
    s = jnp.einsum('bqhd,bqnld->bqhnl', q, kv[..., 0, :]).astype(jnp.float32)
    s = s + jnp.moveaxis(tbl[_t5_bucket(dist)], -1, 2)
    B, Q, H, n, L = s.shape
    p = _masked_softmax(s.reshape(B, Q, H, n * L), (dist >= 0).reshape(B, Q, 1, n * L))
    return jnp.einsum('bqhnl,bqnld->bqhd', p.reshape(B, Q, H, n, L).astype(q.dtype), kv[..., 1, :])


def _window_attend(q, kv, qpos, kpos, tbl):
    dist = qpos[:, None] - kpos[None, :]
    s = jnp.einsum('bqhd,bkd->bhqk', q, kv[:, :, 0]).astype(jnp.float32)
    s = s + jnp.moveaxis(tbl[_t5_bucket(dist)], -1, 0)
    mask = (dist >= 0) & (dist <= WINDOW) & (kpos >= 0)[None, :]
    p = _masked_softmax(s, mask)
    return jnp.einsum('bhqk,bkd->bqhd', p.astype(q.dtype), kv[:, :, 1])


def _window_banded(q, kv, pos, tbl):
    B, S = q.shape[:2]
    qb = QBLK if S % QBLK == 0 else S
    nb = S // qb
    padded = jnp.pad(kv, ((0, 0), (WINDOW, 0), (0, 0), (0, 0)))
    kidx = jnp.arange(nb)[:, None] * qb + jnp.arange(WINDOW + qb)[None, :]
    kv_b = padded[:, kidx]
    kpos_b = pos[0] + kidx - WINDOW
    q_b = q.reshape(B, nb, qb, H_NSA, DH)
    o = jax.vmap(_window_attend, in_axes=(1, 1, 0, 0, None), out_axes=1)(
        q_b, kv_b, pos.reshape(nb, qb), kpos_b, tbl)
    return o.reshape(B, S, H_NSA, DH)


def _mla_attend(qpos, q_lat, q_pe, lat, kpe, kpos):
    s = (jnp.einsum('bqhc,bkc->bhqk', q_lat, lat) + jnp.einsum('bqhr,bkr->bhqk', q_pe, kpe)).astype(jnp.float32)
    p = _masked_softmax(s, kpos[None, :] <= qpos[:, None])
    return jnp.einsum('bhqk,bkc->bqhc', p.astype(lat.dtype), lat)


def _fox_attend(qpos, q, r_q, kv, r_k, kpos):
    B, Q = q.shape[:2]
    T = kv.shape[1]
    qg = q.reshape(B, Q, KV_FOX, G_FOX, DH)
    s = jnp.einsum('bqkgd,btkd->bkgqt', qg, kv[:, :, 0]).astype(jnp.float32)
    rk = r_k.reshape(B, T, KV_FOX, G_FOX).transpose(0, 2, 3, 1)[:, :, :, None, :]
    rq = r_q.reshape(B, Q, KV_FOX, G_FOX).transpose(0, 2, 3, 1)[..., None]
    p = _masked_softmax(s + rk - rq, kpos[None, :] <= qpos[:, None])
    o = jnp.einsum('bkgqt,btkd->bqkgd', p.astype(kv.dtype), kv[:, :, 1])
    return o.reshape(B, Q, H_FOX, DH)


def _diff_attend(qpos, q, kv, kpos, tbl, lam):
    B, Q = q.shape[:2]
    T = kv.shape[1]
    k = kv[:, :, 0].reshape(B, T, KV_DIFF, 2, DD)
    v = kv[:, :, 1]
    qg = q.reshape(B, Q, KV_DIFF, G_DIFF, 2, DD)
    s = jnp.einsum('bqkgid,btkid->ibkgqt', qg, k).astype(jnp.float32)
    dist = qpos[:, None] - kpos[None, :]
    bias = jnp.moveaxis(tbl[_t5_bucket(dist)], -1, 0).reshape(KV_DIFF, G_DIFF, Q, T)
    p = _masked_softmax(s + bias, dist >= 0)
    a = (p[0] - lam * p[1]).astype(v.dtype)
    o = jnp.einsum('bkgqt,btkv->bqkgv', a, v)
    return o.reshape(B, Q, H_DIFF, 2 * DD)


def _gather_pages(pool, page_table):
    g = pool[page_table]
    return g.reshape(g.shape[0], -1, *g.shape[3:])


def _layer(x, pos, lidx, past, rel_bias, w_in, cmp_pos, cmp_w1, cmp_w2, mla_gq, mla_gkv,
           mla_wuq, mla_wuk, mla_wuv, fox_bf, diff_lam, diff_g, w_branch, w_out, ln_g, ln_b):
    B, S, _ = x.shape
    dt = x.dtype
    (a_q, a_cmp, a_sel, a_win, a_gate, a_z,
     b_cq, b_ckv, b_kpe, b_z,
     c_q, c_k, c_v, c_f, c_z,
     d_q, d_k, d_v, d_z, merge_logit) = _split(_pmatmul(x, w_in), IN_SIZES)
    tbl_a, tbl_d = rel_bias[:, :H_NSA], rel_bias[:, H_NSA:]

    def with_past(name, new):
        return new if past is None else jnp.concatenate([past[name], new], axis=1)

    q_a = a_q.reshape(B, S, H_NSA, DH) * DH ** -0.5
    cmp_new = a_cmp.reshape(B, S, 2, DH)
    sel_new = a_sel.reshape(B, S, 2, DH)
    win_new = a_win.reshape(B, S, 2, DH)
    o_cmp, sel_idx = _nsa_compressed(q_a, pos, with_past('nsa_cmp', cmp_new), cmp_pos, cmp_w1, cmp_w2, tbl_a)
    if past is None:
        gather = _local_block_gather(sel_new)
        win_all = win_new
        o_win = _window_banded(q_a, win_new, pos, tbl_a)
    else:
        gather = _paged_block_gather(past['nsa_sel_pool'], past['page_table'], sel_new)
        win_all = jnp.concatenate([past['nsa_win'], win_new], axis=1)
        kpos_w = pos[0] - past['nsa_win'].shape[1] + jnp.arange(win_all.shape[1])
        o_win = _window_attend(q_a, win_all, pos, kpos_w, tbl_a)
    win_state = win_all[:, win_all.shape[1] - min(WINDOW, win_all.shape[1]):]
    o_sel = _sweep(lambda qp, qb, ib: _nsa_selected(qp, qb, ib, gather, tbl_a), pos, q_a, sel_idx)
    g_a = jax.nn.sigmoid(a_gate.reshape(B, S, H_NSA, 3))
    o_a = (g_a[..., 0:1] * o_cmp + g_a[..., 1:2] * o_sel + g_a[..., 2:3] * o_win).reshape(B, S, BR_W)

    cq = _rmsnorm(b_cq, mla_gq)
    qf = (cq @ mla_wuq).reshape(B, S, H_MLA, D_NOPE + D_ROPE)
    q_lat = jnp.einsum('bshn,chn->bshc', qf[..., :D_NOPE], mla_wuk) * MLA_SCALE
    q_pe = _rope(qf[..., D_NOPE:], pos) * MLA_SCALE
    lat_new = _rmsnorm(b_ckv, mla_gkv)
    kpe_new = _rope(b_kpe, pos)
    lat, kpe = with_past('mla_lat', lat_new), with_past('mla_kpe', kpe_new)
    kpos_all = jnp.arange(lat.shape[1])
    o_lat = _sweep(lambda qp, ql, qr: _mla_attend(qp, ql, qr, lat, kpe, kpos_all), pos, q_lat, q_pe)
    o_b = jnp.einsum('bshc,chv->bshv', o_lat, mla_wuv).reshape(B, S, BR_W)

    q_c = c_q.reshape(B, S, H_FOX, DH) * DH ** -0.5
    fkv_new = jnp.stack([c_k.reshape(B, S, KV_FOX, DH), c_v.reshape(B, S, KV_FOX, DH)], axis=2)
    logf_new = jax.nn.log_sigmoid((c_f + fox_bf).astype(jnp.float32))
    fkv = with_past('fox_kv', fkv_new)
    logf = with_past('fox_logf', logf_new).astype(jnp.float32)
    r = lax.cumsum(logf, axis=1, reverse=True) - logf
    o_c = _sweep(lambda qp, qb, rq: _fox_attend(qp, qb, rq, fkv, r, kpos_all),
                 pos, q_c, r[:, r.shape[1] - S:]).reshape(B, S, BR_W)

    q_d = d_q.reshape(B, S, H_DIFF, 2, DD) * DD ** -0.5
    dkv_new = jnp.stack([d_k.reshape(B, S, KV_DIFF, 2 * DD), d_v.reshape(B, S, KV_DIFF, 2 * DD)], axis=2)
    dkv = with_past('diff_kv', dkv_new)
    lam_init = 0.8 - 0.6 * math.exp(-0.3 * lidx)
    lam = (jnp.exp(jnp.sum(diff_lam[0] * diff_lam[1]).astype(jnp.float32))
           - jnp.exp(jnp.sum(diff_lam[2] * diff_lam[3]).astype(jnp.float32)) + lam_init)
    o_d = _sweep(lambda qp, qb: _diff_attend(qp, qb, dkv, kpos_all, tbl_d, lam), pos, q_d)
    o_d = (_rmsnorm(o_d, diff_g, 1e-5) * (1.0 - lam_init)).reshape(B, S, BR_W)

    outs = jnp.stack([o_a * jax.nn.silu(a_z), o_b * jax.nn.silu(b_z),
                      o_c * jax.nn.silu(c_z), o_d * jax.nn.silu(d_z)], axis=2)
    branch = jnp.einsum('bsnc,ncd->bsnd', outs, w_branch)
    gates = jax.nn.sigmoid(merge_logit.reshape(B, S, N_BRANCH, D_MODEL))
    y = _pmatmul(jnp.sum(gates * branch, axis=2), w_out)
    x_new = _layernorm(DEEPNORM_ALPHA * x + y, ln_g, ln_b)
    states = (cmp_new, sel_new, win_state, lat_new, kpe_new, fkv_new, logf_new.astype(dt), dkv_new)
    return x_new, states


TQ = 128
TK = 128
LANE = 128
BF = jnp.bfloat16
F32 = jnp.float32
VMEM_LIMIT = 56 * 1024 * 1024

PG_AQ, PG_CMP, PG_SEL, PG_WIN, PG_MISC = 0, 512, 640, 768, 896
PG_CQ, PG_CKV, PG_FQ, PG_FKV, PG_DQ, PG_DKV, PG_Z, PG_END = 1024, 1280, 1536, 2048, 2304, 3328, 3584, 4608
MISC_KPE, MISC_GATE, MISC_F = 0, 32, 48
N_SMALL = N_IN - N_BRANCH * D_MODEL


def _in_offsets():
    offs, s = [], 0
    for n in IN_SIZES:
        offs.append(s)
        s += n
    return offs


def _small_proj_columns():
    import numpy as np
    (o_aq, o_cmp, o_sel, o_win, o_gate, o_az, o_cq, o_ckv, o_kpe, o_bz,
     o_fq, o_fk, o_fv, o_ff, o_cz, o_dq, o_dk, o_dv, o_dz, _) = _in_offsets()
    src = np.full((PG_END,), -1, np.int32)
    for h in range(H_NSA):
        src[PG_AQ + LANE * h: PG_AQ + LANE * h + DH] = o_aq + DH * h + np.arange(DH)
    src[PG_CMP:PG_CMP + 2 * DH] = o_cmp + np.arange(2 * DH)
    src[PG_SEL:PG_SEL + 2 * DH] = o_sel + np.arange(2 * DH)
    src[PG_WIN:PG_WIN + 2 * DH] = o_win + np.arange(2 * DH)
    src[PG_MISC + MISC_KPE: PG_MISC + MISC_KPE + D_ROPE] = o_kpe + np.arange(D_ROPE)
    src[PG_MISC + MISC_GATE: PG_MISC + MISC_GATE + 3 * H_NSA] = o_gate + np.arange(3 * H_NSA)
    src[PG_MISC + MISC_F: PG_MISC + MISC_F + H_FOX] = o_ff + np.arange(H_FOX)
    src[PG_CQ:PG_CQ + D_CQ] = o_cq + np.arange(D_CQ)
    src[PG_CKV:PG_CKV + D_C] = o_ckv + np.arange(D_C)
    for h in range(H_FOX):
        kvh = h // G_FOX
        base = PG_FQ + LANE * h + DH * kvh
        src[base: base + DH] = o_fq + DH * h + np.arange(DH)
    src[PG_FKV:PG_FKV + 2 * KV_FOX * DH] = o_fk + np.arange(2 * KV_FOX * DH)
    for kvh in range(KV_DIFF):
        for mp in range(2):
            for g in range(G_DIFF):
                h = kvh * G_DIFF + g
                r = (kvh * 2 + mp) * G_DIFF + g
                base = PG_DQ + LANE * r + 2 * DD * kvh + DD * mp
                src[base: base + DD] = o_dq + 2 * DD * h + DD * mp + np.arange(DD)
    src[PG_DKV:PG_DKV + 2 * KV_DIFF * 2 * DD] = o_dk + np.arange(2 * KV_DIFF * 2 * DD)
    src[PG_Z + 0 * BR_W: PG_Z + 1 * BR_W] = o_az + np.arange(BR_W)
    src[PG_Z + 1 * BR_W: PG_Z + 2 * BR_W] = o_bz + np.arange(BR_W)
    src[PG_Z + 2 * BR_W: PG_Z + 3 * BR_W] = o_cz + np.arange(BR_W)
    src[PG_Z + 3 * BR_W: PG_Z + 4 * BR_W] = o_dz + np.arange(BR_W)
    return src


def _permute_cols(w, src):
    import numpy as np
    g = jnp.take(w, jnp.asarray(np.maximum(src, 0)), axis=-1)
    return jnp.where(jnp.asarray(src >= 0), g, 0.0)


def _uq_columns():
    import numpy as np
    src = np.full((H_MLA * D_NOPE + H_MLA * LANE,), -1, np.int32)
    per = D_NOPE + D_ROPE
    for h in range(H_MLA):
        src[D_NOPE * h: D_NOPE * (h + 1)] = per * h + np.arange(D_NOPE)
        src[H_MLA * D_NOPE + LANE * h: H_MLA * D_NOPE + LANE * h + D_ROPE] = per * h + D_NOPE + np.arange(D_ROPE)
    return src


def _rope_tables(pos):
    half = D_ROPE // 2
    inv = ROPE_BASE ** (-jnp.arange(half, dtype=F32) / half)
    ang = pos.astype(F32)[:, None] * inv
    cos, sin = jnp.cos(ang), jnp.sin(ang)
    n = pos.shape[0]
    one = jnp.ones((n, LANE - D_ROPE), F32)
    zer = jnp.zeros((n, LANE - D_ROPE), F32)
    zh = jnp.zeros((n, half), F32)
    c = jnp.concatenate([cos, cos, one], -1)
    s1 = jnp.concatenate([zh, sin, zer], -1)
    s2 = jnp.concatenate([-sin, zh, zer], -1)
    return c, s1, s2


def _proj_kernel(x_ref, wp_ref, wuq_ref, wuk_ref, gq_ref, gkv_ref, fb_ref, c_ref, s1_ref, s2_ref,
                 qa_ref, cmp_ref, sel_ref, win_ref, selb_ref, winb_ref, misc_ref, lat_ref, kcat_ref,
                 qcat_ref, fq_ref, fkv_ref, fkvb_ref, dq_ref, dkv_ref, dkvb_ref, z_ref):
    xb = x_ref[...].astype(BF)

    def mm(a, b):
        return jnp.dot(xb, wp_ref[:, a:b], preferred_element_type=F32)

    cos, s1, s2 = c_ref[...], s1_ref[...], s2_ref[...]

    def rope(v):
        return v * cos + pltpu.roll(v, D_ROPE // 2, 1) * s1 + pltpu.roll(v, LANE - D_ROPE // 2, 1) * s2

    qa_ref[...] = (mm(PG_AQ, PG_CMP) * DH ** -0.5).astype(BF)
    cmp_ref[...] = mm(PG_CMP, PG_SEL)
    t = mm(PG_SEL, PG_WIN)
    sel_ref[...] = t
    selb_ref[...] = t.astype(BF)
    t = mm(PG_WIN, PG_MISC)
    win_ref[...] = t
    winb_ref[...] = t.astype(BF)

    misc = mm(PG_MISC, PG_CQ)
    lane = lax.broadcasted_iota(jnp.int32, misc.shape, 1)
    roped = rope(misc)
    xf = misc + fb_ref[...]
    logf = jnp.minimum(xf, 0.0) - jnp.log1p(jnp.exp(-jnp.abs(xf)))
    misc_ref[...] = jnp.where((lane >= MISC_F) & (lane < MISC_F + H_FOX), logf, roped)

    cq = mm(PG_CQ, PG_CKV)
    cq = cq * lax.rsqrt(jnp.mean(cq * cq, -1, keepdims=True) + 1e-6) * gq_ref[...]
    qf = jnp.dot(cq.astype(BF), wuq_ref[...], preferred_element_type=F32)
    nn = H_MLA * D_NOPE
    ql = jnp.dot(qf[:, :nn].astype(BF), wuk_ref[...], preferred_element_type=F32) * MLA_SCALE
    wq = D_C + LANE
    for h in range(H_MLA):
        qcat_ref[:, wq * h: wq * h + D_C] = ql[:, D_C * h: D_C * (h + 1)].astype(BF)
        pe = rope(qf[:, nn + LANE * h: nn + LANE * (h + 1)]) * MLA_SCALE
        qcat_ref[:, wq * h + D_C: wq * (h + 1)] = pe.astype(BF)

    ckv = mm(PG_CKV, PG_FQ)
    lat = ckv * lax.rsqrt(jnp.mean(ckv * ckv, -1, keepdims=True) + 1e-6) * gkv_ref[...]
    lat_ref[...] = lat
    kcat_ref[:, :D_C] = lat.astype(BF)
    kcat_ref[:, D_C:] = jnp.where(lane < D_ROPE, roped, 0.0).astype(BF)

    fq_ref[...] = (mm(PG_FQ, PG_FKV) * DH ** -0.5).astype(BF)
    t = mm(PG_FKV, PG_DQ)
    fkv_ref[...] = t
    fkvb_ref[...] = t.astype(BF)
    dq_ref[...] = (mm(PG_DQ, PG_DKV) * DD ** -0.5).astype(BF)
    t = mm(PG_DKV, PG_Z)
    dkv_ref[...] = t
    dkvb_ref[...] = t.astype(BF)
    z_ref[...] = mm(PG_Z, PG_END)


def _proj(x2, wp, wuq, wukbd, gq, gkv, fb, tabs, tm):
    M = x2.shape[0]
    npos = tabs[0].shape[0] // tm
    row = lambda i: (i, 0)
    cst = lambda i: (0, 0)
    tab = lambda i: (i % npos, 0)
    widths = [(512, BF), (128, F32), (128, F32), (128, F32), (128, BF), (128, BF), (128, F32), (D_C, F32),
              (D_C + LANE, BF), (H_MLA * (D_C + LANE), BF), (512, BF), (256, F32), (256, BF), (1024, BF),
              (256, F32), (256, BF), (1024, F32)]
    return pl.pallas_call(
        _proj_kernel,
        grid=(M // tm,),
        in_specs=[pl.BlockSpec((tm, D_MODEL), row),
                  pl.BlockSpec(wp.shape, cst), pl.BlockSpec(wuq.shape, cst), pl.BlockSpec(wukbd.shape, cst),
                  pl.BlockSpec((1, D_CQ), cst), pl.BlockSpec((1, D_C), cst), pl.BlockSpec((1, LANE), cst),
                  pl.BlockSpec((tm, LANE), tab), pl.BlockSpec((tm, LANE), tab), pl.BlockSpec((tm, LANE), tab)],
        out_specs=[pl.BlockSpec((tm, w), row) for w, _ in widths],
        out_shape=[jax.ShapeDtypeStruct((M, w), d) for w, d in widths],
        compiler_params=pltpu.CompilerParams(dimension_semantics=("arbitrary",), vmem_limit_bytes=VMEM_LIMIT),
        name="proj",
    )(x2, wp, wuq, wukbd, gq, gkv, fb, *tabs)


def _cmp_summary_kernel(x_ref, pos_ref, wc_ref, w2_ref, o_ref):
    xb = (x_ref[...] + pos_ref[...]).astype(BF)
    hid = jnp.dot(xb, wc_ref[...], preferred_element_type=F32)
    act = hid * jax.nn.sigmoid(hid)
    o_ref[...] = jnp.dot(act.astype(BF), w2_ref[...], preferred_element_type=F32)


def _cmp_summary(x3, layer, pos_row, wc, w2, tb):
    _, N, W = x3.shape
    return pl.pallas_call(
        _cmp_summary_kernel,
        grid=(N // tb,),
        in_specs=[pl.BlockSpec((None, tb, W), lambda i: (layer, i, 0)),
                  pl.BlockSpec((1, W), lambda i: (0, 0)),
                  pl.BlockSpec(wc.shape, lambda i: (0, 0)),
                  pl.BlockSpec(w2.shape, lambda i: (0, 0))],
        out_specs=pl.BlockSpec((tb, LANE), lambda i: (i, 0)),
        out_shape=jax.ShapeDtypeStruct((N, LANE), F32),
        compiler_params=pltpu.CompilerParams(dimension_semantics=("arbitrary",), vmem_limit_bytes=VMEM_LIMIT),
        name="cmp_summary",
    )(x3, pos_row, wc, w2)


def _split3(x):
    hi = x.astype(BF)
    r = x - hi.astype(F32)
    mid = r.astype(BF)
    lo = (r - mid.astype(F32)).astype(BF)
    return hi, mid, lo


def _dot3(x, u):
    hi, mid, lo = _split3(x)
    d = lambda a: jnp.dot(a, u, preferred_element_type=F32)
    return d(hi) + d(mid) + d(lo)


def _prefix_kernel(x_ref, u_ref, o_ref):
    u = u_ref[...]
    carry = jnp.zeros((x_ref.shape[0], 1), F32)
    for c in range(x_ref.shape[1] // LANE):
        p = _dot3(x_ref[:, LANE * c: LANE * (c + 1)], u)
        o_ref[:, LANE * c: LANE * (c + 1)] = p + carry
        carry = carry + p[:, LANE - 1: LANE]


def _prefix(x, u):
    return pl.pallas_call(_prefix_kernel, out_shape=jax.ShapeDtypeStruct(x.shape, F32), name="fox_prefix")(x, u)


def _online(s, v, m_ref, l_ref, acc_ref):
    m_prev = m_ref[...]
    m_new = jnp.maximum(m_prev, jnp.max(s, -1, keepdims=True))
    alpha = jnp.exp(m_prev - m_new)
    p = jnp.exp(s - m_new)
    l_ref[...] = alpha * l_ref[...] + jnp.sum(p, -1, keepdims=True)
    acc_ref[...] = alpha * acc_ref[...] + jnp.dot(p.astype(BF), v, preferred_element_type=F32)
    m_ref[...] = m_new


def _init_state(m_ref, l_ref, acc_ref):
    m_ref[...] = jnp.full(m_ref.shape, NEG_INF, F32)
    l_ref[...] = jnp.zeros(l_ref.shape, F32)
    acc_ref[...] = jnp.zeros(acc_ref.shape, F32)


def _qk(q, k):
    return lax.dot_general(q, k, (((1,), (1,)), ((), ())), preferred_element_type=F32)


def _causal_tile(rows):
    r = lax.broadcasted_iota(jnp.int32, (rows, TK), 0) & (TQ - 1)
    c = lax.broadcasted_iota(jnp.int32, (rows, TK), 1)
    return c <= r


def _mla_kernel(q_ref, k_ref, o_ref, qs, m_ref, l_ref, acc_ref):
    qb = pl.program_id(1)
    wq = D_C + LANE
    for h in range(H_MLA):
        qs[TQ * h: TQ * (h + 1), :] = q_ref[:, wq * h: wq * (h + 1)]
    _init_state(m_ref, l_ref, acc_ref)

    def body(kb, c):
        k = k_ref[pl.ds(pl.multiple_of(kb * TK, TK), TK), :]
        _online(_qk(qs[...], k), k[:, :D_C], m_ref, l_ref, acc_ref)
        return c

    lax.fori_loop(0, qb, body, 0)
    k = k_ref[pl.ds(pl.multiple_of(qb * TK, TK), TK), :]
    s = jnp.where(_causal_tile(H_MLA * TQ), _qk(qs[...], k), NEG_INF)
    _online(s, k[:, :D_C], m_ref, l_ref, acc_ref)
    for h in range(H_MLA):
        blk = slice(TQ * h, TQ * (h + 1))
        o_ref[:, D_C * h: D_C * (h + 1)] = acc_ref[blk, :] / l_ref[blk, :]


def _fox_kernel(q_ref, kv_ref, pq_ref, pk_ref, o_ref, qs, m_ref, l_ref, acc_ref):
    qb = pl.program_id(1)
    for h in range(H_FOX):
        qs[TQ * h: TQ * (h + 1), :] = q_ref[:, LANE * h: LANE * (h + 1)]
    _init_state(m_ref, l_ref, acc_ref)
    pq = pq_ref[...]

    def decay(kb):
        pk = pk_ref[kb]
        return jnp.concatenate([pq[:, h: h + 1] - pk[h: h + 1, :] for h in range(H_FOX)], 0)

    def body(kb, c):
        kv = kv_ref[pl.ds(pl.multiple_of(kb * TK, TK), TK), :]
        s = _qk(qs[...], kv[:, :LANE]) + decay(kb)
        _online(s, kv[:, LANE:], m_ref, l_ref, acc_ref)
        return c

    lax.fori_loop(0, qb, body, 0)
    kv = kv_ref[pl.ds(pl.multiple_of(qb * TK, TK), TK), :]
    s = jnp.where(_causal_tile(H_FOX * TQ), _qk(qs[...], kv[:, :LANE]) + decay(qb), NEG_INF)
    _online(s, kv[:, LANE:], m_ref, l_ref, acc_ref)
    for h in range(H_FOX):
        blk = slice(TQ * h, TQ * (h + 1))
        kvh = h // G_FOX
        o_ref[:, DH * h: DH * (h + 1)] = acc_ref[blk, DH * kvh: DH * (kvh + 1)] / l_ref[blk, :]


def _diff_rows():
    out = []
    for kvh in range(KV_DIFF):
        for mp in range(2):
            for g in range(G_DIFF):
                out.append((kvh * G_DIFF + g, mp))
    return out


def _diff_kernel(q_ref, kv_ref, tb_ref, dl_ref, lam0_ref, o_ref, qs, m_ref, l_ref, acc_ref):
    qb = pl.program_id(1)
    rows = _diff_rows()
    for r in range(len(rows)):
        qs[TQ * r: TQ * (r + 1), :] = q_ref[:, LANE * r: LANE * (r + 1)]
    _init_state(m_ref, l_ref, acc_ref)

    def bias(d):
        return jnp.concatenate([tb_ref[h, d] for h, _ in rows], 0)

    def body(kb, c):
        kv = kv_ref[pl.ds(pl.multiple_of(kb * TK, TK), TK), :]
        s = _qk(qs[...], kv[:, :LANE]) + bias(jnp.minimum(qb - kb, 2))
        _online(s, kv[:, LANE:], m_ref, l_ref, acc_ref)
        return c

    lax.fori_loop(0, qb, body, 0)
    kv = kv_ref[pl.ds(pl.multiple_of(qb * TK, TK), TK), :]
    s = jnp.where(_causal_tile(len(rows) * TQ), _qk(qs[...], kv[:, :LANE]) + bias(0), NEG_INF)
    _online(s, kv[:, LANE:], m_ref, l_ref, acc_ref)

    dl = dl_ref[...]
    lam = (jnp.exp(jnp.sum(dl[0:1] * dl[1:2], -1, keepdims=True))
           - jnp.exp(jnp.sum(dl[2:3] * dl[3:4], -1, keepdims=True)) + lam0_ref[...])
    for kvh in range(KV_DIFF):
        for g in range(G_DIFF):
            h = kvh * G_DIFF + g
            b0 = slice(TQ * rows.index((h, 0)), TQ * (rows.index((h, 0)) + 1))
            b1 = slice(TQ * rows.index((h, 1)), TQ * (rows.index((h, 1)) + 1))
            vs = slice(2 * DD * kvh, 2 * DD * (kvh + 1))
            o0 = acc_ref[b0, vs] / l_ref[b0, :]
            o1 = acc_ref[b1, vs] / l_ref[b1, :]
            o_ref[:, 2 * DD * h: 2 * DD * (h + 1)] = o0 - lam * o1


def _nsa_kernel(q_ref, kvc_ref, sel_ref, win_ref, cb_ref, tb_ref, o_ref, qs, m_ref, l_ref, acc_ref, smask):
    qb = pl.program_id(1)
    R = H_NSA * TQ
    nkb = smask.shape[0]
    n_sel = nkb * (TK // SEL_BLK)
    n_pair = kvc_ref.shape[1]
    qs[...] = jnp.zeros(qs.shape, BF)
    for h in range(H_NSA):
        qs[TQ * h: TQ * (h + 1), :] = q_ref[:, LANE * h: LANE * (h + 1)]
    q = qs[...]

    ke = kvc_ref[0].astype(BF)
    ko = kvc_ref[1].astype(BF)
    ri = lax.broadcasted_iota(jnp.int32, (R, n_pair), 0) & (TQ - 1)
    li = lax.broadcasted_iota(jnp.int32, (R, n_pair), 1)
    qpos = qb * TQ + ri
    me = qpos >= SEL_BLK * li + (CMP_BLK - 1)
    mo = qpos >= SEL_BLK * li + (SEL_BLK - 1)
    se = jnp.where(me, _qk(q, ke) + cb_ref[0], NEG_INF)
    so = jnp.where(mo, _qk(q, ko) + cb_ref[1], NEG_INF)
    mx = jnp.maximum(jnp.max(se, -1, keepdims=True), jnp.max(so, -1, keepdims=True))
    ee = jnp.where(me, jnp.exp(se - mx), 0.0)
    eo = jnp.where(mo, jnp.exp(so - mx), 0.0)
    den = jnp.maximum(jnp.sum(ee, -1, keepdims=True) + jnp.sum(eo, -1, keepdims=True), 1e-30)
    pe = ee / den
    po = eo / den
    ocmp = (jnp.dot(pe.astype(BF), ke, preferred_element_type=F32)
            + jnp.dot(po.astype(BF), ko, preferred_element_type=F32))
    for h in range(H_NSA):
        o_ref[:, DH * h: DH * (h + 1)] = ocmp[TQ * h: TQ * (h + 1), DH:]
    pp = pe + po
    imp = pp[0:TQ]
    for h in range(1, H_NSA):
        imp = imp + pp[TQ * h: TQ * (h + 1)]

    bi = lax.broadcasted_iota(jnp.int32, (TQ, n_pair), 1)
    qi = qb * TQ + lax.broadcasted_iota(jnp.int32, (TQ, n_pair), 0)
    cur = lax.shift_right_logical(qi, int(math.log2(SEL_BLK)))
    valid = bi <= cur
    forced = valid & ((bi == 0) | (cur - bi < N_LOCAL))
    score = jnp.where(valid, jnp.where(forced, FORCE, imp), -FORCE)
    rank = jnp.zeros((TQ, n_pair), F32)
    for i in range(n_sel):
        col = score[:, i: i + 1]
        ahead = (col > score) | ((col == score) & (bi > i))
        rank = rank + jnp.where(ahead, 1.0, 0.0)
    madd = jnp.where(rank < TOP_N, 0.0, NEG_INF)
    lane = lax.broadcasted_iota(jnp.int32, (TQ, TK), 1)
    for kb in range(nkb):
        smask[kb] = jnp.where(lane < SEL_BLK, madd[:, 2 * kb: 2 * kb + 1], madd[:, 2 * kb + 1: 2 * kb + 2])

    def bias(d):
        return jnp.concatenate([tb_ref[h, d] for h in range(H_NSA)], 0)

    _init_state(m_ref, l_ref, acc_ref)

    def sel_body(kb, c):
        kv = sel_ref[pl.ds(pl.multiple_of(kb * TK, TK), TK), :]
        sm = smask[kb]
        s = _qk(q, kv) + bias(jnp.minimum(qb - kb, 2)) + jnp.concatenate([sm] * H_NSA, 0)
        _online(s, kv, m_ref, l_ref, acc_ref)
        return c

    lax.fori_loop(0, qb, sel_body, 0)
    kv = sel_ref[pl.ds(pl.multiple_of(qb * TK, TK), TK), :]
    s = _qk(q, kv) + bias(0) + jnp.concatenate([smask[qb]] * H_NSA, 0)
    _online(jnp.where(_causal_tile(R), s, NEG_INF), kv, m_ref, l_ref, acc_ref)
    for h in range(H_NSA):
        blk = slice(TQ * h, TQ * (h + 1))
        o_ref[:, BR_W + DH * h: BR_W + DH * (h + 1)] = acc_ref[blk, DH:] / l_ref[blk, :]

    _init_state(m_ref, l_ref, acc_ref)
    rr = lax.broadcasted_iota(jnp.int32, (R, TK), 0) & (TQ - 1)
    cc = lax.broadcasted_iota(jnp.int32, (R, TK), 1)

    def win_body(kb, c):
        kv = win_ref[pl.ds(pl.multiple_of(kb * TK, TK), TK), :]
        dist = (qb - kb) * TQ + rr - cc
        s = _qk(q, kv) + bias(jnp.minimum(qb - kb, 2))
        _online(jnp.where(dist <= WINDOW, s, NEG_INF), kv, m_ref, l_ref, acc_ref)
        return c

    lax.fori_loop(jnp.maximum(qb - WINDOW // TK, 0), qb, win_body, 0)
    kv = win_ref[pl.ds(pl.multiple_of(qb * TK, TK), TK), :]
    s = _qk(q, kv) + bias(0)
    _online(jnp.where(_causal_tile(R), s, NEG_INF), kv, m_ref, l_ref, acc_ref)
    for h in range(H_NSA):
        blk = slice(TQ * h, TQ * (h + 1))
        o_ref[:, 2 * BR_W + DH * h: 2 * BR_W + DH * (h + 1)] = acc_ref[blk, DH:] / l_ref[blk, :]


def _attn_call(kern, B, S, ins, specs, out_w, scratch, name):
    nq = S // TQ
    return pl.pallas_call(
        kern,
        grid=(B, nq),
        in_specs=specs,
        out_specs=pl.BlockSpec((TQ, out_w), lambda b, i: (b * nq + i, 0)),
        out_shape=jax.ShapeDtypeStruct((B * S, out_w), F32),
        scratch_shapes=scratch,
        compiler_params=pltpu.CompilerParams(dimension_semantics=("arbitrary", "arbitrary"),
                                             vmem_limit_bytes=VMEM_LIMIT),
        name=name,
    )(*ins)


def _state_scratch(rows, qw, dv):
    return [pltpu.VMEM((rows, qw), BF), pltpu.VMEM((rows, 1), F32), pltpu.VMEM((rows, 1), F32),
            pltpu.VMEM((rows, dv), F32)]


def _post_kernel(x_ref, olat_ref, onsa_ref, ofox_ref, odiff_ref, misc_ref, z_ref, wuv_ref, dg_ref, ds_ref,
                 wbr_ref, wm_ref, wo_ref, lg_ref, lb_ref, o_ref):
    x = x_ref[...]
    xb = x.astype(BF)
    gate = jax.nn.sigmoid(misc_ref[:, MISC_GATE: MISC_GATE + 3 * H_NSA])
    onsa = onsa_ref[...]
    pieces = []
    for h in range(H_NSA):
        hs = slice(DH * h, DH * (h + 1))
        pieces.append(gate[:, 3 * h: 3 * h + 1] * onsa[:, hs]
                      + gate[:, 3 * h + 1: 3 * h + 2] * onsa[:, BR_W + DH * h: BR_W + DH * (h + 1)]
                      + gate[:, 3 * h + 2: 3 * h + 3] * onsa[:, 2 * BR_W + DH * h: 2 * BR_W + DH * (h + 1)])
    oa = jnp.concatenate(pieces, -1)
    ob = jnp.dot(olat_ref[...].astype(BF), wuv_ref[...], preferred_element_type=F32)
    oc = ofox_ref[...]
    od_raw = odiff_ref[...]
    pieces = []
    for h in range(H_DIFF):
        v = od_raw[:, 2 * DD * h: 2 * DD * (h + 1)]
        pieces.append(v * lax.rsqrt(jnp.mean(v * v, -1, keepdims=True) + 1e-5))
    od = jnp.concatenate(pieces, -1) * dg_ref[...] * ds_ref[...]
    z = z_ref[...]
    zs = z * jax.nn.sigmoid(z)
    acc = jnp.zeros(x.shape, F32)
    for n, o in enumerate((oa, ob, oc, od)):
        on = (o * zs[:, BR_W * n: BR_W * (n + 1)]).astype(BF)
        br = jnp.dot(on, wbr_ref[n], preferred_element_type=F32)
        g = jax.nn.sigmoid(jnp.dot(xb, wm_ref[:, D_MODEL * n: D_MODEL * (n + 1)], preferred_element_type=F32))
        acc = acc + g * br
    y = jnp.dot(acc.astype(BF), wo_ref[...], preferred_element_type=F32)
    t = DEEPNORM_ALPHA * x + y
    mu = jnp.mean(t, -1, keepdims=True)
    var = jnp.mean(jnp.square(t - mu), -1, keepdims=True)
    o_ref[...] = (t - mu) * lax.rsqrt(var + 1e-5) * lg_ref[...] + lb_ref[...]


def _post(x2, olat, onsa, ofox, odiff, misc, z, wuv, dg, ds, wbr, wm, wo, lg, lb, tm):
    M = x2.shape[0]
    row = lambda i: (i, 0)
    cst2 = lambda i: (0, 0)
    cst3 = lambda i: (0, 0, 0)
    acts = (x2, olat, onsa, ofox, odiff, misc, z)
    return pl.pallas_call(
        _post_kernel,
        grid=(M // tm,),
        in_specs=[pl.BlockSpec((tm, a.shape[1]), row) for a in acts]
        + [pl.BlockSpec(wuv.shape, cst2), pl.BlockSpec(dg.shape, cst2), pl.BlockSpec(ds.shape, cst2),
           pl.BlockSpec(wbr.shape, cst3), pl.BlockSpec(wm.shape, cst2), pl.BlockSpec(wo.shape, cst2),
           pl.BlockSpec(lg.shape, cst2), pl.BlockSpec(lb.shape, cst2)],
        out_specs=pl.BlockSpec((tm, D_MODEL), row),
        out_shape=jax.ShapeDtypeStruct((M, D_MODEL), F32),
        compiler_params=pltpu.CompilerParams(dimension_semantics=("arbitrary",), vmem_limit_bytes=VMEM_LIMIT),
        name="post_merge",
    )(*acts, wuv, dg, ds, wbr, wm, wo, lg, lb)


def _layer_weights(l, w_in, cmp_pos, cmp_w1, cmp_w2, mla_gq, mla_gkv, mla_wuq, mla_wuk, mla_wuv,
                   fox_bf, diff_lam, diff_g, w_branch, w_out, ln_g, ln_b):
    import numpy as np
    wp = _permute_cols(w_in[l, :, :N_SMALL], _small_proj_columns()).astype(BF)
    wm = w_in[l, :, N_SMALL:].astype(BF)
    wuq = _permute_cols(mla_wuq[l], _uq_columns()).astype(BF)
    eye = jnp.eye(H_MLA, dtype=F32)
    wukbd = jnp.einsum('chn,hg->hngc', mla_wuk[l], eye).reshape(H_MLA * D_NOPE, H_MLA * D_C).astype(BF)
    wuvbd = jnp.einsum('chv,hg->hcgv', mla_wuv[l], eye).reshape(H_MLA * D_C, H_MLA * D_V).astype(BF)
    fb = jnp.zeros((1, LANE), F32).at[0, MISC_F: MISC_F + H_FOX].set(fox_bf[l])
    w1 = cmp_w1[l].reshape(2, CMP_BLK, DH, DH)
    wc = jnp.einsum('cldh,ce->lcdeh', w1, jnp.eye(2, dtype=F32)).reshape(CMP_BLK * 2 * DH, 2 * DH).astype(BF)
    w2 = jnp.einsum('che,cf->chfe', cmp_w2[l], jnp.eye(2, dtype=F32)).reshape(2 * DH, 2 * DH).astype(BF)
    cpos = jnp.swapaxes(cmp_pos[l], 0, 1).reshape(1, CMP_BLK * 2 * DH)
    lam_init = 0.8 - 0.6 * math.exp(-0.3 * l)
    return dict(
        wp=wp, wm=wm, wuq=wuq, wukbd=wukbd, wuvbd=wuvbd, fb=fb, wc=wc, w2=w2, cpos=cpos,
        gq=mla_gq[l][None], gkv=mla_gkv[l][None], dl=diff_lam[l],
        lam0=jnp.full((1, 1), lam_init, F32), ds=jnp.full((1, 1), 1.0 - lam_init, F32),
        dg=jnp.tile(diff_g[l], H_DIFF)[None], wbr=w_branch[l].astype(BF), wo=w_out[l].astype(BF),
        lg=ln_g[l][None], lb=ln_b[l][None])


def _prompt_bias_tables(rel_bias, S):
    nq = S // TQ
    i = jnp.arange(TQ)[:, None]
    j = jnp.arange(TK)[None, :]
    tiles = jnp.stack([rel_bias[_t5_bucket(d * TQ + i - j)] for d in range(3)])
    tb = jnp.moveaxis(tiles, -1, 0)
    n_cmp = S // CMP_BLK
    ends = (jnp.arange(n_cmp) + 1) * CMP_BLK - 1
    dist = jnp.arange(S)[:, None] - ends[None, :]
    cb = rel_bias[:, :H_NSA][_t5_bucket(dist)]
    cb = cb.reshape(nq, TQ, n_cmp // 2, 2, H_NSA)
    cb = jnp.transpose(cb, (0, 3, 4, 1, 2)).reshape(nq, 2, H_NSA * TQ, n_cmp // 2)
    return tb[:H_NSA], tb[H_NSA:], cb


def _prompt_layer(x, lw, tabs, tb_a, tb_d, cb, u_tri):
    B, S, _ = x.shape
    M = B * S
    nq = S // TQ
    x2 = x.reshape(M, D_MODEL)
    (qa, cmp_s, sel_s, win_s, selb, winb, misc, lat, kcat, qcat, fq, fkv, fkvb, dq, dkv, dkvb, z) = _proj(
        x2, lw['wp'], lw['wuq'], lw['wukbd'], lw['gq'], lw['gkv'], lw['fb'], tabs, 256)

    n_cmp = S // CMP_BLK
    kvc = _cmp_summary(cmp_s.reshape(1, M // CMP_BLK, CMP_BLK * 2 * DH), 0, lw['cpos'], lw['wc'], lw['w2'],
                       min(256, M // CMP_BLK))
    kvc = jnp.swapaxes(kvc.reshape(B, n_cmp // 2, 2, LANE), 1, 2)

    qrow = lambda w: pl.BlockSpec((TQ, w), lambda b, i: (b * nq + i, 0))
    kvrow = lambda w: pl.BlockSpec((S, w), lambda b, i: (b, 0))
    whole = lambda a: pl.BlockSpec(a.shape, lambda b, i: (0,) * a.ndim)

    onsa = _attn_call(
        _nsa_kernel, B, S, (qa, kvc, selb, winb, cb, tb_a),
        [qrow(512), pl.BlockSpec((None, 2, n_cmp // 2, LANE), lambda b, i: (b, 0, 0, 0)), kvrow(LANE), kvrow(LANE),
         pl.BlockSpec((None, 2, H_NSA * TQ, n_cmp // 2), lambda b, i: (i, 0, 0, 0)), whole(tb_a)],
        3 * BR_W, _state_scratch(H_NSA * TQ, LANE, LANE) + [pltpu.VMEM((S // TK, TQ, TK), F32)], "nsa_prompt")

    olat = _attn_call(_mla_kernel, B, S, (qcat, kcat), [qrow(H_MLA * (D_C + LANE)), kvrow(D_C + LANE)],
                      H_MLA * D_C, _state_scratch(H_MLA * TQ, D_C + LANE, D_C), "mla_prompt")

    logf = misc[:, MISC_F: MISC_F + H_FOX].reshape(B, S, H_FOX)
    pre = _prefix(jnp.swapaxes(logf, 1, 2).reshape(B * H_FOX, S), u_tri).reshape(B, H_FOX, S)
    pq = jnp.swapaxes(pre, 1, 2).reshape(M, H_FOX)
    pk = jnp.swapaxes(pre.reshape(B, H_FOX, S // TK, TK), 1, 2)
    pk = jnp.pad(pk, ((0, 0), (0, 0), (0, 8 - H_FOX), (0, 0)))
    ofox = _attn_call(
        _fox_kernel, B, S, (fq, fkvb, pq, pk),
        [qrow(512), kvrow(256), qrow(H_FOX), pl.BlockSpec((None, S // TK, 8, TK), lambda b, i: (b, 0, 0, 0))],
        BR_W, _state_scratch(H_FOX * TQ, LANE, LANE), "fox_prompt")

    odiff = _attn_call(
        _diff_kernel, B, S, (dq, dkvb, tb_d, lw['dl'], lw['lam0']),
        [qrow(1024), kvrow(256), whole(tb_d), whole(lw['dl']), whole(lw['lam0'])],
        BR_W, _state_scratch(2 * H_DIFF * TQ, LANE, LANE), "diff_prompt")

    x_new = _post(x2, olat, onsa, ofox, odiff, misc, z, lw['wuvbd'], lw['dg'], lw['ds'], lw['wbr'], lw['wm'],
                  lw['wo'], lw['lg'], lw['lb'], 256).reshape(B, S, D_MODEL)
    wlen = min(WINDOW, S)
    states = (cmp_s.reshape(B, S, 2, DH), sel_s.reshape(B, S, 2, DH),
              win_s.reshape(B, S, 2, DH)[:, S - wlen:],
              lat.reshape(B, S, D_C), misc[:, MISC_KPE: MISC_KPE + D_ROPE].reshape(B, S, D_ROPE),
              fkv.reshape(B, S, 2, KV_FOX, DH), logf, dkv.reshape(B, S, 2, KV_DIFF, 2 * DD))
    return x_new, states


PG = 4
GATHER_G = 16


def _gather_rows_kernel(pt_ref, *refs):
    o_ref = refs[-1]
    for g, r in enumerate(refs[:-1]):
        o_ref[g: g + 1, :] = r[...]


def _gather_rows(pool3, page_table, gg):
    DB, NP = page_table.shape
    W = pool3.shape[-1]
    return pl.pallas_call(
        _gather_rows_kernel,
        grid_spec=pltpu.PrefetchScalarGridSpec(
            num_scalar_prefetch=1, grid=(DB, NP // gg),
            in_specs=[pl.BlockSpec((None, 1, W), lambda b, t, pt, g=g: (pt[b, t * gg + g], 0, 0))
                      for g in range(gg)],
            out_specs=pl.BlockSpec((None, gg, W), lambda b, t, pt: (b, t, 0))),
        out_shape=jax.ShapeDtypeStruct((DB, NP, W), F32),
        compiler_params=pltpu.CompilerParams(dimension_semantics=("arbitrary", "arbitrary")),
        name="gather_summaries",
    )(page_table, *([pool3] * gg))


def _suffix_kernel(x_ref, ugt_ref, one_ref, sfx_ref, tot_ref):
    x = x_ref[...]
    sfx_ref[...] = _dot3(x, ugt_ref[...])
    tot_ref[...] = _dot3(x, one_ref[...])


def _page_suffix(x, tr):
    R = x.shape[0]
    ugt = (jnp.arange(LANE)[:, None] > jnp.arange(LANE)[None, :]).astype(BF)
    one = jnp.ones((LANE, LANE), BF)
    return pl.pallas_call(
        _suffix_kernel,
        grid=(R // tr,),
        in_specs=[pl.BlockSpec((tr, LANE), lambda i: (i, 0)), pl.BlockSpec((LANE, LANE), lambda i: (0, 0)),
                  pl.BlockSpec((LANE, LANE), lambda i: (0, 0))],
        out_specs=[pl.BlockSpec((tr, LANE), lambda i: (i, 0))] * 2,
        out_shape=[jax.ShapeDtypeStruct((R, LANE), F32)] * 2,
        compiler_params=pltpu.CompilerParams(dimension_semantics=("arbitrary",)),
        name="fox_page_suffix",
    )(x, ugt, one)


def _online_multi(tiles, m_ref, l_ref, acc_ref):
    m_prev = m_ref[...]
    m_new = m_prev
    for s, _ in tiles:
        m_new = jnp.maximum(m_new, jnp.max(s, -1, keepdims=True))
    alpha = jnp.exp(m_prev - m_new)
    l_new = alpha * l_ref[...]
    acc = alpha * acc_ref[...]
    for s, v in tiles:
        p = jnp.exp(s - m_new)
        l_new = l_new + jnp.sum(p, -1, keepdims=True)
        acc = acc + jnp.dot(p.astype(BF), v, preferred_element_type=F32)
    l_ref[...] = l_new
    acc_ref[...] = acc
    m_ref[...] = m_new


def _nsa_sample_kernel(dq, n_win, q_ref, kvc_ref, cb_ref, win_ref, wt_ref, wb1_ref, wb2_ref, o_ref, sm_ref):
    q = q_ref[...]
    rows = q.shape[0]
    n_pair = kvc_ref.shape[1]
    ke = kvc_ref[0].astype(BF)
    ko = kvc_ref[1].astype(BF)
    se = _qk(q, ke) + cb_ref[0]
    so = _qk(q, ko) + cb_ref[1]
    mx = jnp.maximum(jnp.max(se, -1, keepdims=True), jnp.max(so, -1, keepdims=True))
    ee = jnp.exp(se - mx)
    eo = jnp.exp(so - mx)
    den = jnp.maximum(jnp.sum(ee, -1, keepdims=True) + jnp.sum(eo, -1, keepdims=True), 1e-30)
    pe = ee / den
    po = eo / den
    o_ref[:, :LANE] = (jnp.dot(pe.astype(BF), ke, preferred_element_type=F32)
                       + jnp.dot(po.astype(BF), ko, preferred_element_type=F32))
    pp = pe + po
    a = pp[0:8] + pp[8:16]
    imp = a + pltpu.roll(a, dq, 0)
    bi = lax.broadcasted_iota(jnp.int32, (8, n_pair), 1)
    forced = (bi == 0) | (n_pair - bi < N_LOCAL)
    score = jnp.where(forced, FORCE, imp)
    rank = jnp.where(forced, 0.0, 1.0)
    for i in range(n_pair):
        col = score[:, i: i + 1]
        ahead = (col > score) | ((col == score) & (bi > i))
        rank = rank + jnp.where(ahead, 1.0, 0.0)
    madd = jnp.where(rank < TOP_N, 0.0, NEG_INF)
    sm_ref[...] = jnp.concatenate([madd] * (rows // 8), 0)

    wp = win_ref[...].astype(BF)
    wt = wt_ref[...]
    qi1 = lax.broadcasted_iota(jnp.int32, (rows, n_win), 0) & (dq - 1)
    t1 = lax.broadcasted_iota(jnp.int32, (rows, n_win), 1)
    ok1 = n_win + qi1 - t1 <= WINDOW
    qi2 = lax.broadcasted_iota(jnp.int32, (rows, LANE), 0) & (dq - 1)
    ok2 = lax.broadcasted_iota(jnp.int32, (rows, LANE), 1) <= qi2
    s1 = jnp.where(ok1, _qk(q, wp) + wb1_ref[...], NEG_INF)
    s2 = jnp.where(ok2, _qk(q, wt) + wb2_ref[...], NEG_INF)
    mx = jnp.maximum(jnp.max(s1, -1, keepdims=True), jnp.max(s2, -1, keepdims=True))
    e1 = jnp.where(ok1, jnp.exp(s1 - mx), 0.0)
    e2 = jnp.where(ok2, jnp.exp(s2 - mx), 0.0)
    den = jnp.maximum(jnp.sum(e1, -1, keepdims=True) + jnp.sum(e2, -1, keepdims=True), 1e-30)
    o_ref[:, LANE:] = (jnp.dot((e1 / den).astype(BF), wp, preferred_element_type=F32)
                       + jnp.dot((e2 / den).astype(BF), wt, preferred_element_type=F32))


def _stream_kernel(dq, nsteps, pt_ref, qml_ref, qmp_ref, qf_ref, qd_ref, qs_ref, sm_ref, *rest):
    npg = 7 * PG
    pages = rest[:npg]
    (tl_ref, tf_ref, td_ref, ts_ref, tlf_ref, dbl_ref, dbt_ref, dbc_ref, sbl_ref, sbt_ref, sbc_ref,
     dl_ref, lam0_ref) = rest[npg: npg + 13]
    om_ref, of_ref, od_ref, os_ref = rest[npg + 13: npg + 17]
    (mm, lm, am, mf, lf, af, md, ld, ad, ms, ls, as_, carry) = rest[npg + 17:]
    t = pl.program_id(1)

    @pl.when(t == 0)
    def _():
        _init_state(mm, lm, am)
        _init_state(mf, lf, af)
        _init_state(md, ld, ad)
        _init_state(ms, ls, as_)
        carry[...] = jnp.zeros(carry.shape, F32)

    qml, qmp, qf, qd, qs = qml_ref[...], qmp_ref[...], qf_ref[...], qd_ref[...], qs_ref[...]
    tlf = tlf_ref[...]
    cs = [tlf[:, 0:1]]
    for j in range(1, dq):
        cs.append(cs[-1] + tlf[:, j: j + 1])
    qrow = lax.broadcasted_iota(jnp.int32, (qf.shape[0], 1), 0) & (dq - 1)
    cq = cs[dq - 1]
    for j in range(dq - 2, -1, -1):
        cq = jnp.where(qrow == j, cs[j], cq)

    lane = lax.broadcasted_iota(jnp.int32, (qs.shape[0], LANE), 1)
    sm = sm_ref[...]
    first = t == 0
    mla_t, fox_t, diff_t, sel_t = [], [], [], []
    run = carry[...]
    for g in range(PG):
        lat_r, kpe_r, fox_r, sfx_r, tot_r, diff_r, sel_r = pages[7 * g: 7 * g + 7]
        lat = lat_r[...].astype(BF)
        kpe = kpe_r[...].astype(BF)
        mla_t.append((_qk(qml, lat) + _qk(qmp[:, :D_ROPE], kpe), lat))
        fkv = fox_r[...].astype(BF)
        fox_t.append((_qk(qf, fkv[:, :LANE]) + (cq + run + sfx_r[...]), fkv[:, LANE:]))
        run = run + tot_r[:, 0:1]
        dkv = diff_r[...].astype(BF)
        dbias = jnp.where(first, dbl_ref[...], dbc_ref[...]) if g == 0 else dbc_ref[...]
        diff_t.append((_qk(qd, dkv[:, :LANE]) + dbias, dkv[:, LANE:]))
        skv = sel_r[...].astype(BF)
        sbias = jnp.where(first, sbl_ref[...], sbc_ref[...]) if g == 0 else sbc_ref[...]
        madd = jnp.where(lane < SEL_BLK, sm[:, 2 * g: 2 * g + 1], sm[:, 2 * g + 1: 2 * g + 2])
        sel_t.append((_qk(qs, skv) + sbias + madd, skv))
    carry[...] = run
    _online_multi(mla_t, mm, lm, am)
    _online_multi(fox_t, mf, lf, af)
    _online_multi(diff_t, md, ld, ad)
    _online_multi(sel_t, ms, ls, as_)

    @pl.when(t == nsteps - 1)
    def _():
        def causal(rows):
            qi = lax.broadcasted_iota(jnp.int32, (rows, LANE), 0) & (dq - 1)
            return lax.broadcasted_iota(jnp.int32, (rows, LANE), 1) <= qi

        tl, tf, td, ts = tl_ref[...], tf_ref[...], td_ref[...], ts_ref[...]
        s = _qk(qml, tl[:, :D_C]) + _qk(qmp, tl[:, D_C:])
        _online_multi([(jnp.where(causal(qml.shape[0]), s, NEG_INF), tl[:, :D_C])], mm, lm, am)
        ctile = jnp.zeros((qf.shape[0], LANE), F32)
        for j in range(dq):
            ctile = jnp.where(lane == j, cs[j], ctile)
        s = _qk(qf, tf[:, :LANE]) + (cq - ctile)
        _online_multi([(jnp.where(causal(qf.shape[0]), s, NEG_INF), tf[:, LANE:])], mf, lf, af)
        s = _qk(qd, td[:, :LANE]) + dbt_ref[...]
        _online_multi([(jnp.where(causal(qd.shape[0]), s, NEG_INF), td[:, LANE:])], md, ld, ad)
        s = _qk(qs, ts) + sbt_ref[...]
        _online_multi([(jnp.where(causal(qs.shape[0]), s, NEG_INF), ts)], ms, ls, as_)
        om_ref[...] = am[...] / lm[...]
        of_ref[...] = af[...] / lf[...]
        os_ref[...] = as_[...] / ls[...]
        dl = dl_ref[...]
        lam = (jnp.exp(jnp.sum(dl[0:1] * dl[1:2], -1, keepdims=True))
               - jnp.exp(jnp.sum(dl[2:3] * dl[3:4], -1, keepdims=True)) + lam0_ref[...])
        od = ad[...] / ld[...]
        half = G_DIFF * dq
        for kvh in range(KV_DIFF):
            o0 = od[2 * half * kvh: 2 * half * kvh + half]
            o1 = od[2 * half * kvh + half: 2 * half * (kvh + 1)]
            od_ref[half * kvh: half * (kvh + 1), :] = o0 - lam * o1


def _sample_bias_tables(rel_bias, past_len, dq, n_win):
    tbl_a, tbl_d = rel_bias[:, :H_NSA], rel_bias[:, H_NSA:]
    q = jnp.arange(dq)[:, None]

    def rows_hq(tbl, dist):
        b = tbl[_t5_bucket(dist)]
        return jnp.transpose(b, (2, 0, 1)).reshape(tbl.shape[1] * dq, dist.shape[1])

    n_cmp = past_len // CMP_BLK
    ends = (jnp.arange(n_cmp) + 1) * CMP_BLK - 1
    cb = rows_hq(tbl_a, past_len + q - ends[None, :]).reshape(H_NSA * dq, n_cmp // 2, 2)
    cb = jnp.moveaxis(cb, -1, 0)
    key = jnp.arange(LANE)[None, :]
    wb1 = rows_hq(tbl_a, n_win + q - jnp.arange(n_win)[None, :])
    near = rows_hq(tbl_a, PAGE_SIZE + q - key)
    tail = rows_hq(tbl_a, q - key)
    far = rows_hq(tbl_a, 2 * PAGE_SIZE + q - key)[:, :1]

    def diff_rows(x):
        x = x.reshape(KV_DIFF, 1, G_DIFF, dq, x.shape[-1])
        return jnp.broadcast_to(x, (KV_DIFF, 2, G_DIFF, dq, x.shape[-1])).reshape(2 * H_DIFF * dq, x.shape[-1])

    dnear = diff_rows(rows_hq(tbl_d, PAGE_SIZE + q - key))
    dtail = diff_rows(rows_hq(tbl_d, q - key))
    dfar = diff_rows(rows_hq(tbl_d, 2 * PAGE_SIZE + q - key)[:, :1])
    return cb, wb1, near, tail, far, dnear, dtail, dfar


def _rows_hq(a, DB, dq, H):
    W = a.shape[1] // H
    return jnp.transpose(a.reshape(DB, dq, H, W), (0, 2, 1, 3)).reshape(DB, H * dq, W)


def _rows_token(a, DB, dq, H):
    W = a.shape[-1]
    return jnp.transpose(a.reshape(DB, H, dq, W), (0, 2, 1, 3)).reshape(DB * dq, H * W)


def _pad_tail(a, DB, dq):
    a = a.reshape(DB, dq, a.shape[-1])
    return jnp.pad(a, ((0, 0), (0, LANE - dq), (0, 0)))


def _sample_layer(x, l, lw, tabs, bias, page_table, caches, sfx, tot):
    (c_cmp, c_sel, st_win, c_lat, c_kpe, c_fox, c_diff) = caches
    cb, wb1, near, tail, far, dnear, dtail, dfar = bias
    DB, dq, _ = x.shape
    M = DB * dq
    NP = page_table.shape[1]
    n_pool = c_lat.shape[1]
    n_win = st_win.shape[2]
    assert dq & (dq - 1) == 0 and H_NSA * dq == 16 and NP % PG == 0
    x2 = x.reshape(M, D_MODEL)
    (qa, cmp_s, sel_s, win_s, selb, winb, misc, lat, kcat, qcat, fq, fkv, fkvb, dq_, dkv, dkvb, z) = _proj(
        x2, lw['wp'], lw['wuq'], lw['wukbd'], lw['gq'], lw['gkv'], lw['fb'], tabs, M)

    n_blk = n_pool * (PAGE_SIZE // CMP_BLK)
    kvc_pool = _cmp_summary(c_cmp.reshape(c_cmp.shape[0], n_blk, CMP_BLK * 2 * DH), l, lw['cpos'], lw['wc'],
                            lw['w2'], 256 if n_blk % 256 == 0 else n_blk)
    gg = min(GATHER_G, NP)
    kvc = _gather_rows(kvc_pool.reshape(n_pool, 1, (PAGE_SIZE // CMP_BLK) * LANE), page_table, gg)
    n_cmp = NP * (PAGE_SIZE // CMP_BLK)
    kvc = jnp.swapaxes(kvc.reshape(DB, n_cmp // 2, 2, LANE), 1, 2)

    q_nsa = _rows_hq(qa, DB, dq, H_NSA)
    per_b = lambda *blk: pl.BlockSpec((None,) + blk, lambda b: (b,) + (0,) * len(blk))
    cst = lambda a: pl.BlockSpec(a.shape, lambda b: (0,) * a.ndim)
    wt = _pad_tail(winb, DB, dq)
    ocw, selmask = pl.pallas_call(
        lambda *r: _nsa_sample_kernel(dq, n_win, *r),
        grid=(DB,),
        in_specs=[per_b(16, LANE), per_b(2, n_cmp // 2, LANE), cst(cb),
                  pl.BlockSpec((None, None, n_win, LANE), lambda b: (l, b, 0, 0)), per_b(LANE, LANE),
                  cst(wb1), cst(tail)],
        out_specs=[per_b(16, 2 * LANE), per_b(16, n_cmp // 2)],
        out_shape=[jax.ShapeDtypeStruct((DB, 16, 2 * LANE), F32), jax.ShapeDtypeStruct((DB, 16, n_cmp // 2), F32)],
        compiler_params=pltpu.CompilerParams(dimension_semantics=("arbitrary",), vmem_limit_bytes=VMEM_LIMIT),
        name="nsa_sample",
    )(q_nsa, kvc, cb, st_win, wt, wb1, tail)

    nsteps = NP // PG
    sm = selmask.reshape(DB, 16, NP, 2)[:, :, ::-1]
    sm = jnp.transpose(sm.reshape(DB, 16, nsteps, PG * 2), (0, 2, 1, 3))

    qm = _rows_hq(qcat, DB, dq, H_MLA)
    qml, qmp = qm[..., :D_C], qm[..., D_C:]
    qf = _rows_hq(fq, DB, dq, H_FOX)
    qd = _rows_hq(dq_, DB, dq, 2 * H_DIFF)
    logf_new = misc[:, MISC_F: MISC_F + H_FOX].reshape(DB, dq, H_FOX)
    tlf = jnp.broadcast_to(jnp.swapaxes(logf_new, 1, 2)[:, :, None, :], (DB, H_FOX, dq, dq)).reshape(DB, 16, dq)
    tlf = jnp.pad(tlf, ((0, 0), (0, 0), (0, LANE - dq)))

    def page(cols):
        return [pl.BlockSpec((None, None, PAGE_SIZE, cols),
                             lambda b, t, pt, g=g: (l, pt[b, NP - 1 - (t * PG + g)], 0, 0)) for g in range(PG)]

    def prow():
        return [pl.BlockSpec((None, None, 16, LANE),
                             lambda b, t, pt, g=g: (l, pt[b, NP - 1 - (t * PG + g)], 0, 0)) for g in range(PG)]

    lat_s, kpe_s, fox_s, sfx_s, tot_s, diff_s, sel_s_ = (page(D_C), page(D_ROPE), page(256), prow(), prow(),
                                                          page(256), page(LANE))
    page_specs, page_args = [], []
    for g in range(PG):
        page_specs += [lat_s[g], kpe_s[g], fox_s[g], sfx_s[g], tot_s[g], diff_s[g], sel_s_[g]]
        page_args += [c_lat, c_kpe, c_fox, sfx, tot, c_diff, c_sel]
    pb = lambda *blk: pl.BlockSpec((None,) + blk, lambda b, t, pt: (b,) + (0,) * len(blk))
    cs = lambda a: pl.BlockSpec(a.shape, lambda b, t, pt: (0,) * a.ndim)
    tails = (_pad_tail(kcat, DB, dq), _pad_tail(fkvb, DB, dq), _pad_tail(dkvb, DB, dq), _pad_tail(selb, DB, dq))
    consts = (dnear, dtail, dfar, near, tail, far, lw['dl'], lw['lam0'])
    st = lambda rows, dv: [pltpu.VMEM((rows, 1), F32), pltpu.VMEM((rows, 1), F32), pltpu.VMEM((rows, dv), F32)]
    om, of, od, os_ = pl.pallas_call(
        lambda *r: _stream_kernel(dq, nsteps, *r),
        grid_spec=pltpu.PrefetchScalarGridSpec(
            num_scalar_prefetch=1, grid=(DB, nsteps),
            in_specs=[pb(16, D_C), pb(16, LANE), pb(16, LANE), pb(32, LANE), pb(16, LANE),
                      pl.BlockSpec((None, None, 16, 2 * PG), lambda b, t, pt: (b, t, 0, 0))]
            + page_specs
            + [pb(LANE, D_C + LANE), pb(LANE, 256), pb(LANE, 256), pb(LANE, LANE), pb(16, LANE)]
            + [cs(a) for a in consts],
            out_specs=[pb(16, D_C), pb(16, LANE), pb(16, LANE), pb(16, LANE)],
            scratch_shapes=st(16, D_C) + st(16, LANE) + st(32, LANE) + st(16, LANE) + [pltpu.VMEM((16, 1), F32)]),
        out_shape=[jax.ShapeDtypeStruct((DB, 16, D_C), F32), jax.ShapeDtypeStruct((DB, 16, LANE), F32),
                   jax.ShapeDtypeStruct((DB, 16, LANE), F32), jax.ShapeDtypeStruct((DB, 16, LANE), F32)],
        compiler_params=pltpu.CompilerParams(dimension_semantics=("arbitrary", "arbitrary"),
                                             vmem_limit_bytes=VMEM_LIMIT),
        name="sample_stream",
    )(page_table, qml, qmp, qf, qd, q_nsa, sm, *page_args, *tails, tlf, *consts)

    olat = _rows_token(om, DB, dq, H_MLA)

    def pick_kv(o, H, G):
        o = o.reshape(DB, H, dq, 2, DH)
        o = jnp.stack([o[:, h, :, h // G] for h in range(H)], 1)
        return _rows_token(o.reshape(DB, H * dq, DH), DB, dq, H)

    ofox = pick_kv(of, H_FOX, G_FOX)
    odiff = pick_kv(od, H_DIFF, G_DIFF)
    upper = lambda o: _rows_token(o[..., DH:], DB, dq, H_NSA)
    onsa = jnp.concatenate([upper(ocw[..., :LANE]), upper(os_), upper(ocw[..., LANE:])], -1)
    x_new = _post(x2, olat, onsa, ofox, odiff, misc, z, lw['wuvbd'], lw['dg'], lw['ds'], lw['wbr'], lw['wm'],
                  lw['wo'], lw['lg'], lw['lb'], M).reshape(DB, dq, D_MODEL)
    win_all = jnp.concatenate([st_win[l], win_s.reshape(DB, dq, LANE)], 1)
    wlen = min(WINDOW, win_all.shape[1])
    states = (cmp_s.reshape(DB, dq, 2, DH), sel_s.reshape(DB, dq, 2, DH),
              win_all[:, win_all.shape[1] - wlen:].reshape(DB, wlen, 2, DH),
              lat.reshape(DB, dq, D_C), misc[:, MISC_KPE: MISC_KPE + D_ROPE].reshape(DB, dq, D_ROPE),
              fkv.reshape(DB, dq, 2, KV_FOX, DH), logf_new, dkv.reshape(DB, dq, 2, KV_DIFF, 2 * DD))
    return x_new, states


def kernel(x_prompt, x_sample, cache_nsa_cmp_kv, cache_nsa_sel_kv, state_nsa_win_kv, cache_mla_latent,
           cache_mla_kpe, cache_fox_kv, cache_fox_logf, cache_diff_kv, page_table, rel_bias, w_in,
           nsa_cmp_pos, nsa_cmp_w1, nsa_cmp_w2, mla_q_norm, mla_kv_norm, mla_w_uq, mla_w_uk, mla_w_uv,
           fox_forget_bias, diff_lambda, diff_subln, w_branch, w_out, ln_g, ln_b):
    past_len = page_table.shape[1] * PAGE_SIZE
    S = x_prompt.shape[1]
    DB, dq = x_sample.shape[:2]
    depth, n_pool = cache_mla_latent.shape[:2]
    tabs_s = _rope_tables(jnp.tile(past_len + jnp.arange(dq), DB))
    bias_s = _sample_bias_tables(rel_bias, past_len, dq, state_nsa_win_kv.shape[2])
    lf = jnp.swapaxes(cache_fox_logf, 2, 3)
    lf = jnp.broadcast_to(lf[:, :, :, None, :], (depth, n_pool, H_FOX, dq, PAGE_SIZE))
    n_rows = depth * n_pool * H_FOX * dq
    sfx, tot = _page_suffix(lf.reshape(n_rows, PAGE_SIZE), 2048 if n_rows % 2048 == 0 else n_rows)
    sfx = sfx.reshape(depth, n_pool, H_FOX * dq, PAGE_SIZE)
    tot = tot.reshape(depth, n_pool, H_FOX * dq, PAGE_SIZE)
    caches = (cache_nsa_cmp_kv,
              cache_nsa_sel_kv.reshape(depth, n_pool, PAGE_SIZE, 2 * DH),
              state_nsa_win_kv.reshape(depth, DB, state_nsa_win_kv.shape[2], 2 * DH),
              cache_mla_latent, cache_mla_kpe,
              cache_fox_kv.reshape(depth, n_pool, PAGE_SIZE, 2 * KV_FOX * DH),
              cache_diff_kv.reshape(depth, n_pool, PAGE_SIZE, 2 * KV_DIFF * 2 * DD))
    tabs_p = _rope_tables(jnp.arange(S))
    tb_a, tb_d, cb = _prompt_bias_tables(rel_bias, S)
    u_tri = (jnp.arange(LANE)[:, None] <= jnp.arange(LANE)[None, :]).astype(BF)
    xp, xs = x_prompt, x_sample
    st_p, st_s = [], []
    for l in range(depth):
        lwp = _layer_weights(l, w_in, nsa_cmp_pos, nsa_cmp_w1, nsa_cmp_w2, mla_q_norm, mla_kv_norm, mla_w_uq,
                             mla_w_uk, mla_w_uv, fox_forget_bias, diff_lambda, diff_subln, w_branch, w_out,
                             ln_g, ln_b)
        xp, sp = _prompt_layer(xp, lwp, tabs_p, tb_a, tb_d, cb, u_tri)
        xs, ss = _sample_layer(xs, l, lwp, tabs_s, bias_s, page_table, caches, sfx, tot)
        st_p.append(sp)
        st_s.append(ss)
    (p_cmp, p_sel, p_win, p_lat, p_kpe, p_fkv, p_flf, p_dkv) = [jnp.stack(z) for z in zip(*st_p)]
    (s_cmp, s_sel, s_win, s_lat, s_kpe, s_fkv, s_flf, s_dkv) = [jnp.stack(z) for z in zip(*st_s)]
    return (xp, xs, p_cmp, s_cmp, p_sel, s_sel, p_win, s_win, p_lat, s_lat, p_kpe, s_kpe,
            p_fkv, s_fkv, p_flf, s_flf, p_dkv, s_dkv)
```

```python
import functools
import math
import jax
import jax.numpy as jnp
from jax import lax
from jax.experimental import pallas as pl
from jax.experimental.pallas import tpu as pltpu


D_MODEL = 1024
DEPTH = 4
PAGE_SIZE = 128

N_BRANCH = 4
BR_W = D_MODEL // N_BRANCH
DH = 64
QBLK = 128
H_NSA = 4
CMP_BLK = 32
SEL_BLK = 64
CMP_PER_SEL = SEL_BLK // CMP_BLK
TOP_N = 16
N_LOCAL = 2
WINDOW = 512
H_MLA = 4
D_CQ = 256
D_C = 256
D_NOPE = 64
D_ROPE = 32
D_V = 64
ROPE_BASE = 10000.0
MLA_SCALE = (D_NOPE + D_ROPE) ** -0.5
H_FOX = 4
KV_FOX = 2
G_FOX = H_FOX // KV_FOX
H_DIFF = 4
KV_DIFF = 2
G_DIFF = H_DIFF // KV_DIFF
DD = 32
N_BUCKETS = 32
MAX_DIST = 128
DEEPNORM_ALPHA = (2 * DEPTH) ** 0.25
NEG_INF = -1e30
FORCE = 1e9

IN_SIZES = (
    H_NSA * DH, 2 * DH, 2 * DH, 2 * DH, 3 * H_NSA, BR_W,
    D_CQ, D_C, D_ROPE, BR_W,
    H_FOX * DH, KV_FOX * DH, KV_FOX * DH, H_FOX, BR_W,
    H_DIFF * 2 * DD, KV_DIFF * 2 * DD, KV_DIFF * 2 * DD, BR_W,
    N_BRANCH * D_MODEL,
)
N_IN = sum(IN_SIZES)


def _mm_kernel(x_ref, w_ref, o_ref):
    o_ref[...] = jnp.dot(x_ref[...].astype(jnp.bfloat16), w_ref[...].astype(jnp.bfloat16),
                         preferred_element_type=jnp.float32)


def _pmatmul(x, w):
    lead = x.shape[:-1]
    K = x.shape[-1]
    N = w.shape[-1]
    x2 = x.reshape(-1, K)
    M = x2.shape[0]
    tn = 256
    n_pad = -(-N // tn) * tn
    if n_pad != N:
        w = jnp.pad(w, ((0, 0), (0, n_pad - N)))
    tm = min(M, 1024)
    out = pl.pallas_call(
        _mm_kernel,
        grid=(M // tm, n_pad // tn),
        in_specs=[pl.BlockSpec((tm, K), lambda i, j: (i, 0)),
                  pl.BlockSpec((K, tn), lambda i, j: (0, j))],
        out_specs=pl.BlockSpec((tm, tn), lambda i, j: (i, j)),
        out_shape=jax.ShapeDtypeStruct((M, n_pad), jnp.float32),
        name="matmul",
    )(x2, w)
    return out[:, :N].reshape(*lead, N)


def _split(h, sizes):
    out, start = [], 0
    for n in sizes:
        out.append(h[..., start:start + n])
        start += n
    return out


def _rmsnorm(x, g, eps=1e-6):
    xf = x.astype(jnp.float32)
    y = xf * lax.rsqrt(jnp.mean(xf * xf, -1, keepdims=True) + eps)
    return (y * g).astype(x.dtype)


def _layernorm(x, g, b, eps=1e-5):
    xf = x.astype(jnp.float32)
    mu = jnp.mean(xf, -1, keepdims=True)
    var = jnp.mean(jnp.square(xf - mu), -1, keepdims=True)
    return ((xf - mu) * lax.rsqrt(var + eps) * g + b).astype(x.dtype)


def _masked_softmax(s, mask):
    s = jnp.where(mask, s.astype(jnp.float32), NEG_INF)
    e = jnp.where(mask, jnp.exp(s - jnp.max(s, -1, keepdims=True)), 0.0)
    return e / jnp.maximum(jnp.sum(e, -1, keepdims=True), 1e-30)


def _t5_bucket(dist):
    n = jnp.maximum(dist, 0)
    exact = N_BUCKETS // 2
    nf = jnp.maximum(n, 1).astype(jnp.float32)
    large = exact + (jnp.log(nf / exact) / math.log(MAX_DIST / exact) * (N_BUCKETS - exact)).astype(jnp.int32)
    return jnp.where(n < exact, n, jnp.minimum(large, N_BUCKETS - 1))


def _rope(x, pos):
    half = D_ROPE // 2
    inv = ROPE_BASE ** (-jnp.arange(half, dtype=jnp.float32) / half)
    ang = pos.astype(jnp.float32)[:, None] * inv
    ang = ang.reshape(ang.shape[0], *([1] * (x.ndim - 3)), half)
    cos, sin = jnp.cos(ang), jnp.sin(ang)
    x1 = x[..., :half].astype(jnp.float32)
    x2 = x[..., half:].astype(jnp.float32)
    return jnp.concatenate([x1 * cos - x2 * sin, x2 * cos + x1 * sin], -1).astype(x.dtype)


def _sweep(fn, q_pos, *q_arrays):
    sq = q_pos.shape[0]
    if sq > QBLK and sq % QBLK == 0:
        nb = sq // QBLK
        blocks = tuple(jnp.moveaxis(a.reshape(a.shape[0], nb, QBLK, *a.shape[2:]), 1, 0) for a in q_arrays)
        out = lax.map(lambda args: fn(args[0], *args[1]), (q_pos.reshape(nb, QBLK), blocks))
        out = jnp.moveaxis(out, 0, 1)
        return out.reshape(out.shape[0], sq, *out.shape[3:])
    return fn(q_pos, *q_arrays)


def _block_rows(rows):
    B, T = rows.shape[:2]
    n_blk = -(-T // SEL_BLK)
    pad = ((0, 0), (0, n_blk * SEL_BLK - T)) + ((0, 0),) * (rows.ndim - 2)
    return jnp.pad(rows, pad).reshape(B, n_blk, SEL_BLK, *rows.shape[2:])


def _take_blocks(blocks, idx):
    return jax.vmap(lambda blk, i: blk[i])(blocks, idx)


def _local_block_gather(rows):
    blocks = _block_rows(rows)
    return lambda idx: _take_blocks(blocks, idx)


def _paged_block_gather(pool, page_table, new_rows):
    B = new_rows.shape[0]
    bpp = PAGE_SIZE // SEL_BLK
    n_past = page_table.shape[1] * bpp
    pool_blocks = pool.reshape(-1, SEL_BLK, *pool.shape[2:])
    tail = _block_rows(new_rows)
    n_tail = tail.shape[1]

    def gather(idx):
        ip = jnp.clip(idx, 0, n_past - 1)
        page = jnp.take_along_axis(page_table, (ip // bpp).reshape(B, -1), axis=1).reshape(ip.shape)
        from_pool = pool_blocks[page * bpp + ip % bpp]
        from_tail = _take_blocks(tail, jnp.clip(idx - n_past, 0, n_tail - 1))
        return jnp.where((idx < n_past)[..., None, None, None], from_pool, from_tail)
    return gather


def _nsa_compressed(q, qpos, rows, cmp_pos, cmp_w1, cmp_w2, tbl):
    B, T = rows.shape[:2]
    Q = q.shape[1]
    n_cmp = T // CMP_BLK
    blocks = rows[:, :n_cmp * CMP_BLK].reshape(B, n_cmp, CMP_BLK, 2, DH)
    hid = jnp.einsum('bnlcd,cldh->bnch', blocks + jnp.swapaxes(cmp_pos, 0, 1),
                     cmp_w1.reshape(2, CMP_BLK, DH, DH))
    kv_c = jnp.einsum('bnch,che->bnce', jax.nn.silu(hid), cmp_w2)
    ends = (jnp.arange(n_cmp) + 1) * CMP_BLK - 1
    dist = qpos[:, None] - ends[None, :]
    s = jnp.einsum('bqhd,bnd->bhqn', q, kv_c[:, :, 0]).astype(jnp.float32)
    s = s + jnp.moveaxis(tbl[_t5_bucket(dist)], -1, 0)
    p = _masked_softmax(s, dist >= 0)
    o = jnp.einsum('bhqn,bnd->bqhd', p.astype(q.dtype), kv_c[:, :, 1])
    n_sel = -(-T // SEL_BLK)
    imp = jnp.pad(jnp.sum(p, axis=1), ((0, 0), (0, 0), (0, n_sel * CMP_PER_SEL - n_cmp)))
    imp = imp.reshape(B, Q, n_sel, CMP_PER_SEL).sum(-1)
    blk = jnp.arange(n_sel)[None, :]
    cur = (qpos // SEL_BLK)[:, None]
    valid = blk <= cur
    forced = valid & ((blk == 0) | (cur - blk < N_LOCAL))
    score = jnp.where(valid, jnp.where(forced, FORCE, imp), -FORCE)
    _, idx = lax.top_k(score, min(TOP_N, n_sel))
    return o, idx


def _nsa_selected(qpos, q, idx, gather, tbl):
    kv = gather(idx)
    kpos = idx[..., None] * SEL_BLK + jnp.arange(SEL_BLK)
    dist = qpos[None, :, None, None] - kpos
    s = jnp.einsum('bqhd,bqnld->bqhnl', q, kv[..., 0, :]).astype(jnp.float32)
    s = s + jnp.moveaxis(tbl[_t5_bucket(dist)], -1, 2)
    B, Q, H, n, L = s.shape
    p = _masked_softmax(s.reshape(B, Q, H, n * L), (dist >= 0).reshape(B, Q, 1, n * L))
    return jnp.einsum('bqhnl,bqnld->bqhd', p.reshape(B, Q, H, n, L).astype(q.dtype), kv[..., 1, :])


def _window_attend(q, kv, qpos, kpos, tbl):
    dist = qpos[:, None] - kpos[None, :]
    s = jnp.einsum('bqhd,bkd->bhqk', q, kv[:, :, 0]).astype(jnp.float32)
    s = s + jnp.moveaxis(tbl[_t5_bucket(dist)], -1, 0)
    mask = (dist >= 0) & (dist <= WINDOW) & (kpos >= 0)[None, :]
    p = _masked_softmax(s, mask)
    return jnp.einsum('bhqk,bkd->bqhd', p.astype(q.dtype), kv[:, :, 1])


def _window_banded(q, kv, pos, tbl):
    B, S = q.shape[:2]
    qb = QBLK if S % QBLK == 0 else S
    nb = S // qb
    padded = jnp.pad(kv, ((0, 0), (WINDOW, 0), (0, 0), (0, 0)))
    kidx = jnp.arange(nb)[:, None] * qb + jnp.arange(WINDOW + qb)[None, :]
    kv_b = padded[:, kidx]
    kpos_b = pos[0] + kidx - WINDOW
    q_b = q.reshape(B, nb, qb, H_NSA, DH)
    o = jax.vmap(_window_attend, in_axes=(1, 1, 0, 0, None), out_axes=1)(
        q_b, kv_b, pos.reshape(nb, qb), kpos_b, tbl)
    return o.reshape(B, S, H_NSA, DH)


def _mla_attend(qpos, q_lat, q_pe, lat, kpe, kpos):
    s = (jnp.einsum('bqhc,bkc->bhqk', q_lat, lat) + jnp.einsum('bqhr,bkr->bhqk', q_pe, kpe)).astype(jnp.float32)
    p = _masked_softmax(s, kpos[None, :] <= qpos[:, None])
    return jnp.einsum('bhqk,bkc->bqhc', p.astype(lat.dtype), lat)


def _fox_attend(qpos, q, r_q, kv, r_k, kpos):
    B, Q = q.shape[:2]
    T = kv.shape[1]
    qg = q.reshape(B, Q, KV_FOX, G_FOX, DH)
    s = jnp.einsum('bqkgd,btkd->bkgqt', qg, kv[:, :, 0]).astype(jnp.float32)
    rk = r_k.reshape(B, T, KV_FOX, G_FOX).transpose(0, 2, 3, 1)[:, :, :, None, :]
    rq = r_q.reshape(B, Q, KV_FOX, G_FOX).transpose(0, 2, 3, 1)[..., None]
    p = _masked_softmax(s + rk - rq, kpos[None, :] <= qpos[:, None])
    o = jnp.einsum('bkgqt,btkd->bqkgd', p.astype(kv.dtype), kv[:, :, 1])
    return o.reshape(B, Q, H_FOX, DH)


def _diff_attend(qpos, q, kv, kpos, tbl, lam):
    B, Q = q.shape[:2]
    T = kv.shape[1]
    k = kv[:, :, 0].reshape(B, T, KV_DIFF, 2, DD)
    v = kv[:, :, 1]
    qg = q.reshape(B, Q, KV_DIFF, G_DIFF, 2, DD)
    s = jnp.einsum('bqkgid,btkid->ibkgqt', qg, k).astype(jnp.float32)
    dist = qpos[:, None] - kpos[None, :]
    bias = jnp.moveaxis(tbl[_t5_bucket(dist)], -1, 0).reshape(KV_DIFF, G_DIFF, Q, T)
    p = _masked_softmax(s + bias, dist >= 0)
    a = (p[0] - lam * p[1]).astype(v.dtype)
    o = jnp.einsum('bkgqt,btkv->bqkgv', a, v)
    return o.reshape(B, Q, H_DIFF, 2 * DD)


def _gather_pages(pool, page_table):
    g = pool[page_table]
    return g.reshape(g.shape[0], -1, *g.shape[3:])


def _layer(x, pos, lidx, past, rel_bias, w_in, cmp_pos, cmp_w1, cmp_w2, mla_gq, mla_gkv,
           mla_wuq, mla_wuk, mla_wuv, fox_bf, diff_lam, diff_g, w_branch, w_out, ln_g, ln_b):
    B, S, _ = x.shape
    dt = x.dtype
    (a_q, a_cmp, a_sel, a_win, a_gate, a_z,
     b_cq, b_ckv, b_kpe, b_z,
     c_q, c_k, c_v, c_f, c_z,
     d_q, d_k, d_v, d_z, merge_logit) = _split(_pmatmul(x, w_in), IN_SIZES)
    tbl_a, tbl_d = rel_bias[:, :H_NSA], rel_bias[:, H_NSA:]

    def with_past(name, new):
        return new if past is None else jnp.concatenate([past[name], new], axis=1)

    q_a = a_q.reshape(B, S, H_NSA, DH) * DH ** -0.5
    cmp_new = a_cmp.reshape(B, S, 2, DH)
    sel_new = a_sel.reshape(B, S, 2, DH)
    win_new = a_win.reshape(B, S, 2, DH)
    o_cmp, sel_idx = _nsa_compressed(q_a, pos, with_past('nsa_cmp', cmp_new), cmp_pos, cmp_w1, cmp_w2, tbl_a)
    if past is None:
        gather = _local_block_gather(sel_new)
        win_all = win_new
        o_win = _window_banded(q_a, win_new, pos, tbl_a)
    else:
        gather = _paged_block_gather(past['nsa_sel_pool'], past['page_table'], sel_new)
        win_all = jnp.concatenate([past['nsa_win'], win_new], axis=1)
        kpos_w = pos[0] - past['nsa_win'].shape[1] + jnp.arange(win_all.shape[1])
        o_win = _window_attend(q_a, win_all, pos, kpos_w, tbl_a)
    win_state = win_all[:, win_all.shape[1] - min(WINDOW, win_all.shape[1]):]
    o_sel = _sweep(lambda qp, qb, ib: _nsa_selected(qp, qb, ib, gather, tbl_a), pos, q_a, sel_idx)
    g_a = jax.nn.sigmoid(a_gate.reshape(B, S, H_NSA, 3))
    o_a = (g_a[..., 0:1] * o_cmp + g_a[..., 1:2] * o_sel + g_a[..., 2:3] * o_win).reshape(B, S, BR_W)

    cq = _rmsnorm(b_cq, mla_gq)
    qf = (cq @ mla_wuq).reshape(B, S, H_MLA, D_NOPE + D_ROPE)
    q_lat = jnp.einsum('bshn,chn->bshc', qf[..., :D_NOPE], mla_wuk) * MLA_SCALE
    q_pe = _rope(qf[..., D_NOPE:], pos) * MLA_SCALE
    lat_new = _rmsnorm(b_ckv, mla_gkv)
    kpe_new = _rope(b_kpe, pos)
    lat, kpe = with_past('mla_lat', lat_new), with_past('mla_kpe', kpe_new)
    kpos_all = jnp.arange(lat.shape[1])
    o_lat = _sweep(lambda qp, ql, qr: _mla_attend(qp, ql, qr, lat, kpe, kpos_all), pos, q_lat, q_pe)
    o_b = jnp.einsum('bshc,chv->bshv', o_lat, mla_wuv).reshape(B, S, BR_W)

    q_c = c_q.reshape(B, S, H_FOX, DH) * DH ** -0.5
    fkv_new = jnp.stack([c_k.reshape(B, S, KV_FOX, DH), c_v.reshape(B, S, KV_FOX, DH)], axis=2)
    logf_new = jax.nn.log_sigmoid((c_f + fox_bf).astype(jnp.float32))
    fkv = with_past('fox_kv', fkv_new)
    logf = with_past('fox_logf', logf_new).astype(jnp.float32)
    r = lax.cumsum(logf, axis=1, reverse=True) - logf
    o_c = _sweep(lambda qp, qb, rq: _fox_attend(qp, qb, rq, fkv, r, kpos_all),
                 pos, q_c, r[:, r.shape[1] - S:]).reshape(B, S, BR_W)

    q_d = d_q.reshape(B, S, H_DIFF, 2, DD) * DD ** -0.5
    dkv_new = jnp.stack([d_k.reshape(B, S, KV_DIFF, 2 * DD), d_v.reshape(B, S, KV_DIFF, 2 * DD)], axis=2)
    dkv = with_past('diff_kv', dkv_new)
    lam_init = 0.8 - 0.6 * math.exp(-0.3 * lidx)
    lam = (jnp.exp(jnp.sum(diff_lam[0] * diff_lam[1]).astype(jnp.float32))
           - jnp.exp(jnp.sum(diff_lam[2] * diff_lam[3]).astype(jnp.float32)) + lam_init)
    o_d = _sweep(lambda qp, qb: _diff_attend(qp, qb, dkv, kpos_all, tbl_d, lam), pos, q_d)
    o_d = (_rmsnorm(o_d, diff_g, 1e-5) * (1.0 - lam_init)).reshape(B, S, BR_W)

    outs = jnp.stack([o_a * jax.nn.silu(a_z), o_b * jax.nn.silu(b_z),
                      o_c * jax.nn.silu(c_z), o_d * jax.nn.silu(d_z)], axis=2)
    branch = jnp.einsum('bsnc,ncd->bsnd', outs, w_branch)
    gates = jax.nn.sigmoid(merge_logit.reshape(B, S, N_BRANCH, D_MODEL))
    y = _pmatmul(jnp.sum(gates * branch, axis=2), w_out)
    x_new = _layernorm(DEEPNORM_ALPHA * x + y, ln_g, ln_b)
    states = (cmp_new, sel_new, win_state, lat_new, kpe_new, fkv_new, logf_new.astype(dt), dkv_new)
    return x_new, states


TQ = 128
TK = 128
LANE = 128
BF = jnp.bfloat16
F32 = jnp.float32
VMEM_LIMIT = 56 * 1024 * 1024

PG_AQ, PG_CMP, PG_SEL, PG_WIN, PG_MISC = 0, 512, 640, 768, 896
PG_CQ, PG_CKV, PG_FQ, PG_FKV, PG_DQ, PG_DKV, PG_Z, PG_END = 1024, 1280, 1536, 2048, 2304, 3328, 3584, 4608
MISC_KPE, MISC_GATE, MISC_F = 0, 32, 48
N_SMALL = N_IN - N_BRANCH * D_MODEL


def _in_offsets():
    offs, s = [], 0
    for n in IN_SIZES:
        offs.append(s)
        s += n
    return offs


def _small_proj_columns():
    import numpy as np
    (o_aq, o_cmp, o_sel, o_win, o_gate, o_az, o_cq, o_ckv, o_kpe, o_bz,
     o_fq, o_fk, o_fv, o_ff, o_cz, o_dq, o_dk, o_dv, o_dz, _) = _in_offsets()
    src = np.full((PG_END,), -1, np.int32)
    for h in range(H_NSA):
        src[PG_AQ + LANE * h: PG_AQ + LANE * h + DH] = o_aq + DH * h + np.arange(DH)
    src[PG_CMP:PG_CMP + 2 * DH] = o_cmp + np.arange(2 * DH)
    src[PG_SEL:PG_SEL + 2 * DH] = o_sel + np.arange(2 * DH)
    src[PG_WIN:PG_WIN + 2 * DH] = o_win + np.arange(2 * DH)
    src[PG_MISC + MISC_KPE: PG_MISC + MISC_KPE + D_ROPE] = o_kpe + np.arange(D_ROPE)
    src[PG_MISC + MISC_GATE: PG_MISC + MISC_GATE + 3 * H_NSA] = o_gate + np.arange(3 * H_NSA)
    src[PG_MISC + MISC_F: PG_MISC + MISC_F + H_FOX] = o_ff + np.arange(H_FOX)
    src[PG_CQ:PG_CQ + D_CQ] = o_cq + np.arange(D_CQ)
    src[PG_CKV:PG_CKV + D_C] = o_ckv + np.arange(D_C)
    for h in range(H_FOX):
        kvh = h // G_FOX
        base = PG_FQ + LANE * h + DH * kvh
        src[base: base + DH] = o_fq + DH * h + np.arange(DH)
    src[PG_FKV:PG_FKV + 2 * KV_FOX * DH] = o_fk + np.arange(2 * KV_FOX * DH)
    for kvh in range(KV_DIFF):
        for mp in range(2):
            for g in range(G_DIFF):
                h = kvh * G_DIFF + g
                r = (kvh * 2 + mp) * G_DIFF + g
                base = PG_DQ + LANE * r + 2 * DD * kvh + DD * mp
                src[base: base + DD] = o_dq + 2 * DD * h + DD * mp + np.arange(DD)
    src[PG_DKV:PG_DKV + 2 * KV_DIFF * 2 * DD] = o_dk + np.arange(2 * KV_DIFF * 2 * DD)
    src[PG_Z + 0 * BR_W: PG_Z + 1 * BR_W] = o_az + np.arange(BR_W)
    src[PG_Z + 1 * BR_W: PG_Z + 2 * BR_W] = o_bz + np.arange(BR_W)
    src[PG_Z + 2 * BR_W: PG_Z + 3 * BR_W] = o_cz + np.arange(BR_W)
    src[PG_Z + 3 * BR_W: PG_Z + 4 * BR_W] = o_dz + np.arange(BR_W)
    return src


def _permute_cols(w, src):
    import numpy as np
    g = jnp.take(w, jnp.asarray(np.maximum(src, 0)), axis=-1)
    return jnp.where(jnp.asarray(src >= 0), g, 0.0)


def _uq_columns():
    import numpy as np
    src = np.full((H_MLA * D_NOPE + H_MLA * LANE,), -1, np.int32)
    per = D_NOPE + D_ROPE
    for h in range(H_MLA):
        src[D_NOPE * h: D_NOPE * (h + 1)] = per * h + np.arange(D_NOPE)
        src[H_MLA * D_NOPE + LANE * h: H_MLA * D_NOPE + LANE * h + D_ROPE] = per * h + D_NOPE + np.arange(D_ROPE)
    return src


def _rope_tables(pos):
    half = D_ROPE // 2
    inv = ROPE_BASE ** (-jnp.arange(half, dtype=F32) / half)
    ang = pos.astype(F32)[:, None] * inv
    cos, sin = jnp.cos(ang), jnp.sin(ang)
    n = pos.shape[0]
    one = jnp.ones((n, LANE - D_ROPE), F32)
    zer = jnp.zeros((n, LANE - D_ROPE), F32)
    zh = jnp.zeros((n, half), F32)
    c = jnp.concatenate([cos, cos, one], -1)
    s1 = jnp.concatenate([zh, sin, zer], -1)
    s2 = jnp.concatenate([-sin, zh, zer], -1)
    return c, s1, s2


def _proj_kernel(x_ref, wp_ref, wuq_ref, wuk_ref, gq_ref, gkv_ref, fb_ref, c_ref, s1_ref, s2_ref,
                 qa_ref, cmp_ref, sel_ref, win_ref, selb_ref, winb_ref, misc_ref, lat_ref, kcat_ref,
                 qcat_ref, fq_ref, fkv_ref, fkvb_ref, dq_ref, dkv_ref, dkvb_ref, z_ref):
    xb = x_ref[...].astype(BF)

    def mm(a, b):
        return jnp.dot(xb, wp_ref[:, a:b], preferred_element_type=F32)

    cos, s1, s2 = c_ref[...], s1_ref[...], s2_ref[...]

    def rope(v):
        return v * cos + pltpu.roll(v, D_ROPE // 2, 1) * s1 + pltpu.roll(v, LANE - D_ROPE // 2, 1) * s2

    qa_ref[...] = (mm(PG_AQ, PG_CMP) * DH ** -0.5).astype(BF)
    cmp_ref[...] = mm(PG_CMP, PG_SEL)
    t = mm(PG_SEL, PG_WIN)
    sel_ref[...] = t
    selb_ref[...] = t.astype(BF)
    t = mm(PG_WIN, PG_MISC)
    win_ref[...] = t
    winb_ref[...] = t.astype(BF)

    misc = mm(PG_MISC, PG_CQ)
    lane = lax.broadcasted_iota(jnp.int32, misc.shape, 1)
    roped = rope(misc)
    xf = misc + fb_ref[...]
    logf = jnp.minimum(xf, 0.0) - jnp.log1p(jnp.exp(-jnp.abs(xf)))
    misc_ref[...] = jnp.where((lane >= MISC_F) & (lane < MISC_F + H_FOX), logf, roped)

    cq = mm(PG_CQ, PG_CKV)
    cq = cq * lax.rsqrt(jnp.mean(cq * cq, -1, keepdims=True) + 1e-6) * gq_ref[...]
    qf = jnp.dot(cq.astype(BF), wuq_ref[...], preferred_element_type=F32)
    nn = H_MLA * D_NOPE
    ql = jnp.dot(qf[:, :nn].astype(BF), wuk_ref[...], preferred_element_type=F32) * MLA_SCALE
    wq = D_C + LANE
    for h in range(H_MLA):
        qcat_ref[:, wq * h: wq * h + D_C] = ql[:, D_C * h: D_C * (h + 1)].astype(BF)
        pe = rope(qf[:, nn + LANE * h: nn + LANE * (h + 1)]) * MLA_SCALE
        qcat_ref[:, wq * h + D_C: wq * (h + 1)] = pe.astype(BF)

    ckv = mm(PG_CKV, PG_FQ)
    lat = ckv * lax.rsqrt(jnp.mean(ckv * ckv, -1, keepdims=True) + 1e-6) * gkv_ref[...]
    lat_ref[...] = lat
    kcat_ref[:, :D_C] = lat.astype(BF)
    kcat_ref[:, D_C:] = jnp.where(lane < D_ROPE, roped, 0.0).astype(BF)

    fq_ref[...] = (mm(PG_FQ, PG_FKV) * DH ** -0.5).astype(BF)
    t = mm(PG_FKV, PG_DQ)
    fkv_ref[...] = t
    fkvb_ref[...] = t.astype(BF)
    dq_ref[...] = (mm(PG_DQ, PG_DKV) * DD ** -0.5).astype(BF)
    t = mm(PG_DKV, PG_Z)
    dkv_ref[...] = t
    dkvb_ref[...] = t.astype(BF)
    z_ref[...] = mm(PG_Z, PG_END)


def _proj(x2, wp, wuq, wukbd, gq, gkv, fb, tabs, tm):
    M = x2.shape[0]
    npos = tabs[0].shape[0] // tm
    row = lambda i: (i, 0)
    cst = lambda i: (0, 0)
    tab = lambda i: (i % npos, 0)
    widths = [(512, BF), (128, F32), (128, F32), (128, F32), (128, BF), (128, BF), (128, F32), (D_C, F32),
              (D_C + LANE, BF), (H_MLA * (D_C + LANE), BF), (512, BF), (256, F32), (256, BF), (1024, BF),
              (256, F32), (256, BF), (1024, F32)]
    return pl.pallas_call(
        _proj_kernel,
        grid=(M // tm,),
        in_specs=[pl.BlockSpec((tm, D_MODEL), row),
                  pl.BlockSpec(wp.shape, cst), pl.BlockSpec(wuq.shape, cst), pl.BlockSpec(wukbd.shape, cst),
                  pl.BlockSpec((1, D_CQ), cst), pl.BlockSpec((1, D_C), cst), pl.BlockSpec((1, LANE), cst),
                  pl.BlockSpec((tm, LANE), tab), pl.BlockSpec((tm, LANE), tab), pl.BlockSpec((tm, LANE), tab)],
        out_specs=[pl.BlockSpec((tm, w), row) for w, _ in widths],
        out_shape=[jax.ShapeDtypeStruct((M, w), d) for w, d in widths],
        compiler_params=pltpu.CompilerParams(dimension_semantics=("arbitrary",), vmem_limit_bytes=VMEM_LIMIT),
        name="proj",
    )(x2, wp, wuq, wukbd, gq, gkv, fb, *tabs)


def _cmp_summary_kernel(x_ref, pos_ref, wc_ref, w2_ref, o_ref):
    xb = (x_ref[...] + pos_ref[...]).astype(BF)
    hid = jnp.dot(xb, wc_ref[...], preferred_element_type=F32)
    act = hid * jax.nn.sigmoid(hid)
    o_ref[...] = jnp.dot(act.astype(BF), w2_ref[...], preferred_element_type=F32)


def _cmp_summary(x3, layer, pos_row, wc, w2, tb):
    _, N, W = x3.shape
    return pl.pallas_call(
        _cmp_summary_kernel,
        grid=(N // tb,),
        in_specs=[pl.BlockSpec((None, tb, W), lambda i: (layer, i, 0)),
                  pl.BlockSpec((1, W), lambda i: (0, 0)),
                  pl.BlockSpec(wc.shape, lambda i: (0, 0)),
                  pl.BlockSpec(w2.shape, lambda i: (0, 0))],
        out_specs=pl.BlockSpec((tb, LANE), lambda i: (i, 0)),
        out_shape=jax.ShapeDtypeStruct((N, LANE), F32),
        compiler_params=pltpu.CompilerParams(dimension_semantics=("arbitrary",), vmem_limit_bytes=VMEM_LIMIT),
        name="cmp_summary",
    )(x3, pos_row, wc, w2)


CMP_PAGES = 64


def _cmp_pool_kernel(xt_ref, pos_ref, wc_ref, w2_ref, o_ref, xs):
    n_pages = xt_ref.shape[0]
    per = PAGE_SIZE // CMP_BLK
    for p in range(n_pages):
        xs[PAGE_SIZE * p: PAGE_SIZE * (p + 1), :] = xt_ref[p].T
    w = 2 * DH
    hid = jnp.zeros((n_pages * per, w), F32)
    for i in range(CMP_BLK):
        rows = xs[pl.ds(i, n_pages * per, stride=CMP_BLK), :] + pos_ref[:, w * i: w * (i + 1)]
        hid = hid + jnp.dot(rows.astype(BF), wc_ref[w * i: w * (i + 1), :], preferred_element_type=F32)
    act = hid * jax.nn.sigmoid(hid)
    o_ref[...] = jnp.dot(act.astype(BF), w2_ref[...], preferred_element_type=F32)


def _cmp_pool_summary(xt, layer, pos_row, wc, w2, pp):
    n_pool = xt.shape[1]
    per = PAGE_SIZE // CMP_BLK
    return pl.pallas_call(
        _cmp_pool_kernel,
        grid=(n_pool // pp,),
        in_specs=[pl.BlockSpec((None, pp, 2 * DH, PAGE_SIZE), lambda i: (layer, i, 0, 0)),
                  pl.BlockSpec(pos_row.shape, lambda i: (0, 0)),
                  pl.BlockSpec(wc.shape, lambda i: (0, 0)),
                  pl.BlockSpec(w2.shape, lambda i: (0, 0))],
        out_specs=pl.BlockSpec((pp * per, LANE), lambda i: (i, 0)),
        out_shape=jax.ShapeDtypeStruct((n_pool * per, LANE), F32),
        scratch_shapes=[pltpu.VMEM((pp * PAGE_SIZE, 2 * DH), F32)],
        compiler_params=pltpu.CompilerParams(dimension_semantics=("arbitrary",), vmem_limit_bytes=VMEM_LIMIT),
        name="cmp_pool_summary",
    )(xt, pos_row, wc, w2)


def _split3(x):
    hi = x.astype(BF)
    r = x - hi.astype(F32)
    mid = r.astype(BF)
    lo = (r - mid.astype(F32)).astype(BF)
    return hi, mid, lo


def _dot3(x, u):
    hi, mid, lo = _split3(x)
    d = lambda a: jnp.dot(a, u, preferred_element_type=F32)
    return d(hi) + d(mid) + d(lo)


def _prefix_kernel(x_ref, u_ref, o_ref):
    u = u_ref[...]
    carry = jnp.zeros((x_ref.shape[0], 1), F32)
    for c in range(x_ref.shape[1] // LANE):
        p = _dot3(x_ref[:, LANE * c: LANE * (c + 1)], u)
        o_ref[:, LANE * c: LANE * (c + 1)] = p + carry
        carry = carry + p[:, LANE - 1: LANE]


def _prefix(x, u):
    return pl.pallas_call(_prefix_kernel, out_shape=jax.ShapeDtypeStruct(x.shape, F32), name="fox_prefix")(x, u)


def _online(s, v, m_ref, l_ref, acc_ref):
    m_prev = m_ref[...]
    m_new = jnp.maximum(m_prev, jnp.max(s, -1, keepdims=True))
    alpha = jnp.exp(m_prev - m_new)
    p = jnp.exp(s - m_new)
    l_ref[...] = alpha * l_ref[...] + jnp.sum(p, -1, keepdims=True)
    acc_ref[...] = alpha * acc_ref[...] + jnp.dot(p.astype(BF), v, preferred_element_type=F32)
    m_ref[...] = m_new


def _init_state(m_ref, l_ref, acc_ref):
    m_ref[...] = jnp.full(m_ref.shape, NEG_INF, F32)
    l_ref[...] = jnp.zeros(l_ref.shape, F32)
    acc_ref[...] = jnp.zeros(acc_ref.shape, F32)


def _qk(q, k):
    return lax.dot_general(q, k, (((1,), (1,)), ((), ())), preferred_element_type=F32)


def _key_tiles(S):
    nkb = S // TK
    return 4 if nkb % 4 == 0 else (2 if nkb % 2 == 0 else 1)


def _causal_sweep(qb, W, rows, tile, m_ref, l_ref, acc_ref):
    nfull = lax.shift_right_logical(qb, int(math.log2(W)))

    def body(j, c):
        s, v = tile(j)
        _online(s, v, m_ref, l_ref, acc_ref)
        return c

    lax.fori_loop(0, nfull, body, 0)
    r = lax.broadcasted_iota(jnp.int32, (rows, W * TK), 0) & (TQ - 1)
    c = lax.broadcasted_iota(jnp.int32, (rows, W * TK), 1)
    s, v = tile(nfull)
    _online(jnp.where(c <= r + TQ * (qb - W * nfull), s, NEG_INF), v, m_ref, l_ref, acc_ref)


def _bias_tiles(tb_ref, heads, qb, W, j):
    cols = []
    for u in range(W):
        d = jnp.clip(qb - (W * j + u), 0, 2)
        cols.append(jnp.concatenate([tb_ref[h, d] for h in heads], 0))
    return cols[0] if W == 1 else jnp.concatenate(cols, 1)


def _mla_kernel(W, q_ref, k_ref, o_ref, qs, m_ref, l_ref, acc_ref):
    qb = pl.program_id(1)
    wq = D_C + LANE
    kw = W * TK
    for h in range(H_MLA):
        qs[TQ * h: TQ * (h + 1), :] = q_ref[:, wq * h: wq * (h + 1)]
    _init_state(m_ref, l_ref, acc_ref)

    def tile(j):
        k = k_ref[pl.ds(pl.multiple_of(j * kw, kw), kw), :]
        return _qk(qs[...], k), k[:, :D_C]

    _causal_sweep(qb, W, H_MLA * TQ, tile, m_ref, l_ref, acc_ref)
    for h in range(H_MLA):
        blk = slice(TQ * h, TQ * (h + 1))
        o_ref[:, D_C * h: D_C * (h + 1)] = acc_ref[blk, :] / l_ref[blk, :]


def _fox_kernel(W, q_ref, kv_ref, pq_ref, pk_ref, o_ref, qs, m_ref, l_ref, acc_ref):
    qb = pl.program_id(1)
    kw = W * TK
    for h in range(H_FOX):
        qs[TQ * h: TQ * (h + 1), :] = q_ref[:, LANE * h: LANE * (h + 1)]
    _init_state(m_ref, l_ref, acc_ref)
    pq = pq_ref[...]

    def tile(j):
        kv = kv_ref[pl.ds(pl.multiple_of(j * kw, kw), kw), :]
        pk = pk_ref[j]
        decay = jnp.concatenate([pq[:, h: h + 1] - pk[h: h + 1, :] for h in range(H_FOX)], 0)
        return _qk(qs[...], kv[:, :LANE]) + decay, kv[:, LANE:]

    _causal_sweep(qb, W, H_FOX * TQ, tile, m_ref, l_ref, acc_ref)
    for h in range(H_FOX):
        blk = slice(TQ * h, TQ * (h + 1))
        kvh = h // G_FOX
        o_ref[:, DH * h: DH * (h + 1)] = acc_ref[blk, DH * kvh: DH * (kvh + 1)] / l_ref[blk, :]


def _diff_rows():
    out = []
    for kvh in range(KV_DIFF):
        for mp in range(2):
            for g in range(G_DIFF):
                out.append((kvh * G_DIFF + g, mp))
    return out


def _diff_kernel(W, q_ref, kv_ref, tb_ref, dl_ref, lam0_ref, o_ref, qs, m_ref, l_ref, acc_ref):
    qb = pl.program_id(1)
    kw = W * TK
    rows = _diff_rows()
    for r in range(len(rows)):
        qs[TQ * r: TQ * (r + 1), :] = q_ref[:, LANE * r: LANE * (r + 1)]
    _init_state(m_ref, l_ref, acc_ref)

    def tile(j):
        kv = kv_ref[pl.ds(pl.multiple_of(j * kw, kw), kw), :]
        s = _qk(qs[...], kv[:, :LANE]) + _bias_tiles(tb_ref, [h for h, _ in rows], qb, W, j)
        return s, kv[:, LANE:]

    _causal_sweep(qb, W, len(rows) * TQ, tile, m_ref, l_ref, acc_ref)

    dl = dl_ref[...]
    lam = (jnp.exp(jnp.sum(dl[0:1] * dl[1:2], -1, keepdims=True))
           - jnp.exp(jnp.sum(dl[2:3] * dl[3:4], -1, keepdims=True)) + lam0_ref[...])
    for kvh in range(KV_DIFF):
        for g in range(G_DIFF):
            h = kvh * G_DIFF + g
            b0 = slice(TQ * rows.index((h, 0)), TQ * (rows.index((h, 0)) + 1))
            b1 = slice(TQ * rows.index((h, 1)), TQ * (rows.index((h, 1)) + 1))
            vs = slice(2 * DD * kvh, 2 * DD * (kvh + 1))
            o0 = acc_ref[b0, vs] / l_ref[b0, :]
            o1 = acc_ref[b1, vs] / l_ref[b1, :]
            o_ref[:, 2 * DD * h: 2 * DD * (h + 1)] = o0 - lam * o1


def _nsa_kernel(W, q_ref, kvc_ref, sel_ref, win_ref, cb_ref, tb_ref, o_ref, qs, m_ref, l_ref, acc_ref, smask):
    qb = pl.program_id(1)
    R = H_NSA * TQ
    kw = W * TK
    nkb = smask.shape[0] * W
    n_sel = nkb * (TK // SEL_BLK)
    n_pair = kvc_ref.shape[1]
    qs[...] = jnp.zeros(qs.shape, BF)
    for h in range(H_NSA):
        qs[TQ * h: TQ * (h + 1), :] = q_ref[:, LANE * h: LANE * (h + 1)]
    q = qs[...]

    ke = kvc_ref[0].astype(BF)
    ko = kvc_ref[1].astype(BF)
    ri = lax.broadcasted_iota(jnp.int32, (R, n_pair), 0) & (TQ - 1)
    li = lax.broadcasted_iota(jnp.int32, (R, n_pair), 1)
    qpos = qb * TQ + ri
    me = qpos >= SEL_BLK * li + (CMP_BLK - 1)
    mo = qpos >= SEL_BLK * li + (SEL_BLK - 1)
    se = jnp.where(me, _qk(q, ke) + cb_ref[0], NEG_INF)
    so = jnp.where(mo, _qk(q, ko) + cb_ref[1], NEG_INF)
    mx = jnp.maximum(jnp.max(se, -1, keepdims=True), jnp.max(so, -1, keepdims=True))
    ee = jnp.where(me, jnp.exp(se - mx), 0.0)
    eo = jnp.where(mo, jnp.exp(so - mx), 0.0)
    den = jnp.maximum(jnp.sum(ee, -1, keepdims=True) + jnp.sum(eo, -1, keepdims=True), 1e-30)
    pe = ee / den
    po = eo / den
    ocmp = (jnp.dot(pe.astype(BF), ke, preferred_element_type=F32)
            + jnp.dot(po.astype(BF), ko, preferred_element_type=F32))
    for h in range(H_NSA):
        o_ref[:, DH * h: DH * (h + 1)] = ocmp[TQ * h: TQ * (h + 1), DH:]
    pp = pe + po
    imp = pp[0:TQ]
    for h in range(1, H_NSA):
        imp = imp + pp[TQ * h: TQ * (h + 1)]

    bi = lax.broadcasted_iota(jnp.int32, (TQ, n_pair), 1)
    qi = qb * TQ + lax.broadcasted_iota(jnp.int32, (TQ, n_pair), 0)
    cur = lax.shift_right_logical(qi, int(math.log2(SEL_BLK)))
    valid = bi <= cur
    forced = valid & ((bi == 0) | (cur - bi < N_LOCAL))
    score = jnp.where(valid, jnp.where(forced, FORCE, imp), -FORCE)
    rank = jnp.zeros((TQ, n_pair), F32)
    for i in range(n_sel):
        col = score[:, i: i + 1]
        ahead = (col > score) | ((col == score) & (bi > i))
        rank = rank + jnp.where(ahead, 1.0, 0.0)
    madd = jnp.where(rank < TOP_N, 0.0, NEG_INF)
    lane = lax.broadcasted_iota(jnp.int32, (TQ, TK), 1)
    for j in range(nkb // W):
        cols = [jnp.where(lane < SEL_BLK, madd[:, 2 * kb: 2 * kb + 1], madd[:, 2 * kb + 1: 2 * kb + 2])
                for kb in range(W * j, W * (j + 1))]
        smask[j] = cols[0] if W == 1 else jnp.concatenate(cols, 1)

    _init_state(m_ref, l_ref, acc_ref)
    heads = list(range(H_NSA))

    def sel_tile(j):
        kv = sel_ref[pl.ds(pl.multiple_of(j * kw, kw), kw), :]
        s = _qk(q, kv) + _bias_tiles(tb_ref, heads, qb, W, j) + jnp.concatenate([smask[j]] * H_NSA, 0)
        return s, kv

    _causal_sweep(qb, W, R, sel_tile, m_ref, l_ref, acc_ref)
    for h in range(H_NSA):
        blk = slice(TQ * h, TQ * (h + 1))
        o_ref[:, BR_W + DH * h: BR_W + DH * (h + 1)] = acc_ref[blk, DH:] / l_ref[blk, :]

    _init_state(m_ref, l_ref, acc_ref)
    rr = lax.broadcasted_iota(jnp.int32, (R, TK), 0) & (TQ - 1)
    cc = lax.broadcasted_iota(jnp.int32, (R, TK), 1)
    back = WINDOW // TK
    tiles = []
    for u in range(back + 1):
        kb = qb - (back - u)
        kv = win_ref[pl.ds(pl.multiple_of(jnp.maximum(kb, 0) * TK, TK), TK), :]
        dist = (back - u) * TQ + rr - cc
        ok = (dist >= 0) & (dist <= WINDOW) & (kb >= 0)
        bias = jnp.concatenate([tb_ref[h, min(back - u, 2)] for h in heads], 0)
        tiles.append((jnp.where(ok, _qk(q, kv) + bias, NEG_INF), kv, False))
    _online_multi(tiles, m_ref, l_ref, acc_ref)
    for h in range(H_NSA):
        blk = slice(TQ * h, TQ * (h + 1))
        o_ref[:, 2 * BR_W + DH * h: 2 * BR_W + DH * (h + 1)] = acc_ref[blk, DH:] / l_ref[blk, :]


def _attn_call(kern, B, S, ins, specs, out_w, scratch, name):
    nq = S // TQ
    return pl.pallas_call(
        kern,
        grid=(B, nq),
        in_specs=specs,
        out_specs=pl.BlockSpec((TQ, out_w), lambda b, i: (b * nq + i, 0)),
        out_shape=jax.ShapeDtypeStruct((B * S, out_w), F32),
        scratch_shapes=scratch,
        compiler_params=pltpu.CompilerParams(dimension_semantics=("arbitrary", "arbitrary"),
                                             vmem_limit_bytes=VMEM_LIMIT),
        name=name,
    )(*ins)


def _state_scratch(rows, qw, dv):
    return [pltpu.VMEM((rows, qw), BF), pltpu.VMEM((rows, 1), F32), pltpu.VMEM((rows, 1), F32),
            pltpu.VMEM((rows, dv), F32)]


def _post_kernel(x_ref, olat_ref, onsa_ref, ofox_ref, odiff_ref, misc_ref, z_ref, wuv_ref, dg_ref, ds_ref,
                 wbr_ref, wm_ref, wo_ref, lg_ref, lb_ref, o_ref):
    x = x_ref[...]
    xb = x.astype(BF)
    gate = jax.nn.sigmoid(misc_ref[:, MISC_GATE: MISC_GATE + 3 * H_NSA])
    onsa = onsa_ref[...]
    pieces = []
    for h in range(H_NSA):
        hs = slice(DH * h, DH * (h + 1))
        pieces.append(gate[:, 3 * h: 3 * h + 1] * onsa[:, hs]
                      + gate[:, 3 * h + 1: 3 * h + 2] * onsa[:, BR_W + DH * h: BR_W + DH * (h + 1)]
                      + gate[:, 3 * h + 2: 3 * h + 3] * onsa[:, 2 * BR_W + DH * h: 2 * BR_W + DH * (h + 1)])
    oa = jnp.concatenate(pieces, -1)
    ob = jnp.dot(olat_ref[...].astype(BF), wuv_ref[...], preferred_element_type=F32)
    oc = ofox_ref[...]
    od_raw = odiff_ref[...]
    pieces = []
    for h in range(H_DIFF):
        v = od_raw[:, 2 * DD * h: 2 * DD * (h + 1)]
        pieces.append(v * lax.rsqrt(jnp.mean(v * v, -1, keepdims=True) + 1e-5))
    od = jnp.concatenate(pieces, -1) * dg_ref[...] * ds_ref[...]
    z = z_ref[...]
    zs = z * jax.nn.sigmoid(z)
    acc = jnp.zeros(x.shape, F32)
    for n, o in enumerate((oa, ob, oc, od)):
        on = (o * zs[:, BR_W * n: BR_W * (n + 1)]).astype(BF)
        br = jnp.dot(on, wbr_ref[n], preferred_element_type=F32)
        g = jax.nn.sigmoid(jnp.dot(xb, wm_ref[:, D_MODEL * n: D_MODEL * (n + 1)], preferred_element_type=F32))
        acc = acc + g * br
    y = jnp.dot(acc.astype(BF), wo_ref[...], preferred_element_type=F32)
    t = DEEPNORM_ALPHA * x + y
    mu = jnp.mean(t, -1, keepdims=True)
    var = jnp.mean(jnp.square(t - mu), -1, keepdims=True)
    o_ref[...] = (t - mu) * lax.rsqrt(var + 1e-5) * lg_ref[...] + lb_ref[...]


def _post(x2, olat, onsa, ofox, odiff, misc, z, wuv, dg, ds, wbr, wm, wo, lg, lb, tm):
    M = x2.shape[0]
    row = lambda i: (i, 0)
    cst2 = lambda i: (0, 0)
    cst3 = lambda i: (0, 0, 0)
    acts = (x2, olat, onsa, ofox, odiff, misc, z)
    return pl.pallas_call(
        _post_kernel,
        grid=(M // tm,),
        in_specs=[pl.BlockSpec((tm, a.shape[1]), row) for a in acts]
        + [pl.BlockSpec(wuv.shape, cst2), pl.BlockSpec(dg.shape, cst2), pl.BlockSpec(ds.shape, cst2),
           pl.BlockSpec(wbr.shape, cst3), pl.BlockSpec(wm.shape, cst2), pl.BlockSpec(wo.shape, cst2),
           pl.BlockSpec(lg.shape, cst2), pl.BlockSpec(lb.shape, cst2)],
        out_specs=pl.BlockSpec((tm, D_MODEL), row),
        out_shape=jax.ShapeDtypeStruct((M, D_MODEL), F32),
        compiler_params=pltpu.CompilerParams(dimension_semantics=("arbitrary",), vmem_limit_bytes=VMEM_LIMIT),
        name="post_merge",
    )(*acts, wuv, dg, ds, wbr, wm, wo, lg, lb)


def _layer_weights(l, w_in, cmp_pos, cmp_w1, cmp_w2, mla_gq, mla_gkv, mla_wuq, mla_wuk, mla_wuv,
                   fox_bf, diff_lam, diff_g, w_branch, w_out, ln_g, ln_b):
    import numpy as np
    wp = _permute_cols(w_in[l, :, :N_SMALL], _small_proj_columns()).astype(BF)
    wm = w_in[l, :, N_SMALL:].astype(BF)
    wuq = _permute_cols(mla_wuq[l], _uq_columns()).astype(BF)
    eye = jnp.eye(H_MLA, dtype=F32)
    wukbd = jnp.einsum('chn,hg->hngc', mla_wuk[l], eye).reshape(H_MLA * D_NOPE, H_MLA * D_C).astype(BF)
    wuvbd = jnp.einsum('chv,hg->hcgv', mla_wuv[l], eye).reshape(H_MLA * D_C, H_MLA * D_V).astype(BF)
    fb = jnp.zeros((1, LANE), F32).at[0, MISC_F: MISC_F + H_FOX].set(fox_bf[l])
    w1 = cmp_w1[l].reshape(2, CMP_BLK, DH, DH)
    wc = jnp.einsum('cldh,ce->lcdeh', w1, jnp.eye(2, dtype=F32)).reshape(CMP_BLK * 2 * DH, 2 * DH).astype(BF)
    w2 = jnp.einsum('che,cf->chfe', cmp_w2[l], jnp.eye(2, dtype=F32)).reshape(2 * DH, 2 * DH).astype(BF)
    cpos = jnp.swapaxes(cmp_pos[l], 0, 1).reshape(1, CMP_BLK * 2 * DH)
    lam_init = 0.8 - 0.6 * math.exp(-0.3 * l)
    return dict(
        wp=wp, wm=wm, wuq=wuq, wukbd=wukbd, wuvbd=wuvbd, fb=fb, wc=wc, w2=w2, cpos=cpos,
        gq=mla_gq[l][None], gkv=mla_gkv[l][None], dl=diff_lam[l],
        lam0=jnp.full((1, 1), lam_init, F32), ds=jnp.full((1, 1), 1.0 - lam_init, F32),
        dg=jnp.tile(diff_g[l], H_DIFF)[None], wbr=w_branch[l].astype(BF), wo=w_out[l].astype(BF),
        lg=ln_g[l][None], lb=ln_b[l][None])


def _prompt_bias_tables(rel_bias, S):
    nq = S // TQ
    i = jnp.arange(TQ)[:, None]
    j = jnp.arange(TK)[None, :]
    tiles = jnp.stack([rel_bias[_t5_bucket(d * TQ + i - j)] for d in range(3)])
    tb = jnp.moveaxis(tiles, -1, 0)
    n_cmp = S // CMP_BLK
    ends = (jnp.arange(n_cmp) + 1) * CMP_BLK - 1
    dist = jnp.arange(S)[:, None] - ends[None, :]
    cb = rel_bias[:, :H_NSA][_t5_bucket(dist)]
    cb = cb.reshape(nq, TQ, n_cmp // 2, 2, H_NSA)
    cb = jnp.transpose(cb, (0, 3, 4, 1, 2)).reshape(nq, 2, H_NSA * TQ, n_cmp // 2)
    return tb[:H_NSA], tb[H_NSA:], cb


def _prompt_layer(x, lw, tabs, tb_a, tb_d, cb, u_tri):
    B, S, _ = x.shape
    M = B * S
    nq = S // TQ
    W = _key_tiles(S)
    kw = W * TK
    x2 = x.reshape(M, D_MODEL)
    (qa, cmp_s, sel_s, win_s, selb, winb, misc, lat, kcat, qcat, fq, fkv, fkvb, dq, dkv, dkvb, z) = _proj(
        x2, lw['wp'], lw['wuq'], lw['wukbd'], lw['gq'], lw['gkv'], lw['fb'], tabs, 256)

    n_cmp = S // CMP_BLK
    kvc = _cmp_summary(cmp_s.reshape(1, M // CMP_BLK, CMP_BLK * 2 * DH), 0, lw['cpos'], lw['wc'], lw['w2'],
                       min(256, M // CMP_BLK))
    kvc = jnp.swapaxes(kvc.reshape(B, n_cmp // 2, 2, LANE), 1, 2)

    qrow = lambda w: pl.BlockSpec((TQ, w), lambda b, i: (b * nq + i, 0))
    kvrow = lambda w: pl.BlockSpec((S, w), lambda b, i: (b, 0))
    whole = lambda a: pl.BlockSpec(a.shape, lambda b, i: (0,) * a.ndim)

    onsa = _attn_call(
        functools.partial(_nsa_kernel, W), B, S, (qa, kvc, selb, winb, cb, tb_a),
        [qrow(512), pl.BlockSpec((None, 2, n_cmp // 2, LANE), lambda b, i: (b, 0, 0, 0)), kvrow(LANE), kvrow(LANE),
         pl.BlockSpec((None, 2, H_NSA * TQ, n_cmp // 2), lambda b, i: (i, 0, 0, 0)), whole(tb_a)],
        3 * BR_W, _state_scratch(H_NSA * TQ, LANE, LANE) + [pltpu.VMEM((S // kw, TQ, kw), F32)], "nsa_prompt")

    olat = _attn_call(functools.partial(_mla_kernel, W), B, S, (qcat, kcat), [qrow(H_MLA * (D_C + LANE)), kvrow(D_C + LANE)],
                      H_MLA * D_C, _state_scratch(H_MLA * TQ, D_C + LANE, D_C), "mla_prompt")

    logf = misc[:, MISC_F: MISC_F + H_FOX].reshape(B, S, H_FOX)
    pre = _prefix(jnp.swapaxes(logf, 1, 2).reshape(B * H_FOX, S), u_tri).reshape(B, H_FOX, S)
    pq = jnp.swapaxes(pre, 1, 2).reshape(M, H_FOX)
    pk = jnp.swapaxes(pre.reshape(B, H_FOX, S // kw, kw), 1, 2)
    pk = jnp.pad(pk, ((0, 0), (0, 0), (0, 8 - H_FOX), (0, 0)))
    ofox = _attn_call(
        functools.partial(_fox_kernel, W), B, S, (fq, fkvb, pq, pk),
        [qrow(512), kvrow(256), qrow(H_FOX), pl.BlockSpec((None, S // kw, 8, kw), lambda b, i: (b, 0, 0, 0))],
        BR_W, _state_scratch(H_FOX * TQ, LANE, LANE), "fox_prompt")

    odiff = _attn_call(
        functools.partial(_diff_kernel, W), B, S, (dq, dkvb, tb_d, lw['dl'], lw['lam0']),
        [qrow(1024), kvrow(256), whole(tb_d), whole(lw['dl']), whole(lw['lam0'])],
        BR_W, _state_scratch(2 * H_DIFF * TQ, LANE, LANE), "diff_prompt")

    x_new = _post(x2, olat, onsa, ofox, odiff, misc, z, lw['wuvbd'], lw['dg'], lw['ds'], lw['wbr'], lw['wm'],
                  lw['wo'], lw['lg'], lw['lb'], 256).reshape(B, S, D_MODEL)
    wlen = min(WINDOW, S)
    states = (cmp_s.reshape(B, S, 2, DH), sel_s.reshape(B, S, 2, DH),
              win_s.reshape(B, S, 2, DH)[:, S - wlen:],
              lat.reshape(B, S, D_C), misc[:, MISC_KPE: MISC_KPE + D_ROPE].reshape(B, S, D_ROPE),
              fkv.reshape(B, S, 2, KV_FOX, DH), logf, dkv.reshape(B, S, 2, KV_DIFF, 2 * DD))
    return x_new, states


PG = 4
GATHER_G = 16


def _gather_rows_kernel(pt_ref, *refs):
    o_ref = refs[-1]
    for g, r in enumerate(refs[:-1]):
        o_ref[g: g + 1, :] = r[...]


def _gather_rows(pool3, page_table, gg):
    DB, NP = page_table.shape
    W = pool3.shape[-1]
    return pl.pallas_call(
        _gather_rows_kernel,
        grid_spec=pltpu.PrefetchScalarGridSpec(
            num_scalar_prefetch=1, grid=(DB, NP // gg),
            in_specs=[pl.BlockSpec((None, 1, W), lambda b, t, pt, g=g: (pt[b, t * gg + g], 0, 0))
                      for g in range(gg)],
            out_specs=pl.BlockSpec((None, gg, W), lambda b, t, pt: (b, t, 0))),
        out_shape=jax.ShapeDtypeStruct((DB, NP, W), F32),
        compiler_params=pltpu.CompilerParams(dimension_semantics=("arbitrary", "arbitrary")),
        name="gather_summaries",
    )(page_table, *([pool3] * gg))


def _suffix_kernel(x_ref, ugt_ref, one_ref, sfx_ref, tot_ref):
    x = x_ref[...]
    sfx_ref[...] = _dot3(x, ugt_ref[...])
    tot_ref[...] = _dot3(x, one_ref[...])


def _page_suffix(x, tr):
    R = x.shape[0]
    ugt = (jnp.arange(LANE)[:, None] > jnp.arange(LANE)[None, :]).astype(BF)
    one = jnp.ones((LANE, LANE), BF)
    return pl.pallas_call(
        _suffix_kernel,
        grid=(R // tr,),
        in_specs=[pl.BlockSpec((tr, LANE), lambda i: (i, 0)), pl.BlockSpec((LANE, LANE), lambda i: (0, 0)),
                  pl.BlockSpec((LANE, LANE), lambda i: (0, 0))],
        out_specs=[pl.BlockSpec((tr, LANE), lambda i: (i, 0))] * 2,
        out_shape=[jax.ShapeDtypeStruct((R, LANE), F32)] * 2,
        compiler_params=pltpu.CompilerParams(dimension_semantics=("arbitrary",)),
        name="fox_page_suffix",
    )(x, ugt, one)


def _online_multi(tiles, m_ref, l_ref, acc_ref):
    m_prev = m_ref[...]
    m_new = m_prev
    for s, _, _ in tiles:
        m_new = jnp.maximum(m_new, jnp.max(s, -1, keepdims=True))
    alpha = jnp.exp(m_prev - m_new)
    l_new = alpha * l_ref[...]
    acc = alpha * acc_ref[...]
    for s, v, v_t in tiles:
        p = jnp.exp(s - m_new)
        l_new = l_new + jnp.sum(p, -1, keepdims=True)
        pb = p.astype(BF)
        acc = acc + (_qk(pb, v) if v_t else jnp.dot(pb, v, preferred_element_type=F32))
    l_ref[...] = l_new
    acc_ref[...] = acc
    m_ref[...] = m_new


def _nsa_sample_kernel(dq, n_win, q_ref, kvc_ref, cb_ref, win_ref, wt_ref, wb1_ref, wb2_ref, o_ref, sm_ref):
    q = q_ref[...]
    rows = q.shape[0]
    n_pair = kvc_ref.shape[1]
    ke = kvc_ref[0].astype(BF)
    ko = kvc_ref[1].astype(BF)
    se = _qk(q, ke) + cb_ref[0]
    so = _qk(q, ko) + cb_ref[1]
    mx = jnp.maximum(jnp.max(se, -1, keepdims=True), jnp.max(so, -1, keepdims=True))
    ee = jnp.exp(se - mx)
    eo = jnp.exp(so - mx)
    den = jnp.maximum(jnp.sum(ee, -1, keepdims=True) + jnp.sum(eo, -1, keepdims=True), 1e-30)
    pe = ee / den
    po = eo / den
    o_ref[:, :LANE] = (jnp.dot(pe.astype(BF), ke, preferred_element_type=F32)
                       + jnp.dot(po.astype(BF), ko, preferred_element_type=F32))
    pp = pe + po
    a = pp[0:8] + pp[8:16]
    imp = a + pltpu.roll(a, dq, 0)
    bi = lax.broadcasted_iota(jnp.int32, (8, n_pair), 1)
    forced = (bi == 0) | (n_pair - bi < N_LOCAL)
    score = jnp.where(forced, FORCE, imp)
    rank = jnp.where(forced, 0.0, 1.0)
    for i in range(n_pair):
        col = score[:, i: i + 1]
        ahead = (col > score) | ((col == score) & (bi > i))
        rank = rank + jnp.where(ahead, 1.0, 0.0)
    madd = jnp.where(rank < TOP_N, 0.0, NEG_INF)
    sm_ref[...] = jnp.concatenate([madd] * (rows // 8), 0)

    wp = win_ref[...].astype(BF)
    wt = wt_ref[...]
    qi1 = lax.broadcasted_iota(jnp.int32, (rows, n_win), 0) & (dq - 1)
    t1 = lax.broadcasted_iota(jnp.int32, (rows, n_win), 1)
    ok1 = n_win + qi1 - t1 <= WINDOW
    qi2 = lax.broadcasted_iota(jnp.int32, (rows, LANE), 0) & (dq - 1)
    ok2 = lax.broadcasted_iota(jnp.int32, (rows, LANE), 1) <= qi2
    s1 = jnp.where(ok1, jnp.dot(q, wp, preferred_element_type=F32) + wb1_ref[...], NEG_INF)
    s2 = jnp.where(ok2, _qk(q, wt) + wb2_ref[...], NEG_INF)
    mx = jnp.maximum(jnp.max(s1, -1, keepdims=True), jnp.max(s2, -1, keepdims=True))
    e1 = jnp.where(ok1, jnp.exp(s1 - mx), 0.0)
    e2 = jnp.where(ok2, jnp.exp(s2 - mx), 0.0)
    den = jnp.maximum(jnp.sum(e1, -1, keepdims=True) + jnp.sum(e2, -1, keepdims=True), 1e-30)
    o_ref[:, LANE:] = (_qk((e1 / den).astype(BF), wp)
                       + jnp.dot((e2 / den).astype(BF), wt, preferred_element_type=F32))


def _stream_kernel(dq, nsteps, pt_ref, qml_ref, qmp_ref, qf_ref, qd_ref, qs_ref, sm_ref, *rest):
    npg = 7 * PG
    pages = rest[:npg]
    (tl_ref, tf_ref, td_ref, ts_ref, tlf_ref, dbl_ref, dbt_ref, dbc_ref, sbl_ref, sbt_ref, sbc_ref,
     dl_ref, lam0_ref) = rest[npg: npg + 13]
    om_ref, of_ref, od_ref, os_ref = rest[npg + 13: npg + 17]
    (mm, lm, am, mf, lf, af, md, ld, ad, ms, ls, as_, carry) = rest[npg + 17:]
    t = pl.program_id(1)

    @pl.when(t == 0)
    def _():
        _init_state(mm, lm, am)
        _init_state(mf, lf, af)
        _init_state(md, ld, ad)
        _init_state(ms, ls, as_)
        carry[...] = jnp.zeros(carry.shape, F32)

    qml, qmp, qf, qd, qs = qml_ref[...], qmp_ref[...], qf_ref[...], qd_ref[...], qs_ref[...]
    tlf = tlf_ref[...]
    cs = [tlf[:, 0:1]]
    for j in range(1, dq):
        cs.append(cs[-1] + tlf[:, j: j + 1])
    qrow = lax.broadcasted_iota(jnp.int32, (qf.shape[0], 1), 0) & (dq - 1)
    cq = cs[dq - 1]
    for j in range(dq - 2, -1, -1):
        cq = jnp.where(qrow == j, cs[j], cq)

    lane = lax.broadcasted_iota(jnp.int32, (qs.shape[0], LANE), 1)
    sm = sm_ref[...]
    first = t == 0
    mla_t, fox_t, diff_t, sel_t = [], [], [], []
    run = carry[...]
    for g in range(PG):
        lat_r, kpe_r, fox_r, sfx_r, tot_r, diff_r, sel_r = pages[7 * g: 7 * g + 7]
        lat = lat_r[...].astype(BF)
        kpe = kpe_r[...].astype(BF)
        s = _qk(qml, lat) + jnp.dot(qmp[:, :D_ROPE], kpe, preferred_element_type=F32)
        mla_t.append((s, lat, False))
        fkv = fox_r[...].astype(BF)
        s = jnp.dot(qf, fkv[:LANE], preferred_element_type=F32) + (cq + run + sfx_r[...])
        fox_t.append((s, fkv[LANE:], True))
        run = run + tot_r[:, 0:1]
        dkv = diff_r[...].astype(BF)
        dbias = jnp.where(first, dbl_ref[...], dbc_ref[...]) if g == 0 else dbc_ref[...]
        diff_t.append((jnp.dot(qd, dkv[:LANE], preferred_element_type=F32) + dbias, dkv[LANE:], True))
        skv = sel_r[...].astype(BF)
        sbias = jnp.where(first, sbl_ref[...], sbc_ref[...]) if g == 0 else sbc_ref[...]
        madd = jnp.where(lane < SEL_BLK, sm[:, 2 * g: 2 * g + 1], sm[:, 2 * g + 1: 2 * g + 2])
        sel_t.append((jnp.dot(qs, skv, preferred_element_type=F32) + sbias + madd, skv, True))
    carry[...] = run
    _online_multi(mla_t, mm, lm, am)
    _online_multi(fox_t, mf, lf, af)
    _online_multi(diff_t, md, ld, ad)
    _online_multi(sel_t, ms, ls, as_)

    @pl.when(t == nsteps - 1)
    def _():
        def causal(rows):
            qi = lax.broadcasted_iota(jnp.int32, (rows, LANE), 0) & (dq - 1)
            return lax.broadcasted_iota(jnp.int32, (rows, LANE), 1) <= qi

        tl, tf, td, ts = tl_ref[...], tf_ref[...], td_ref[...], ts_ref[...]
        s = _qk(qml, tl[:, :D_C]) + _qk(qmp, tl[:, D_C:])
        _online_multi([(jnp.where(causal(qml.shape[0]), s, NEG_INF), tl[:, :D_C], False)], mm, lm, am)
        ctile = jnp.zeros((qf.shape[0], LANE), F32)
        for j in range(dq):
            ctile = jnp.where(lane == j, cs[j], ctile)
        s = _qk(qf, tf[:, :LANE]) + (cq - ctile)
        _online_multi([(jnp.where(causal(qf.shape[0]), s, NEG_INF), tf[:, LANE:], False)], mf, lf, af)
        s = _qk(qd, td[:, :LANE]) + dbt_ref[...]
        _online_multi([(jnp.where(causal(qd.shape[0]), s, NEG_INF), td[:, LANE:], False)], md, ld, ad)
        s = _qk(qs, ts) + sbt_ref[...]
        _online_multi([(jnp.where(causal(qs.shape[0]), s, NEG_INF), ts, False)], ms, ls, as_)
        om_ref[...] = am[...] / lm[...]
        of_ref[...] = af[...] / lf[...]
        os_ref[...] = as_[...] / ls[...]
        dl = dl_ref[...]
        lam = (jnp.exp(jnp.sum(dl[0:1] * dl[1:2], -1, keepdims=True))
               - jnp.exp(jnp.sum(dl[2:3] * dl[3:4], -1, keepdims=True)) + lam0_ref[...])
        od = ad[...] / ld[...]
        half = G_DIFF * dq
        for kvh in range(KV_DIFF):
            o0 = od[2 * half * kvh: 2 * half * kvh + half]
            o1 = od[2 * half * kvh + half: 2 * half * (kvh + 1)]
            od_ref[half * kvh: half * (kvh + 1), :] = o0 - lam * o1


def _sample_bias_tables(rel_bias, past_len, dq, n_win):
    tbl_a, tbl_d = rel_bias[:, :H_NSA], rel_bias[:, H_NSA:]
    q = jnp.arange(dq)[:, None]

    def rows_hq(tbl, dist):
        b = tbl[_t5_bucket(dist)]
        return jnp.transpose(b, (2, 0, 1)).reshape(tbl.shape[1] * dq, dist.shape[1])

    n_cmp = past_len // CMP_BLK
    ends = (jnp.arange(n_cmp) + 1) * CMP_BLK - 1
    cb = rows_hq(tbl_a, past_len + q - ends[None, :]).reshape(H_NSA * dq, n_cmp // 2, 2)
    cb = jnp.moveaxis(cb, -1, 0)
    key = jnp.arange(LANE)[None, :]
    wb1 = rows_hq(tbl_a, n_win + q - jnp.arange(n_win)[None, :])
    near = rows_hq(tbl_a, PAGE_SIZE + q - key)
    tail = rows_hq(tbl_a, q - key)
    far = rows_hq(tbl_a, 2 * PAGE_SIZE + q - key)[:, :1]

    def diff_rows(x):
        x = x.reshape(KV_DIFF, 1, G_DIFF, dq, x.shape[-1])
        return jnp.broadcast_to(x, (KV_DIFF, 2, G_DIFF, dq, x.shape[-1])).reshape(2 * H_DIFF * dq, x.shape[-1])

    dnear = diff_rows(rows_hq(tbl_d, PAGE_SIZE + q - key))
    dtail = diff_rows(rows_hq(tbl_d, q - key))
    dfar = diff_rows(rows_hq(tbl_d, 2 * PAGE_SIZE + q - key)[:, :1])
    return cb, wb1, near, tail, far, dnear, dtail, dfar


def _rows_hq(a, DB, dq, H):
    W = a.shape[1] // H
    return jnp.transpose(a.reshape(DB, dq, H, W), (0, 2, 1, 3)).reshape(DB, H * dq, W)


def _rows_token(a, DB, dq, H):
    W = a.shape[-1]
    return jnp.transpose(a.reshape(DB, H, dq, W), (0, 2, 1, 3)).reshape(DB * dq, H * W)


def _pad_tail(a, DB, dq):
    a = a.reshape(DB, dq, a.shape[-1])
    return jnp.pad(a, ((0, 0), (0, LANE - dq), (0, 0)))


def _sample_layer(x, l, lw, tabs, bias, page_table, caches, sfx, tot):
    (c_cmp, c_sel, st_win, win_prev, c_lat, c_kpe, c_fox, c_diff) = caches
    cb, wb1, near, tail, far, dnear, dtail, dfar = bias
    DB, dq, _ = x.shape
    M = DB * dq
    NP = page_table.shape[1]
    n_pool = c_lat.shape[1]
    n_win = st_win.shape[3]
    assert dq & (dq - 1) == 0 and H_NSA * dq == 16 and NP % PG == 0
    x2 = x.reshape(M, D_MODEL)
    (qa, cmp_s, sel_s, win_s, selb, winb, misc, lat, kcat, qcat, fq, fkv, fkvb, dq_, dkv, dkvb, z) = _proj(
        x2, lw['wp'], lw['wuq'], lw['wukbd'], lw['gq'], lw['gkv'], lw['fb'], tabs, M)

    kvc_pool = _cmp_pool_summary(c_cmp, l, lw['cpos'], lw['wc'], lw['w2'],
                                 CMP_PAGES if n_pool % CMP_PAGES == 0 else n_pool)
    gg = min(GATHER_G, NP)
    kvc = _gather_rows(kvc_pool.reshape(n_pool, 1, (PAGE_SIZE // CMP_BLK) * LANE), page_table, gg)
    n_cmp = NP * (PAGE_SIZE // CMP_BLK)
    kvc = jnp.swapaxes(kvc.reshape(DB, n_cmp // 2, 2, LANE), 1, 2)

    q_nsa = _rows_hq(qa, DB, dq, H_NSA)
    per_b = lambda *blk: pl.BlockSpec((None,) + blk, lambda b: (b,) + (0,) * len(blk))
    cst = lambda a: pl.BlockSpec(a.shape, lambda b: (0,) * a.ndim)
    wt = _pad_tail(winb, DB, dq)
    ocw, selmask = pl.pallas_call(
        lambda *r: _nsa_sample_kernel(dq, n_win, *r),
        grid=(DB,),
        in_specs=[per_b(16, LANE), per_b(2, n_cmp // 2, LANE), cst(cb),
                  pl.BlockSpec((None, None, LANE, n_win), lambda b: (l, b, 0, 0)), per_b(LANE, LANE),
                  cst(wb1), cst(tail)],
        out_specs=[per_b(16, 2 * LANE), per_b(16, n_cmp // 2)],
        out_shape=[jax.ShapeDtypeStruct((DB, 16, 2 * LANE), F32), jax.ShapeDtypeStruct((DB, 16, n_cmp // 2), F32)],
        compiler_params=pltpu.CompilerParams(dimension_semantics=("arbitrary",), vmem_limit_bytes=VMEM_LIMIT),
        name="nsa_sample",
    )(q_nsa, kvc, cb, st_win, wt, wb1, tail)

    nsteps = NP // PG
    sm = selmask.reshape(DB, 16, NP, 2)[:, :, ::-1]
    sm = jnp.transpose(sm.reshape(DB, 16, nsteps, PG * 2), (0, 2, 1, 3))

    qm = _rows_hq(qcat, DB, dq, H_MLA)
    qml, qmp = qm[..., :D_C], qm[..., D_C:]
    qf = _rows_hq(fq, DB, dq, H_FOX)
    qd = _rows_hq(dq_, DB, dq, 2 * H_DIFF)
    logf_new = misc[:, MISC_F: MISC_F + H_FOX].reshape(DB, dq, H_FOX)
    tlf = jnp.broadcast_to(jnp.swapaxes(logf_new, 1, 2)[:, :, None, :], (DB, H_FOX, dq, dq)).reshape(DB, 16, dq)
    tlf = jnp.pad(tlf, ((0, 0), (0, 0), (0, LANE - dq)))

    def page(rows, cols):
        return [pl.BlockSpec((None, None, rows, cols),
                             lambda b, t, pt, g=g: (l, pt[b, NP - 1 - (t * PG + g)], 0, 0)) for g in range(PG)]

    lat_s, kpe_s, fox_s, sfx_s, tot_s, diff_s, sel_s_ = (
        page(PAGE_SIZE, D_C), page(D_ROPE, PAGE_SIZE), page(256, PAGE_SIZE), page(16, LANE), page(16, LANE),
        page(256, PAGE_SIZE), page(LANE, PAGE_SIZE))
    page_specs, page_args = [], []
    for g in range(PG):
        page_specs += [lat_s[g], kpe_s[g], fox_s[g], sfx_s[g], tot_s[g], diff_s[g], sel_s_[g]]
        page_args += [c_lat, c_kpe, c_fox, sfx, tot, c_diff, c_sel]
    pb = lambda *blk: pl.BlockSpec((None,) + blk, lambda b, t, pt: (b,) + (0,) * len(blk))
    cs = lambda a: pl.BlockSpec(a.shape, lambda b, t, pt: (0,) * a.ndim)
    tails = (_pad_tail(kcat, DB, dq), _pad_tail(fkvb, DB, dq), _pad_tail(dkvb, DB, dq), _pad_tail(selb, DB, dq))
    consts = (dnear, dtail, dfar, near, tail, far, lw['dl'], lw['lam0'])
    st = lambda rows, dv: [pltpu.VMEM((rows, 1), F32), pltpu.VMEM((rows, 1), F32), pltpu.VMEM((rows, dv), F32)]
    om, of, od, os_ = pl.pallas_call(
        lambda *r: _stream_kernel(dq, nsteps, *r),
        grid_spec=pltpu.PrefetchScalarGridSpec(
            num_scalar_prefetch=1, grid=(DB, nsteps),
            in_specs=[pb(16, D_C), pb(16, LANE), pb(16, LANE), pb(32, LANE), pb(16, LANE),
                      pl.BlockSpec((None, None, 16, 2 * PG), lambda b, t, pt: (b, t, 0, 0))]
            + page_specs
            + [pb(LANE, D_C + LANE), pb(LANE, 256), pb(LANE, 256), pb(LANE, LANE), pb(16, LANE)]
            + [cs(a) for a in consts],
            out_specs=[pb(16, D_C), pb(16, LANE), pb(16, LANE), pb(16, LANE)],
            scratch_shapes=st(16, D_C) + st(16, LANE) + st(32, LANE) + st(16, LANE) + [pltpu.VMEM((16, 1), F32)]),
        out_shape=[jax.ShapeDtypeStruct((DB, 16, D_C), F32), jax.ShapeDtypeStruct((DB, 16, LANE), F32),
                   jax.ShapeDtypeStruct((DB, 16, LANE), F32), jax.ShapeDtypeStruct((DB, 16, LANE), F32)],
        compiler_params=pltpu.CompilerParams(dimension_semantics=("arbitrary", "arbitrary"),
                                             vmem_limit_bytes=VMEM_LIMIT),
        name="sample_stream",
    )(page_table, qml, qmp, qf, qd, q_nsa, sm, *page_args, *tails, tlf, *consts)

    olat = _rows_token(om, DB, dq, H_MLA)

    def pick_kv(o, H, G):
        o = o.reshape(DB, H, dq, 2, DH)
        o = jnp.stack([o[:, h, :, h // G] for h in range(H)], 1)
        return _rows_token(o.reshape(DB, H * dq, DH), DB, dq, H)

    ofox = pick_kv(of, H_FOX, G_FOX)
    odiff = pick_kv(od, H_DIFF, G_DIFF)
    upper = lambda o: _rows_token(o[..., DH:], DB, dq, H_NSA)
    onsa = jnp.concatenate([upper(ocw[..., :LANE]), upper(os_), upper(ocw[..., LANE:])], -1)
    x_new = _post(x2, olat, onsa, ofox, odiff, misc, z, lw['wuvbd'], lw['dg'], lw['ds'], lw['wbr'], lw['wm'],
                  lw['wo'], lw['lg'], lw['lb'], M).reshape(DB, dq, D_MODEL)
    win_all = jnp.concatenate([win_prev[l], win_s.reshape(DB, dq, LANE)], 1)
    wlen = min(WINDOW, win_all.shape[1])
    states = (cmp_s.reshape(DB, dq, 2, DH), sel_s.reshape(DB, dq, 2, DH),
              win_all[:, win_all.shape[1] - wlen:].reshape(DB, wlen, 2, DH),
              lat.reshape(DB, dq, D_C), misc[:, MISC_KPE: MISC_KPE + D_ROPE].reshape(DB, dq, D_ROPE),
              fkv.reshape(DB, dq, 2, KV_FOX, DH), logf_new, dkv.reshape(DB, dq, 2, KV_DIFF, 2 * DD))
    return x_new, states


def kernel(x_prompt, x_sample, cache_nsa_cmp_kv, cache_nsa_sel_kv, state_nsa_win_kv, cache_mla_latent,
           cache_mla_kpe, cache_fox_kv, cache_fox_logf, cache_diff_kv, page_table, rel_bias, w_in,
           nsa_cmp_pos, nsa_cmp_w1, nsa_cmp_w2, mla_q_norm, mla_kv_norm, mla_w_uq, mla_w_uk, mla_w_uv,
           fox_forget_bias, diff_lambda, diff_subln, w_branch, w_out, ln_g, ln_b):
    past_len = page_table.shape[1] * PAGE_SIZE
    S = x_prompt.shape[1]
    DB, dq = x_sample.shape[:2]
    depth, n_pool = cache_mla_latent.shape[:2]
    tabs_s = _rope_tables(jnp.tile(past_len + jnp.arange(dq), DB))
    bias_s = _sample_bias_tables(rel_bias, past_len, dq, state_nsa_win_kv.shape[2])
    lf = jnp.swapaxes(cache_fox_logf, 2, 3)
    lf = jnp.broadcast_to(lf[:, :, :, None, :], (depth, n_pool, H_FOX, dq, PAGE_SIZE))
    n_rows = depth * n_pool * H_FOX * dq
    sfx, tot = _page_suffix(lf.reshape(n_rows, PAGE_SIZE), 2048 if n_rows % 2048 == 0 else n_rows)
    sfx = sfx.reshape(depth, n_pool, H_FOX * dq, PAGE_SIZE)
    tot = tot.reshape(depth, n_pool, H_FOX * dq, PAGE_SIZE)
    fm = lambda a, feat: jnp.swapaxes(a.reshape(a.shape[0], a.shape[1], a.shape[2], feat), 2, 3)
    win_prev = state_nsa_win_kv.reshape(depth, DB, state_nsa_win_kv.shape[2], 2 * DH)
    caches = (fm(cache_nsa_cmp_kv, 2 * DH), fm(cache_nsa_sel_kv, 2 * DH), fm(state_nsa_win_kv, 2 * DH), win_prev,
              cache_mla_latent, fm(cache_mla_kpe, D_ROPE),
              fm(cache_fox_kv, 2 * KV_FOX * DH), fm(cache_diff_kv, 2 * KV_DIFF * 2 * DD))
    tabs_p = _rope_tables(jnp.arange(S))
    tb_a, tb_d, cb = _prompt_bias_tables(rel_bias, S)
    u_tri = (jnp.arange(LANE)[:, None] <= jnp.arange(LANE)[None, :]).astype(BF)
    xp, xs = x_prompt, x_sample
    st_p, st_s = [], []
    for l in range(depth):
        lwp = _layer_weights(l, w_in, nsa_cmp_pos, nsa_cmp_w1, nsa_cmp_w2, mla_q_norm, mla_kv_norm, mla_w_uq,
                             mla_w_uk, mla_w_uv, fox_forget_bias, diff_lambda, diff_subln, w_branch, w_out,
                             ln_g, ln_b)
        xp, sp = _prompt_layer(xp, lwp, tabs_p, tb_a, tb_d, cb, u_tri)
        xs, ss = _sample_layer(xs, l, lwp, tabs_s, bias_s, page_table, caches, sfx, tot)
        st_p.append(sp)
        st_s.append(ss)
    (p_cmp, p_sel, p_win, p_lat, p_kpe, p_fkv, p_flf, p_dkv) = [jnp.stack(z) for z in zip(*st_p)]
    (s_cmp, s_sel, s_win, s_lat, s_kpe, s_fkv, s_flf, s_dkv) = [jnp.stack(z) for z in zip(*st_s)]
    return (xp, xs, p_cmp, s_cmp, p_sel, s_sel, p_win, s_win, p_lat, s_lat, p_kpe, s_kpe,
            p_fkv, s_fkv, p_flf, s_flf, p_dkv, s_dkv)
```

```python
import functools
import math
import jax
import jax.numpy as jnp
from jax import lax
from jax.experimental import pallas as pl
from jax.experimental.pallas import tpu as pltpu


D_MODEL = 1024
DEPTH = 4
PAGE_SIZE = 128

N_BRANCH = 4
BR_W = D_MODEL // N_BRANCH
DH = 64
QBLK = 128
H_NSA = 4
CMP_BLK = 32
SEL_BLK = 64
CMP_PER_SEL = SEL_BLK // CMP_BLK
TOP_N = 16
N_LOCAL = 2
WINDOW = 512
H_MLA = 4
D_CQ = 256
D_C = 256
D_NOPE = 64
D_ROPE = 32
D_V = 64
ROPE_BASE = 10000.0
MLA_SCALE = (D_NOPE + D_ROPE) ** -0.5
H_FOX = 4
KV_FOX = 2
G_FOX = H_FOX // KV_FOX
H_DIFF = 4
KV_DIFF = 2
G_DIFF = H_DIFF // KV_DIFF
DD = 32
N_BUCKETS = 32
MAX_DIST = 128
DEEPNORM_ALPHA = (2 * DEPTH) ** 0.25
NEG_INF = -1e30
FORCE = 1e9

IN_SIZES = (
    H_NSA * DH, 2 * DH, 2 * DH, 2 * DH, 3 * H_NSA, BR_W,
    D_CQ, D_C, D_ROPE, BR_W,
    H_FOX * DH, KV_FOX * DH, KV_FOX * DH, H_FOX, BR_W,
    H_DIFF * 2 * DD, KV_DIFF * 2 * DD, KV_DIFF * 2 * DD, BR_W,
    N_BRANCH * D_MODEL,
)
N_IN = sum(IN_SIZES)


def _mm_kernel(x_ref, w_ref, o_ref):
    o_ref[...] = jnp.dot(x_ref[...].astype(jnp.bfloat16), w_ref[...].astype(jnp.bfloat16),
                         preferred_element_type=jnp.float32)


def _pmatmul(x, w):
    lead = x.shape[:-1]
    K = x.shape[-1]
    N = w.shape[-1]
    x2 = x.reshape(-1, K)
    M = x2.shape[0]
    tn = 256
    n_pad = -(-N // tn) * tn
    if n_pad != N:
        w = jnp.pad(w, ((0, 0), (0, n_pad - N)))
    tm = min(M, 1024)
    out = pl.pallas_call(
        _mm_kernel,
        grid=(M // tm, n_pad // tn),
        in_specs=[pl.BlockSpec((tm, K), lambda i, j: (i, 0)),
                  pl.BlockSpec((K, tn), lambda i, j: (0, j))],
        out_specs=pl.BlockSpec((tm, tn), lambda i, j: (i, j)),
        out_shape=jax.ShapeDtypeStruct((M, n_pad), jnp.float32),
        name="matmul",
    )(x2, w)
    return out[:, :N].reshape(*lead, N)


def _split(h, sizes):
    out, start = [], 0
    for n in sizes:
        out.append(h[..., start:start + n])
        start += n
    return out


def _rmsnorm(x, g, eps=1e-6):
    xf = x.astype(jnp.float32)
    y = xf * lax.rsqrt(jnp.mean(xf * xf, -1, keepdims=True) + eps)
    return (y * g).astype(x.dtype)


def _layernorm(x, g, b, eps=1e-5):
    xf = x.astype(jnp.float32)
    mu = jnp.mean(xf, -1, keepdims=True)
    var = jnp.mean(jnp.square(xf - mu), -1, keepdims=True)
    return ((xf - mu) * lax.rsqrt(var + eps) * g + b).astype(x.dtype)


def _masked_softmax(s, mask):
    s = jnp.where(mask, s.astype(jnp.float32), NEG_INF)
    e = jnp.where(mask, jnp.exp(s - jnp.max(s, -1, keepdims=True)), 0.0)
    return e / jnp.maximum(jnp.sum(e, -1, keepdims=True), 1e-30)


def _t5_bucket(dist):
    n = jnp.maximum(dist, 0)
    exact = N_BUCKETS // 2
    nf = jnp.maximum(n, 1).astype(jnp.float32)
    large = exact + (jnp.log(nf / exact) / math.log(MAX_DIST / exact) * (N_BUCKETS - exact)).astype(jnp.int32)
    return jnp.where(n < exact, n, jnp.minimum(large, N_BUCKETS - 1))


def _rope(x, pos):
    half = D_ROPE // 2
    inv = ROPE_BASE ** (-jnp.arange(half, dtype=jnp.float32) / half)
    ang = pos.astype(jnp.float32)[:, None] * inv
    ang = ang.reshape(ang.shape[0], *([1] * (x.ndim - 3)), half)
    cos, sin = jnp.cos(ang), jnp.sin(ang)
    x1 = x[..., :half].astype(jnp.float32)
    x2 = x[..., half:].astype(jnp.float32)
    return jnp.concatenate([x1 * cos - x2 * sin, x2 * cos + x1 * sin], -1).astype(x.dtype)


def _sweep(fn, q_pos, *q_arrays):
    sq = q_pos.shape[0]
    if sq > QBLK and sq % QBLK == 0:
        nb = sq // QBLK
        blocks = tuple(jnp.moveaxis(a.reshape(a.shape[0], nb, QBLK, *a.shape[2:]), 1, 0) for a in q_arrays)
        out = lax.map(lambda args: fn(args[0], *args[1]), (q_pos.reshape(nb, QBLK), blocks))
        out = jnp.moveaxis(out, 0, 1)
        return out.reshape(out.shape[0], sq, *out.shape[3:])
    return fn(q_pos, *q_arrays)


def _block_rows(rows):
    B, T = rows.shape[:2]
    n_blk = -(-T // SEL_BLK)
    pad = ((0, 0), (0, n_blk * SEL_BLK - T)) + ((0, 0),) * (rows.ndim - 2)
    return jnp.pad(rows, pad).reshape(B, n_blk, SEL_BLK, *rows.shape[2:])


def _take_blocks(blocks, idx):
    return jax.vmap(lambda blk, i: blk[i])(blocks, idx)


def _local_block_gather(rows):
    blocks = _block_rows(rows)
    return lambda idx: _take_blocks(blocks, idx)


def _paged_block_gather(pool, page_table, new_rows):
    B = new_rows.shape[0]
    bpp = PAGE_SIZE // SEL_BLK
    n_past = page_table.shape[1] * bpp
    pool_blocks = pool.reshape(-1, SEL_BLK, *pool.shape[2:])
    tail = _block_rows(new_rows)
    n_tail = tail.shape[1]

    def gather(idx):
        ip = jnp.clip(idx, 0, n_past - 1)
        page = jnp.take_along_axis(page_table, (ip // bpp).reshape(B, -1), axis=1).reshape(ip.shape)
        from_pool = pool_blocks[page * bpp + ip % bpp]
        from_tail = _take_blocks(tail, jnp.clip(idx - n_past, 0, n_tail - 1))
        return jnp.where((idx < n_past)[..., None, None, None], from_pool, from_tail)
    return gather


def _nsa_compressed(q, qpos, rows, cmp_pos, cmp_w1, cmp_w2, tbl):
    B, T = rows.shape[:2]
    Q = q.shape[1]
    n_cmp = T // CMP_BLK
    blocks = rows[:, :n_cmp * CMP_BLK].reshape(B, n_cmp, CMP_BLK, 2, DH)
    hid = jnp.einsum('bnlcd,cldh->bnch', blocks + jnp.swapaxes(cmp_pos, 0, 1),
                     cmp_w1.reshape(2, CMP_BLK, DH, DH))
    kv_c = jnp.einsum('bnch,che->bnce', jax.nn.silu(hid), cmp_w2)
    ends = (jnp.arange(n_cmp) + 1) * CMP_BLK - 1
    dist = qpos[:, None] - ends[None, :]
    s = jnp.einsum('bqhd,bnd->bhqn', q, kv_c[:, :, 0]).astype(jnp.float32)
    s = s + jnp.moveaxis(tbl[_t5_bucket(dist)], -1, 0)
    p = _masked_softmax(s, dist >= 0)
    o = jnp.einsum('bhqn,bnd->bqhd', p.astype(q.dtype), kv_c[:, :, 1])
    n_sel = -(-T // SEL_BLK)
    imp = jnp.pad(jnp.sum(p, axis=1), ((0, 0), (0, 0), (0, n_sel * CMP_PER_SEL - n_cmp)))
    imp = imp.reshape(B, Q, n_sel, CMP_PER_SEL).sum(-1)
    blk = jnp.arange(n_sel)[None, :]
    cur = (qpos // SEL_BLK)[:, None]
    valid = blk <= cur
    forced = valid & ((blk == 0) | (cur - blk < N_LOCAL))
    score = jnp.where(valid, jnp.where(forced, FORCE, imp), -FORCE)
    _, idx = lax.top_k(score, min(TOP_N, n_sel))
    return o, idx


def _nsa_selected(qpos, q, idx, gather, tbl):
    kv = gather(idx)
    kpos = idx[..., None] * SEL_BLK + jnp.arange(SEL_BLK)
    dist = qpos[None, :, None, None] - kpos
    s = jnp.einsum('bqhd,bqnld->bqhnl', q, kv[..., 0, :]).astype(jnp.float32)
    s = s + jnp.moveaxis(tbl[_t5_bucket(dist)], -1, 2)
    B, Q, H, n, L = s.shape
    p = _masked_softmax(s.reshape(B, Q, H, n * L), (dist >= 0).reshape(B, Q, 1, n * L))
    return jnp.einsum('bqhnl,bqnld->bqhd', p.reshape(B, Q, H, n, L).astype(q.dtype), kv[..., 1, :])


def _window_attend(q, kv, qpos, kpos, tbl):
    dist = qpos[:, None] - kpos[None, :]
    s = jnp.einsum('bqhd,bkd->bhqk', q, kv[:, :, 0]).astype(jnp.float32)
    s = s + jnp.moveaxis(tbl[_t5_bucket(dist)], -1, 0)
    mask = (dist >= 0) & (dist <= WINDOW) & (kpos >= 0)[None, :]
    p = _masked_softmax(s, mask)
    return jnp.einsum('bhqk,bkd->bqhd', p.astype(q.dtype), kv[:, :, 1])


def _window_banded(q, kv, pos, tbl):
    B, S = q.shape[:2]
    qb = QBLK if S % QBLK == 0 else S
    nb = S // qb
    padded = jnp.pad(kv, ((0, 0), (WINDOW, 0), (0, 0), (0, 0)))
    kidx = jnp.arange(nb)[:, None] * qb + jnp.arange(WINDOW + qb)[None, :]
    kv_b = padded[:, kidx]
    kpos_b = pos[0] + kidx - WINDOW
    q_b = q.reshape(B, nb, qb, H_NSA, DH)
    o = jax.vmap(_window_attend, in_axes=(1, 1, 0, 0, None), out_axes=1)(
        q_b, kv_b, pos.reshape(nb, qb), kpos_b, tbl)
    return o.reshape(B, S, H_NSA, DH)


def _mla_attend(qpos, q_lat, q_pe, lat, kpe, kpos):
    s = (jnp.einsum('bqhc,bkc->bhqk', q_lat, lat) + jnp.einsum('bqhr,bkr->bhqk', q_pe, kpe)).astype(jnp.float32)
    p = _masked_softmax(s, kpos[None, :] <= qpos[:, None])
    return jnp.einsum('bhqk,bkc->bqhc', p.astype(lat.dtype), lat)


def _fox_attend(qpos, q, r_q, kv, r_k, kpos):
    B, Q = q.shape[:2]
    T = kv.shape[1]
    qg = q.reshape(B, Q, KV_FOX, G_FOX, DH)
    s = jnp.einsum('bqkgd,btkd->bkgqt', qg, kv[:, :, 0]).astype(jnp.float32)
    rk = r_k.reshape(B, T, KV_FOX, G_FOX).transpose(0, 2, 3, 1)[:, :, :, None, :]
    rq = r_q.reshape(B, Q, KV_FOX, G_FOX).transpose(0, 2, 3, 1)[..., None]
    p = _masked_softmax(s + rk - rq, kpos[None, :] <= qpos[:, None])
    o = jnp.einsum('bkgqt,btkd->bqkgd', p.astype(kv.dtype), kv[:, :, 1])
    return o.reshape(B, Q, H_FOX, DH)


def _diff_attend(qpos, q, kv, kpos, tbl, lam):
    B, Q = q.shape[:2]
    T = kv.shape[1]
    k = kv[:, :, 0].reshape(B, T, KV_DIFF, 2, DD)
    v = kv[:, :, 1]
    qg = q.reshape(B, Q, KV_DIFF, G_DIFF, 2, DD)
    s = jnp.einsum('bqkgid,btkid->ibkgqt', qg, k).astype(jnp.float32)
    dist = qpos[:, None] - kpos[None, :]
    bias = jnp.moveaxis(tbl[_t5_bucket(dist)], -1, 0).reshape(KV_DIFF, G_DIFF, Q, T)
    p = _masked_softmax(s + bias, dist >= 0)
    a = (p[0] - lam * p[1]).astype(v.dtype)
    o = jnp.einsum('bkgqt,btkv->bqkgv', a, v)
    return o.reshape(B, Q, H_DIFF, 2 * DD)


def _gather_pages(pool, page_table):
    g = pool[page_table]
    return g.reshape(g.shape[0], -1, *g.shape[3:])


def _layer(x, pos, lidx, past, rel_bias, w_in, cmp_pos, cmp_w1, cmp_w2, mla_gq, mla_gkv,
           mla_wuq, mla_wuk, mla_wuv, fox_bf, diff_lam, diff_g, w_branch, w_out, ln_g, ln_b):
    B, S, _ = x.shape
    dt = x.dtype
    (a_q, a_cmp, a_sel, a_win, a_gate, a_z,
     b_cq, b_ckv, b_kpe, b_z,
     c_q, c_k, c_v, c_f, c_z,
     d_q, d_k, d_v, d_z, merge_logit) = _split(_pmatmul(x, w_in), IN_SIZES)
    tbl_a, tbl_d = rel_bias[:, :H_NSA], rel_bias[:, H_NSA:]

    def with_past(name, new):
        return new if past is None else jnp.concatenate([past[name], new], axis=1)

    q_a = a_q.reshape(B, S, H_NSA, DH) * DH ** -0.5
    cmp_new = a_cmp.reshape(B, S, 2, DH)
    sel_new = a_sel.reshape(B, S, 2, DH)
    win_new = a_win.reshape(B, S, 2, DH)
    o_cmp, sel_idx = _nsa_compressed(q_a, pos, with_past('nsa_cmp', cmp_new), cmp_pos, cmp_w1, cmp_w2, tbl_a)
    if past is None:
        gather = _local_block_gather(sel_new)
        win_all = win_new
        o_win = _window_banded(q_a, win_new, pos, tbl_a)
    else:
        gather = _paged_block_gather(past['nsa_sel_pool'], past['page_table'], sel_new)
        win_all = jnp.concatenate([past['nsa_win'], win_new], axis=1)
        kpos_w = pos[0] - past['nsa_win'].shape[1] + jnp.arange(win_all.shape[1])
        o_win = _window_attend(q_a, win_all, pos, kpos_w, tbl_a)
    win_state = win_all[:, win_all.shape[1] - min(WINDOW, win_all.shape[1]):]
    o_sel = _sweep(lambda qp, qb, ib: _nsa_selected(qp, qb, ib, gather, tbl_a), pos, q_a, sel_idx)
    g_a = jax.nn.sigmoid(a_gate.reshape(B, S, H_NSA, 3))
    o_a = (g_a[..., 0:1] * o_cmp + g_a[..., 1:2] * o_sel + g_a[..., 2:3] * o_win).reshape(B, S, BR_W)

    cq = _rmsnorm(b_cq, mla_gq)
    qf = (cq @ mla_wuq).reshape(B, S, H_MLA, D_NOPE + D_ROPE)
    q_lat = jnp.einsum('bshn,chn->bshc', qf[..., :D_NOPE], mla_wuk) * MLA_SCALE
    q_pe = _rope(qf[..., D_NOPE:], pos) * MLA_SCALE
    lat_new = _rmsnorm(b_ckv, mla_gkv)
    kpe_new = _rope(b_kpe, pos)
    lat, kpe = with_past('mla_lat', lat_new), with_past('mla_kpe', kpe_new)
    kpos_all = jnp.arange(lat.shape[1])
    o_lat = _sweep(lambda qp, ql, qr: _mla_attend(qp, ql, qr, lat, kpe, kpos_all), pos, q_lat, q_pe)
    o_b = jnp.einsum('bshc,chv->bshv', o_lat, mla_wuv).reshape(B, S, BR_W)

    q_c = c_q.reshape(B, S, H_FOX, DH) * DH ** -0.5
    fkv_new = jnp.stack([c_k.reshape(B, S, KV_FOX, DH), c_v.reshape(B, S, KV_FOX, DH)], axis=2)
    logf_new = jax.nn.log_sigmoid((c_f + fox_bf).astype(jnp.float32))
    fkv = with_past('fox_kv', fkv_new)
    logf = with_past('fox_logf', logf_new).astype(jnp.float32)
    r = lax.cumsum(logf, axis=1, reverse=True) - logf
    o_c = _sweep(lambda qp, qb, rq: _fox_attend(qp, qb, rq, fkv, r, kpos_all),
                 pos, q_c, r[:, r.shape[1] - S:]).reshape(B, S, BR_W)

    q_d = d_q.reshape(B, S, H_DIFF, 2, DD) * DD ** -0.5
    dkv_new = jnp.stack([d_k.reshape(B, S, KV_DIFF, 2 * DD), d_v.reshape(B, S, KV_DIFF, 2 * DD)], axis=2)
    dkv = with_past('diff_kv', dkv_new)
    lam_init = 0.8 - 0.6 * math.exp(-0.3 * lidx)
    lam = (jnp.exp(jnp.sum(diff_lam[0] * diff_lam[1]).astype(jnp.float32))
           - jnp.exp(jnp.sum(diff_lam[2] * diff_lam[3]).astype(jnp.float32)) + lam_init)
    o_d = _sweep(lambda qp, qb: _diff_attend(qp, qb, dkv, kpos_all, tbl_d, lam), pos, q_d)
    o_d = (_rmsnorm(o_d, diff_g, 1e-5) * (1.0 - lam_init)).reshape(B, S, BR_W)

    outs = jnp.stack([o_a * jax.nn.silu(a_z), o_b * jax.nn.silu(b_z),
                      o_c * jax.nn.silu(c_z), o_d * jax.nn.silu(d_z)], axis=2)
    branch = jnp.einsum('bsnc,ncd->bsnd', outs, w_branch)
    gates = jax.nn.sigmoid(merge_logit.reshape(B, S, N_BRANCH, D_MODEL))
    y = _pmatmul(jnp.sum(gates * branch, axis=2), w_out)
    x_new = _layernorm(DEEPNORM_ALPHA * x + y, ln_g, ln_b)
    states = (cmp_new, sel_new, win_state, lat_new, kpe_new, fkv_new, logf_new.astype(dt), dkv_new)
    return x_new, states


TQ = 128
TK = 128
ROW_CHUNK = 256
LANE = 128
BF = jnp.bfloat16
F32 = jnp.float32
VMEM_LIMIT = 56 * 1024 * 1024

PG_AQ, PG_CMP, PG_SEL, PG_WIN, PG_MISC = 0, 512, 640, 768, 896
PG_CQ, PG_CKV, PG_FQ, PG_FKV, PG_DQ, PG_DKV, PG_Z, PG_END = 1024, 1280, 1536, 2048, 2304, 3328, 3584, 4608
MISC_KPE, MISC_GATE, MISC_F = 0, 32, 48
N_SMALL = N_IN - N_BRANCH * D_MODEL


def _in_offsets():
    offs, s = [], 0
    for n in IN_SIZES:
        offs.append(s)
        s += n
    return offs


def _small_proj_columns():
    import numpy as np
    (o_aq, o_cmp, o_sel, o_win, o_gate, o_az, o_cq, o_ckv, o_kpe, o_bz,
     o_fq, o_fk, o_fv, o_ff, o_cz, o_dq, o_dk, o_dv, o_dz, _) = _in_offsets()
    src = np.full((PG_END,), -1, np.int32)
    for h in range(H_NSA):
        src[PG_AQ + LANE * h: PG_AQ + LANE * h + DH] = o_aq + DH * h + np.arange(DH)
    src[PG_CMP:PG_CMP + 2 * DH] = o_cmp + np.arange(2 * DH)
    src[PG_SEL:PG_SEL + 2 * DH] = o_sel + np.arange(2 * DH)
    src[PG_WIN:PG_WIN + 2 * DH] = o_win + np.arange(2 * DH)
    src[PG_MISC + MISC_KPE: PG_MISC + MISC_KPE + D_ROPE] = o_kpe + np.arange(D_ROPE)
    src[PG_MISC + MISC_GATE: PG_MISC + MISC_GATE + 3 * H_NSA] = o_gate + np.arange(3 * H_NSA)
    src[PG_MISC + MISC_F: PG_MISC + MISC_F + H_FOX] = o_ff + np.arange(H_FOX)
    src[PG_CQ:PG_CQ + D_CQ] = o_cq + np.arange(D_CQ)
    src[PG_CKV:PG_CKV + D_C] = o_ckv + np.arange(D_C)
    for h in range(H_FOX):
        kvh = h // G_FOX
        base = PG_FQ + LANE * h + DH * kvh
        src[base: base + DH] = o_fq + DH * h + np.arange(DH)
    src[PG_FKV:PG_FKV + 2 * KV_FOX * DH] = o_fk + np.arange(2 * KV_FOX * DH)
    for kvh in range(KV_DIFF):
        for mp in range(2):
            for g in range(G_DIFF):
                h = kvh * G_DIFF + g
                r = (kvh * 2 + mp) * G_DIFF + g
                base = PG_DQ + LANE * r + 2 * DD * kvh + DD * mp
                src[base: base + DD] = o_dq + 2 * DD * h + DD * mp + np.arange(DD)
    src[PG_DKV:PG_DKV + 2 * KV_DIFF * 2 * DD] = o_dk + np.arange(2 * KV_DIFF * 2 * DD)
    src[PG_Z + 0 * BR_W: PG_Z + 1 * BR_W] = o_az + np.arange(BR_W)
    src[PG_Z + 1 * BR_W: PG_Z + 2 * BR_W] = o_bz + np.arange(BR_W)
    src[PG_Z + 2 * BR_W: PG_Z + 3 * BR_W] = o_cz + np.arange(BR_W)
    src[PG_Z + 3 * BR_W: PG_Z + 4 * BR_W] = o_dz + np.arange(BR_W)
    return src


def _permute_cols(w, src):
    parts, i, n = [], 0, len(src)
    while i < n:
        j = i + 1
        if src[i] < 0:
            while j < n and src[j] < 0:
                j += 1
            parts.append(jnp.zeros(w.shape[:-1] + (j - i,), w.dtype))
        else:
            while j < n and src[j] == src[j - 1] + 1:
                j += 1
            parts.append(w[..., int(src[i]): int(src[i]) + (j - i)])
        i = j
    return jnp.concatenate(parts, -1)


def _uq_columns():
    import numpy as np
    src = np.full((H_MLA * D_NOPE + H_MLA * LANE,), -1, np.int32)
    per = D_NOPE + D_ROPE
    for h in range(H_MLA):
        src[D_NOPE * h: D_NOPE * (h + 1)] = per * h + np.arange(D_NOPE)
        src[H_MLA * D_NOPE + LANE * h: H_MLA * D_NOPE + LANE * h + D_ROPE] = per * h + D_NOPE + np.arange(D_ROPE)
    return src


def _rope_tables(pos):
    half = D_ROPE // 2
    inv = ROPE_BASE ** (-jnp.arange(half, dtype=F32) / half)
    ang = pos.astype(F32)[:, None] * inv
    cos, sin = jnp.cos(ang), jnp.sin(ang)
    n = pos.shape[0]
    one = jnp.ones((n, LANE - D_ROPE), F32)
    zer = jnp.zeros((n, LANE - D_ROPE), F32)
    zh = jnp.zeros((n, half), F32)
    c = jnp.concatenate([cos, cos, one], -1)
    s1 = jnp.concatenate([zh, sin, zer], -1)
    s2 = jnp.concatenate([-sin, zh, zer], -1)
    return c, s1, s2


def _proj_kernel(x_ref, wp_ref, wuq_ref, wuk_ref, gq_ref, gkv_ref, fb_ref, c_ref, s1_ref, s2_ref,
                 qa_ref, cmp_ref, sel_ref, win_ref, selb_ref, winb_ref, misc_ref, lat_ref, kcat_ref,
                 qcat_ref, fq_ref, fkv_ref, fkvb_ref, dq_ref, dkv_ref, dkvb_ref, z_ref):
    xb = x_ref[...].astype(BF)

    def mm(a, b):
        return jnp.dot(xb, wp_ref[:, a:b], preferred_element_type=F32)

    cos, s1, s2 = c_ref[...], s1_ref[...], s2_ref[...]

    def rope(v):
        return v * cos + pltpu.roll(v, D_ROPE // 2, 1) * s1 + pltpu.roll(v, LANE - D_ROPE // 2, 1) * s2

    qa_ref[...] = (mm(PG_AQ, PG_CMP) * DH ** -0.5).astype(BF)
    cmp_ref[...] = mm(PG_CMP, PG_SEL)
    t = mm(PG_SEL, PG_WIN)
    sel_ref[...] = t
    selb_ref[...] = t.astype(BF)
    t = mm(PG_WIN, PG_MISC)
    win_ref[...] = t
    winb_ref[...] = t.astype(BF)

    misc = mm(PG_MISC, PG_CQ)
    lane = lax.broadcasted_iota(jnp.int32, misc.shape, 1)
    roped = rope(misc)
    xf = misc + fb_ref[...]
    logf = jnp.minimum(xf, 0.0) - jnp.log1p(jnp.exp(-jnp.abs(xf)))
    misc_ref[...] = jnp.where((lane >= MISC_F) & (lane < MISC_F + H_FOX), logf, roped)

    cq = mm(PG_CQ, PG_CKV)
    cq = cq * lax.rsqrt(jnp.mean(cq * cq, -1, keepdims=True) + 1e-6) * gq_ref[...]
    qf = jnp.dot(cq.astype(BF), wuq_ref[...], preferred_element_type=F32)
    nn = H_MLA * D_NOPE
    ql = jnp.dot(qf[:, :nn].astype(BF), wuk_ref[...], preferred_element_type=F32) * MLA_SCALE
    wq = D_C + LANE
    for h in range(H_MLA):
        qcat_ref[:, wq * h: wq * h + D_C] = ql[:, D_C * h: D_C * (h + 1)].astype(BF)
        pe = rope(qf[:, nn + LANE * h: nn + LANE * (h + 1)]) * MLA_SCALE
        qcat_ref[:, wq * h + D_C: wq * (h + 1)] = pe.astype(BF)

    ckv = mm(PG_CKV, PG_FQ)
    lat = ckv * lax.rsqrt(jnp.mean(ckv * ckv, -1, keepdims=True) + 1e-6) * gkv_ref[...]
    lat_ref[...] = lat
    kcat_ref[:, :D_C] = lat.astype(BF)
    kcat_ref[:, D_C:] = jnp.where(lane < D_ROPE, roped, 0.0).astype(BF)

    fq_ref[...] = (mm(PG_FQ, PG_FKV) * DH ** -0.5).astype(BF)
    t = mm(PG_FKV, PG_DQ)
    fkv_ref[...] = t
    fkvb_ref[...] = t.astype(BF)
    dq_ref[...] = (mm(PG_DQ, PG_DKV) * DD ** -0.5).astype(BF)
    t = mm(PG_DKV, PG_Z)
    dkv_ref[...] = t
    dkvb_ref[...] = t.astype(BF)
    z_ref[...] = mm(PG_Z, PG_END)


def _proj(x2, wp, wuq, wukbd, gq, gkv, fb, tabs, tm):
    M = x2.shape[0]
    npos = tabs[0].shape[0] // tm
    row = lambda i: (i, 0)
    cst = lambda i: (0, 0)
    tab = lambda i: (i % npos, 0)
    widths = [(512, BF), (128, F32), (128, F32), (128, F32), (128, BF), (128, BF), (128, F32), (D_C, F32),
              (D_C + LANE, BF), (H_MLA * (D_C + LANE), BF), (512, BF), (256, F32), (256, BF), (1024, BF),
              (256, F32), (256, BF), (1024, F32)]
    return pl.pallas_call(
        _proj_kernel,
        grid=(M // tm,),
        in_specs=[pl.BlockSpec((tm, D_MODEL), row),
                  pl.BlockSpec(wp.shape, cst), pl.BlockSpec(wuq.shape, cst), pl.BlockSpec(wukbd.shape, cst),
                  pl.BlockSpec((1, D_CQ), cst), pl.BlockSpec((1, D_C), cst), pl.BlockSpec((1, LANE), cst),
                  pl.BlockSpec((tm, LANE), tab), pl.BlockSpec((tm, LANE), tab), pl.BlockSpec((tm, LANE), tab)],
        out_specs=[pl.BlockSpec((tm, w), row) for w, _ in widths],
        out_shape=[jax.ShapeDtypeStruct((M, w), d) for w, d in widths],
        compiler_params=pltpu.CompilerParams(dimension_semantics=("arbitrary",), vmem_limit_bytes=VMEM_LIMIT),
        name="proj",
    )(x2, wp, wuq, wukbd, gq, gkv, fb, *tabs)


def _cmp_summary_kernel(x_ref, pos_ref, wc_ref, w2_ref, o_ref):
    xb = (x_ref[...] + pos_ref[...]).astype(BF)
    hid = jnp.dot(xb, wc_ref[...], preferred_element_type=F32)
    act = hid * jax.nn.sigmoid(hid)
    o_ref[...] = jnp.dot(act.astype(BF), w2_ref[...], preferred_element_type=F32)


def _cmp_summary(x3, layer, pos_row, wc, w2, tb):
    _, N, W = x3.shape
    return pl.pallas_call(
        _cmp_summary_kernel,
        grid=(N // tb,),
        in_specs=[pl.BlockSpec((None, tb, W), lambda i: (layer, i, 0)),
                  pl.BlockSpec((1, W), lambda i: (0, 0)),
                  pl.BlockSpec(wc.shape, lambda i: (0, 0)),
                  pl.BlockSpec(w2.shape, lambda i: (0, 0))],
        out_specs=pl.BlockSpec((tb, LANE), lambda i: (i, 0)),
        out_shape=jax.ShapeDtypeStruct((N, LANE), F32),
        compiler_params=pltpu.CompilerParams(dimension_semantics=("arbitrary",), vmem_limit_bytes=VMEM_LIMIT),
        name="cmp_summary",
    )(x3, pos_row, wc, w2)


CMP_PAGES = 64


def _cmp_pool_kernel(xt_ref, pos_ref, wc_ref, w2_ref, o_ref, xs):
    n_pages = xt_ref.shape[0]
    per = PAGE_SIZE // CMP_BLK
    for p in range(n_pages):
        xs[PAGE_SIZE * p: PAGE_SIZE * (p + 1), :] = xt_ref[p].T
    w = 2 * DH
    hid = jnp.zeros((n_pages * per, w), F32)
    for i in range(CMP_BLK):
        rows = xs[pl.ds(i, n_pages * per, stride=CMP_BLK), :] + pos_ref[:, w * i: w * (i + 1)]
        hid = hid + jnp.dot(rows.astype(BF), wc_ref[w * i: w * (i + 1), :], preferred_element_type=F32)
    act = hid * jax.nn.sigmoid(hid)
    o_ref[...] = jnp.dot(act.astype(BF), w2_ref[...], preferred_element_type=F32)


def _cmp_pool_summary(xt, layer, pos_row, wc, w2, pp):
    n_pool = xt.shape[1]
    per = PAGE_SIZE // CMP_BLK
    return pl.pallas_call(
        _cmp_pool_kernel,
        grid=(n_pool // pp,),
        in_specs=[pl.BlockSpec((None, pp, 2 * DH, PAGE_SIZE), lambda i: (layer, i, 0, 0)),
                  pl.BlockSpec(pos_row.shape, lambda i: (0, 0)),
                  pl.BlockSpec(wc.shape, lambda i: (0, 0)),
                  pl.BlockSpec(w2.shape, lambda i: (0, 0))],
        out_specs=pl.BlockSpec((pp * per, LANE), lambda i: (i, 0)),
        out_shape=jax.ShapeDtypeStruct((n_pool * per, LANE), F32),
        scratch_shapes=[pltpu.VMEM((pp * PAGE_SIZE, 2 * DH), F32)],
        compiler_params=pltpu.CompilerParams(dimension_semantics=("arbitrary",), vmem_limit_bytes=VMEM_LIMIT),
        name="cmp_pool_summary",
    )(xt, pos_row, wc, w2)


def _split3(x):
    hi = x.astype(BF)
    r = x - hi.astype(F32)
    mid = r.astype(BF)
    lo = (r - mid.astype(F32)).astype(BF)
    return hi, mid, lo


def _dot3(x, u):
    hi, mid, lo = _split3(x)
    d = lambda a: jnp.dot(a, u, preferred_element_type=F32)
    return d(hi) + d(mid) + d(lo)


def _prefix_kernel(x_ref, u_ref, o_ref):
    u = u_ref[...]
    carry = jnp.zeros((x_ref.shape[0], 1), F32)
    for c in range(x_ref.shape[1] // LANE):
        p = _dot3(x_ref[:, LANE * c: LANE * (c + 1)], u)
        o_ref[:, LANE * c: LANE * (c + 1)] = p + carry
        carry = carry + p[:, LANE - 1: LANE]


def _prefix(x, u):
    return pl.pallas_call(_prefix_kernel, out_shape=jax.ShapeDtypeStruct(x.shape, F32), name="fox_prefix")(x, u)


def _online(s, v, m_ref, l_ref, acc_ref):
    m_prev = m_ref[...]
    m_new = jnp.maximum(m_prev, jnp.max(s, -1, keepdims=True))
    alpha = jnp.exp(m_prev - m_new)
    p = jnp.exp(s - m_new)
    l_ref[...] = alpha * l_ref[...] + jnp.sum(p, -1, keepdims=True)
    acc_ref[...] = alpha * acc_ref[...] + jnp.dot(p.astype(BF), v, preferred_element_type=F32)
    m_ref[...] = m_new


def _init_state(m_ref, l_ref, acc_ref):
    m_ref[...] = jnp.full(m_ref.shape, NEG_INF, F32)
    l_ref[...] = jnp.zeros(l_ref.shape, F32)
    acc_ref[...] = jnp.zeros(acc_ref.shape, F32)


def _qk(q, k):
    return lax.dot_general(q, k, (((1,), (1,)), ((), ())), preferred_element_type=F32)


def _key_tiles(S):
    nkb = S // TK
    return 4 if nkb % 4 == 0 else (2 if nkb % 2 == 0 else 1)


def _causal_sweep(qb, W, rows, tile, m_ref, l_ref, acc_ref):
    nfull = lax.shift_right_logical(qb, int(math.log2(W)))
    chunks = [slice(r0, r0 + ROW_CHUNK) for r0 in range(0, rows, ROW_CHUNK)]

    def body(j, c):
        for rs in chunks:
            s, v = tile(j, rs)
            _online(s, v, m_ref.at[rs], l_ref.at[rs], acc_ref.at[rs])
        return c

    lax.fori_loop(0, nfull, body, 0)
    r = lax.broadcasted_iota(jnp.int32, (ROW_CHUNK, W * TK), 0) & (TQ - 1)
    c = lax.broadcasted_iota(jnp.int32, (ROW_CHUNK, W * TK), 1)
    ok = c <= r + TQ * (qb - W * nfull)
    for rs in chunks:
        s, v = tile(nfull, rs)
        _online(jnp.where(ok, s, NEG_INF), v, m_ref.at[rs], l_ref.at[rs], acc_ref.at[rs])


def _bias_tiles(tb_ref, heads, qb, W, j):
    cols = []
    for u in range(W):
        d = jnp.clip(qb - (W * j + u), 0, 2)
        cols.append(jnp.concatenate([tb_ref[h, d] for h in heads], 0))
    return cols[0] if W == 1 else jnp.concatenate(cols, 1)


def _mla_kernel(W, q_ref, k_ref, o_ref, qs, m_ref, l_ref, acc_ref):
    qb = pl.program_id(1)
    wq = D_C + LANE
    kw = W * TK
    for h in range(H_MLA):
        qs[TQ * h: TQ * (h + 1), :] = q_ref[:, wq * h: wq * (h + 1)]
    _init_state(m_ref, l_ref, acc_ref)

    def tile(j, rs):
        k = k_ref[pl.ds(pl.multiple_of(j * kw, kw), kw), :]
        return _qk(qs[rs, :], k), k[:, :D_C]

    _causal_sweep(qb, W, H_MLA * TQ, tile, m_ref, l_ref, acc_ref)
    for h in range(H_MLA):
        blk = slice(TQ * h, TQ * (h + 1))
        o_ref[:, D_C * h: D_C * (h + 1)] = acc_ref[blk, :] / l_ref[blk, :]


def _fox_kernel(W, q_ref, kv_ref, pq_ref, pk_ref, o_ref, qs, m_ref, l_ref, acc_ref):
    qb = pl.program_id(1)
    kw = W * TK
    for h in range(H_FOX):
        qs[TQ * h: TQ * (h + 1), :] = q_ref[:, LANE * h: LANE * (h + 1)]
    _init_state(m_ref, l_ref, acc_ref)
    pq = pq_ref[...]

    def tile(j, rs):
        kv = kv_ref[pl.ds(pl.multiple_of(j * kw, kw), kw), :]
        pk = pk_ref[j]
        decay = jnp.concatenate([pq[:, h: h + 1] - pk[h: h + 1, :]
                                 for h in range(rs.start // TQ, rs.stop // TQ)], 0)
        return _qk(qs[rs, :], kv[:, :LANE]) + decay, kv[:, LANE:]

    _causal_sweep(qb, W, H_FOX * TQ, tile, m_ref, l_ref, acc_ref)
    for h in range(H_FOX):
        blk = slice(TQ * h, TQ * (h + 1))
        kvh = h // G_FOX
        o_ref[:, DH * h: DH * (h + 1)] = acc_ref[blk, DH * kvh: DH * (kvh + 1)] / l_ref[blk, :]


def _diff_rows():
    out = []
    for kvh in range(KV_DIFF):
        for mp in range(2):
            for g in range(G_DIFF):
                out.append((kvh * G_DIFF + g, mp))
    return out


def _diff_kernel(W, q_ref, kv_ref, tb_ref, dl_ref, lam0_ref, o_ref, qs, m_ref, l_ref, acc_ref):
    qb = pl.program_id(1)
    kw = W * TK
    rows = _diff_rows()
    for r in range(len(rows)):
        qs[TQ * r: TQ * (r + 1), :] = q_ref[:, LANE * r: LANE * (r + 1)]
    _init_state(m_ref, l_ref, acc_ref)

    def tile(j, rs):
        kv = kv_ref[pl.ds(pl.multiple_of(j * kw, kw), kw), :]
        heads = [h for h, _ in rows[rs.start // TQ: rs.stop // TQ]]
        s = _qk(qs[rs, :], kv[:, :LANE]) + _bias_tiles(tb_ref, heads, qb, W, j)
        return s, kv[:, LANE:]

    _causal_sweep(qb, W, len(rows) * TQ, tile, m_ref, l_ref, acc_ref)

    dl = dl_ref[...]
    lam = (jnp.exp(jnp.sum(dl[0:1] * dl[1:2], -1, keepdims=True))
           - jnp.exp(jnp.sum(dl[2:3] * dl[3:4], -1, keepdims=True)) + lam0_ref[...])
    for kvh in range(KV_DIFF):
        for g in range(G_DIFF):
            h = kvh * G_DIFF + g
            b0 = slice(TQ * rows.index((h, 0)), TQ * (rows.index((h, 0)) + 1))
            b1 = slice(TQ * rows.index((h, 1)), TQ * (rows.index((h, 1)) + 1))
            vs = slice(2 * DD * kvh, 2 * DD * (kvh + 1))
            o0 = acc_ref[b0, vs] / l_ref[b0, :]
            o1 = acc_ref[b1, vs] / l_ref[b1, :]
            o_ref[:, 2 * DD * h: 2 * DD * (h + 1)] = o0 - lam * o1


def _nsa_kernel(W, q_ref, kvc_ref, sel_ref, win_ref, cb_ref, tb_ref, o_ref, qs, m_ref, l_ref, acc_ref, smask):
    qb = pl.program_id(1)
    R = H_NSA * TQ
    kw = W * TK
    nkb = smask.shape[0] * W
    n_sel = nkb * (TK // SEL_BLK)
    n_pair = kvc_ref.shape[1]
    qs[...] = jnp.zeros(qs.shape, BF)
    for h in range(H_NSA):
        qs[TQ * h: TQ * (h + 1), :] = q_ref[:, LANE * h: LANE * (h + 1)]
    q = qs[...]

    ke = kvc_ref[0].astype(BF)
    ko = kvc_ref[1].astype(BF)
    ri = lax.broadcasted_iota(jnp.int32, (R, n_pair), 0) & (TQ - 1)
    li = lax.broadcasted_iota(jnp.int32, (R, n_pair), 1)
    qpos = qb * TQ + ri
    me = qpos >= SEL_BLK * li + (CMP_BLK - 1)
    mo = qpos >= SEL_BLK * li + (SEL_BLK - 1)
    se = jnp.where(me, _qk(q, ke) + cb_ref[0], NEG_INF)
    so = jnp.where(mo, _qk(q, ko) + cb_ref[1], NEG_INF)
    mx = jnp.maximum(jnp.max(se, -1, keepdims=True), jnp.max(so, -1, keepdims=True))
    ee = jnp.where(me, jnp.exp(se - mx), 0.0)
    eo = jnp.where(mo, jnp.exp(so - mx), 0.0)
    den = jnp.maximum(jnp.sum(ee, -1, keepdims=True) + jnp.sum(eo, -1, keepdims=True), 1e-30)
    pe = ee / den
    po = eo / den
    ocmp = (jnp.dot(pe.astype(BF), ke, preferred_element_type=F32)
            + jnp.dot(po.astype(BF), ko, preferred_element_type=F32))
    for h in range(H_NSA):
        o_ref[:, DH * h: DH * (h + 1)] = ocmp[TQ * h: TQ * (h + 1), DH:]
    pp = pe + po
    imp = pp[0:TQ]
    for h in range(1, H_NSA):
        imp = imp + pp[TQ * h: TQ * (h + 1)]

    bi = lax.broadcasted_iota(jnp.int32, (TQ, n_pair), 1)
    qi = qb * TQ + lax.broadcasted_iota(jnp.int32, (TQ, n_pair), 0)
    cur = lax.shift_right_logical(qi, int(math.log2(SEL_BLK)))
    valid = bi <= cur
    forced = valid & ((bi == 0) | (cur - bi < N_LOCAL))
    score = jnp.where(valid, jnp.where(forced, FORCE, imp), -FORCE)
    rank = jnp.zeros((TQ, n_pair), F32)
    for i in range(n_sel):
        col = score[:, i: i + 1]
        ahead = (col > score) | ((col == score) & (bi > i))
        rank = rank + jnp.where(ahead, 1.0, 0.0)
    madd = jnp.where(rank < TOP_N, 0.0, NEG_INF)
    lane = lax.broadcasted_iota(jnp.int32, (TQ, TK), 1)
    for j in range(nkb // W):
        cols = [jnp.where(lane < SEL_BLK, madd[:, 2 * kb: 2 * kb + 1], madd[:, 2 * kb + 1: 2 * kb + 2])
                for kb in range(W * j, W * (j + 1))]
        smask[j] = cols[0] if W == 1 else jnp.concatenate(cols, 1)

    _init_state(m_ref, l_ref, acc_ref)
    heads = list(range(H_NSA))

    def sel_tile(j, rs):
        kv = sel_ref[pl.ds(pl.multiple_of(j * kw, kw), kw), :]
        hs = heads[rs.start // TQ: rs.stop // TQ]
        s = _qk(qs[rs, :], kv) + _bias_tiles(tb_ref, hs, qb, W, j) + jnp.concatenate([smask[j]] * len(hs), 0)
        return s, kv

    _causal_sweep(qb, W, R, sel_tile, m_ref, l_ref, acc_ref)
    for h in range(H_NSA):
        blk = slice(TQ * h, TQ * (h + 1))
        o_ref[:, BR_W + DH * h: BR_W + DH * (h + 1)] = acc_ref[blk, DH:] / l_ref[blk, :]

    _init_state(m_ref, l_ref, acc_ref)
    rr = lax.broadcasted_iota(jnp.int32, (R, TK), 0) & (TQ - 1)
    cc = lax.broadcasted_iota(jnp.int32, (R, TK), 1)
    back = WINDOW // TK
    tiles = []
    for u in range(back + 1):
        kb = qb - (back - u)
        kv = win_ref[pl.ds(pl.multiple_of(jnp.maximum(kb, 0) * TK, TK), TK), :]
        dist = (back - u) * TQ + rr - cc
        ok = (dist >= 0) & (dist <= WINDOW) & (kb >= 0)
        bias = jnp.concatenate([tb_ref[h, min(back - u, 2)] for h in heads], 0)
        tiles.append((jnp.where(ok, _qk(q, kv) + bias, NEG_INF), kv, False))
    _online_multi(tiles, m_ref, l_ref, acc_ref)
    for h in range(H_NSA):
        blk = slice(TQ * h, TQ * (h + 1))
        o_ref[:, 2 * BR_W + DH * h: 2 * BR_W + DH * (h + 1)] = acc_ref[blk, DH:] / l_ref[blk, :]


def _attn_call(kern, B, S, ins, specs, out_w, scratch, name):
    nq = S // TQ
    return pl.pallas_call(
        kern,
        grid=(B, nq),
        in_specs=specs,
        out_specs=pl.BlockSpec((TQ, out_w), lambda b, i: (b * nq + i, 0)),
        out_shape=jax.ShapeDtypeStruct((B * S, out_w), F32),
        scratch_shapes=scratch,
        compiler_params=pltpu.CompilerParams(dimension_semantics=("arbitrary", "arbitrary"),
                                             vmem_limit_bytes=VMEM_LIMIT),
        name=name,
    )(*ins)


def _state_scratch(rows, qw, dv):
    return [pltpu.VMEM((rows, qw), BF), pltpu.VMEM((rows, 1), F32), pltpu.VMEM((rows, 1), F32),
            pltpu.VMEM((rows, dv), F32)]


def _post_kernel(x_ref, olat_ref, onsa_ref, ofox_ref, odiff_ref, misc_ref, z_ref, wuv_ref, dg_ref, ds_ref,
                 wbr_ref, wm_ref, wo_ref, lg_ref, lb_ref, o_ref):
    x = x_ref[...]
    xb = x.astype(BF)
    gate = jax.nn.sigmoid(misc_ref[:, MISC_GATE: MISC_GATE + 3 * H_NSA])
    onsa = onsa_ref[...]
    pieces = []
    for h in range(H_NSA):
        hs = slice(DH * h, DH * (h + 1))
        pieces.append(gate[:, 3 * h: 3 * h + 1] * onsa[:, hs]
                      + gate[:, 3 * h + 1: 3 * h + 2] * onsa[:, BR_W + DH * h: BR_W + DH * (h + 1)]
                      + gate[:, 3 * h + 2: 3 * h + 3] * onsa[:, 2 * BR_W + DH * h: 2 * BR_W + DH * (h + 1)])
    oa = jnp.concatenate(pieces, -1)
    ob = jnp.dot(olat_ref[...].astype(BF), wuv_ref[...], preferred_element_type=F32)
    oc = ofox_ref[...]
    od_raw = odiff_ref[...]
    pieces = []
    for h in range(H_DIFF):
        v = od_raw[:, 2 * DD * h: 2 * DD * (h + 1)]
        pieces.append(v * lax.rsqrt(jnp.mean(v * v, -1, keepdims=True) + 1e-5))
    od = jnp.concatenate(pieces, -1) * dg_ref[...] * ds_ref[...]
    z = z_ref[...]
    zs = z * jax.nn.sigmoid(z)
    acc = jnp.zeros(x.shape, F32)
    for n, o in enumerate((oa, ob, oc, od)):
        on = (o * zs[:, BR_W * n: BR_W * (n + 1)]).astype(BF)
        br = jnp.dot(on, wbr_ref[n], preferred_element_type=F32)
        g = jax.nn.sigmoid(jnp.dot(xb, wm_ref[:, D_MODEL * n: D_MODEL * (n + 1)], preferred_element_type=F32))
        acc = acc + g * br
    y = jnp.dot(acc.astype(BF), wo_ref[...], preferred_element_type=F32)
    t = DEEPNORM_ALPHA * x + y
    mu = jnp.mean(t, -1, keepdims=True)
    var = jnp.mean(jnp.square(t - mu), -1, keepdims=True)
    o_ref[...] = (t - mu) * lax.rsqrt(var + 1e-5) * lg_ref[...] + lb_ref[...]


def _post(x2, olat, onsa, ofox, odiff, misc, z, wuv, dg, ds, wbr, wm, wo, lg, lb, tm):
    M = x2.shape[0]
    row = lambda i: (i, 0)
    cst2 = lambda i: (0, 0)
    cst3 = lambda i: (0, 0, 0)
    acts = (x2, olat, onsa, ofox, odiff, misc, z)
    return pl.pallas_call(
        _post_kernel,
        grid=(M // tm,),
        in_specs=[pl.BlockSpec((tm, a.shape[1]), row) for a in acts]
        + [pl.BlockSpec(wuv.shape, cst2), pl.BlockSpec(dg.shape, cst2), pl.BlockSpec(ds.shape, cst2),
           pl.BlockSpec(wbr.shape, cst3), pl.BlockSpec(wm.shape, cst2), pl.BlockSpec(wo.shape, cst2),
           pl.BlockSpec(lg.shape, cst2), pl.BlockSpec(lb.shape, cst2)],
        out_specs=pl.BlockSpec((tm, D_MODEL), row),
        out_shape=jax.ShapeDtypeStruct((M, D_MODEL), F32),
        compiler_params=pltpu.CompilerParams(dimension_semantics=("arbitrary",), vmem_limit_bytes=VMEM_LIMIT),
        name="post_merge",
    )(*acts, wuv, dg, ds, wbr, wm, wo, lg, lb)


def _layer_weights(l, w_in, cmp_pos, cmp_w1, cmp_w2, mla_gq, mla_gkv, mla_wuq, mla_wuk, mla_wuv,
                   fox_bf, diff_lam, diff_g, w_branch, w_out, ln_g, ln_b):
    import numpy as np
    wp = _permute_cols(w_in[l, :, :N_SMALL], _small_proj_columns()).astype(BF)
    wm = w_in[l, :, N_SMALL:].astype(BF)
    wuq = _permute_cols(mla_wuq[l], _uq_columns()).astype(BF)
    eye = jnp.eye(H_MLA, dtype=F32)
    wukbd = jnp.einsum('chn,hg->hngc', mla_wuk[l], eye).reshape(H_MLA * D_NOPE, H_MLA * D_C).astype(BF)
    wuvbd = jnp.einsum('chv,hg->hcgv', mla_wuv[l], eye).reshape(H_MLA * D_C, H_MLA * D_V).astype(BF)
    fb = jnp.zeros((1, LANE), F32).at[0, MISC_F: MISC_F + H_FOX].set(fox_bf[l])
    w1 = cmp_w1[l].reshape(2, CMP_BLK, DH, DH)
    wc = jnp.einsum('cldh,ce->lcdeh', w1, jnp.eye(2, dtype=F32)).reshape(CMP_BLK * 2 * DH, 2 * DH).astype(BF)
    w2 = jnp.einsum('che,cf->chfe', cmp_w2[l], jnp.eye(2, dtype=F32)).reshape(2 * DH, 2 * DH).astype(BF)
    cpos = jnp.swapaxes(cmp_pos[l], 0, 1).reshape(1, CMP_BLK * 2 * DH)
    lam_init = 0.8 - 0.6 * math.exp(-0.3 * l)
    return dict(
        wp=wp, wm=wm, wuq=wuq, wukbd=wukbd, wuvbd=wuvbd, fb=fb, wc=wc, w2=w2, cpos=cpos,
        gq=mla_gq[l][None], gkv=mla_gkv[l][None], dl=diff_lam[l],
        lam0=jnp.full((1, 1), lam_init, F32), ds=jnp.full((1, 1), 1.0 - lam_init, F32),
        dg=jnp.tile(diff_g[l], H_DIFF)[None], wbr=w_branch[l].astype(BF), wo=w_out[l].astype(BF),
        lg=ln_g[l][None], lb=ln_b[l][None])


def _prompt_bias_tables(rel_bias, S):
    nq = S // TQ
    i = jnp.arange(TQ)[:, None]
    j = jnp.arange(TK)[None, :]
    tiles = jnp.stack([rel_bias[_t5_bucket(d * TQ + i - j)] for d in range(3)])
    tb = jnp.moveaxis(tiles, -1, 0)
    n_cmp = S // CMP_BLK
    ends = (jnp.arange(n_cmp) + 1) * CMP_BLK - 1
    dist = jnp.arange(S)[:, None] - ends[None, :]
    cb = rel_bias[:, :H_NSA][_t5_bucket(dist)]
    cb = cb.reshape(nq, TQ, n_cmp // 2, 2, H_NSA)
    cb = jnp.transpose(cb, (0, 3, 4, 1, 2)).reshape(nq, 2, H_NSA * TQ, n_cmp // 2)
    return tb[:H_NSA], tb[H_NSA:], cb


def _prompt_layer(x, lw, tabs, tb_a, tb_d, cb, u_tri):
    B, S, _ = x.shape
    M = B * S
    nq = S // TQ
    W = _key_tiles(S)
    kw = W * TK
    x2 = x.reshape(M, D_MODEL)
    (qa, cmp_s, sel_s, win_s, selb, winb, misc, lat, kcat, qcat, fq, fkv, fkvb, dq, dkv, dkvb, z) = _proj(
        x2, lw['wp'], lw['wuq'], lw['wukbd'], lw['gq'], lw['gkv'], lw['fb'], tabs, 256)

    n_cmp = S // CMP_BLK
    kvc = _cmp_summary(cmp_s.reshape(1, M // CMP_BLK, CMP_BLK * 2 * DH), 0, lw['cpos'], lw['wc'], lw['w2'],
                       min(256, M // CMP_BLK))
    kvc = jnp.swapaxes(kvc.reshape(B, n_cmp // 2, 2, LANE), 1, 2)

    qrow = lambda w: pl.BlockSpec((TQ, w), lambda b, i: (b * nq + i, 0))
    kvrow = lambda w: pl.BlockSpec((S, w), lambda b, i: (b, 0))
    whole = lambda a: pl.BlockSpec(a.shape, lambda b, i: (0,) * a.ndim)

    onsa = _attn_call(
        functools.partial(_nsa_kernel, W), B, S, (qa, kvc, selb, winb, cb, tb_a),
        [qrow(512), pl.BlockSpec((None, 2, n_cmp // 2, LANE), lambda b, i: (b, 0, 0, 0)), kvrow(LANE), kvrow(LANE),
         pl.BlockSpec((None, 2, H_NSA * TQ, n_cmp // 2), lambda b, i: (i, 0, 0, 0)), whole(tb_a)],
        3 * BR_W, _state_scratch(H_NSA * TQ, LANE, LANE) + [pltpu.VMEM((S // kw, TQ, kw), F32)], "nsa_prompt")

    olat = _attn_call(functools.partial(_mla_kernel, W), B, S, (qcat, kcat), [qrow(H_MLA * (D_C + LANE)), kvrow(D_C + LANE)],
                      H_MLA * D_C, _state_scratch(H_MLA * TQ, D_C + LANE, D_C), "mla_prompt")

    logf = misc[:, MISC_F: MISC_F + H_FOX].reshape(B, S, H_FOX)
    pre = _prefix(jnp.swapaxes(logf, 1, 2).reshape(B * H_FOX, S), u_tri).reshape(B, H_FOX, S)
    pq = jnp.swapaxes(pre, 1, 2).reshape(M, H_FOX)
    pk = jnp.swapaxes(pre.reshape(B, H_FOX, S // kw, kw), 1, 2)
    pk = jnp.pad(pk, ((0, 0), (0, 0), (0, 8 - H_FOX), (0, 0)))
    ofox = _attn_call(
        functools.partial(_fox_kernel, W), B, S, (fq, fkvb, pq, pk),
        [qrow(512), kvrow(256), qrow(H_FOX), pl.BlockSpec((None, S // kw, 8, kw), lambda b, i: (b, 0, 0, 0))],
        BR_W, _state_scratch(H_FOX * TQ, LANE, LANE), "fox_prompt")

    odiff = _attn_call(
        functools.partial(_diff_kernel, W), B, S, (dq, dkvb, tb_d, lw['dl'], lw['lam0']),
        [qrow(1024), kvrow(256), whole(tb_d), whole(lw['dl']), whole(lw['lam0'])],
        BR_W, _state_scratch(2 * H_DIFF * TQ, LANE, LANE), "diff_prompt")

    x_new = _post(x2, olat, onsa, ofox, odiff, misc, z, lw['wuvbd'], lw['dg'], lw['ds'], lw['wbr'], lw['wm'],
                  lw['wo'], lw['lg'], lw['lb'], 256).reshape(B, S, D_MODEL)
    wlen = min(WINDOW, S)
    states = (cmp_s.reshape(B, S, 2, DH), sel_s.reshape(B, S, 2, DH),
              win_s.reshape(B, S, 2, DH)[:, S - wlen:],
              lat.reshape(B, S, D_C), misc[:, MISC_KPE: MISC_KPE + D_ROPE].reshape(B, S, D_ROPE),
              fkv.reshape(B, S, 2, KV_FOX, DH), logf, dkv.reshape(B, S, 2, KV_DIFF, 2 * DD))
    return x_new, states


PG = 8
GATHER_G = 16


def _gather_rows_kernel(pt_ref, *refs):
    o_ref = refs[-1]
    for g, r in enumerate(refs[:-1]):
        o_ref[g: g + 1, :] = r[...]


def _gather_rows(pool3, page_table, gg):
    DB, NP = page_table.shape
    W = pool3.shape[-1]
    return pl.pallas_call(
        _gather_rows_kernel,
        grid_spec=pltpu.PrefetchScalarGridSpec(
            num_scalar_prefetch=1, grid=(DB, NP // gg),
            in_specs=[pl.BlockSpec((None, 1, W), lambda b, t, pt, g=g: (pt[b, t * gg + g], 0, 0))
                      for g in range(gg)],
            out_specs=pl.BlockSpec((None, gg, W), lambda b, t, pt: (b, t, 0))),
        out_shape=jax.ShapeDtypeStruct((DB, NP, W), F32),
        compiler_params=pltpu.CompilerParams(dimension_semantics=("arbitrary", "arbitrary")),
        name="gather_summaries",
    )(page_table, *([pool3] * gg))


def _suffix_kernel(x_ref, ugt_ref, one_ref, sfx_ref, tot_ref):
    x = x_ref[...]
    sfx_ref[...] = _dot3(x, ugt_ref[...])
    tot_ref[...] = _dot3(x, one_ref[...])


def _page_suffix(x, tr):
    R = x.shape[0]
    ugt = (jnp.arange(LANE)[:, None] > jnp.arange(LANE)[None, :]).astype(BF)
    one = jnp.ones((LANE, LANE), BF)
    return pl.pallas_call(
        _suffix_kernel,
        grid=(R // tr,),
        in_specs=[pl.BlockSpec((tr, LANE), lambda i: (i, 0)), pl.BlockSpec((LANE, LANE), lambda i: (0, 0)),
                  pl.BlockSpec((LANE, LANE), lambda i: (0, 0))],
        out_specs=[pl.BlockSpec((tr, LANE), lambda i: (i, 0))] * 2,
        out_shape=[jax.ShapeDtypeStruct((R, LANE), F32)] * 2,
        compiler_params=pltpu.CompilerParams(dimension_semantics=("arbitrary",)),
        name="fox_page_suffix",
    )(x, ugt, one)


def _online_multi(tiles, m_ref, l_ref, acc_ref):
    m_prev = m_ref[...]
    m_new = m_prev
    for s, _, _ in tiles:
        m_new = jnp.maximum(m_new, jnp.max(s, -1, keepdims=True))
    alpha = jnp.exp(m_prev - m_new)
    l_new = alpha * l_ref[...]
    acc = alpha * acc_ref[...]
    for s, v, v_t in tiles:
        p = jnp.exp(s - m_new)
        l_new = l_new + jnp.sum(p, -1, keepdims=True)
        pb = p.astype(BF)
        acc = acc + (_qk(pb, v) if v_t else jnp.dot(pb, v, preferred_element_type=F32))
    l_ref[...] = l_new
    acc_ref[...] = acc
    m_ref[...] = m_new


def _nsa_sample_kernel(dq, n_win, q_ref, kvc_ref, cb_ref, win_ref, wt_ref, wb1_ref, wb2_ref, o_ref, sm_ref):
    q = q_ref[...]
    rows = q.shape[0]
    n_pair = kvc_ref.shape[1]
    ke = kvc_ref[0].astype(BF)
    ko = kvc_ref[1].astype(BF)
    se = _qk(q, ke) + cb_ref[0]
    so = _qk(q, ko) + cb_ref[1]
    mx = jnp.maximum(jnp.max(se, -1, keepdims=True), jnp.max(so, -1, keepdims=True))
    ee = jnp.exp(se - mx)
    eo = jnp.exp(so - mx)
    den = jnp.maximum(jnp.sum(ee, -1, keepdims=True) + jnp.sum(eo, -1, keepdims=True), 1e-30)
    pe = ee / den
    po = eo / den
    o_ref[:, :LANE] = (jnp.dot(pe.astype(BF), ke, preferred_element_type=F32)
                       + jnp.dot(po.astype(BF), ko, preferred_element_type=F32))
    pp = pe + po
    a = pp[0:8] + pp[8:16]
    imp = a + pltpu.roll(a, dq, 0)
    bi = lax.broadcasted_iota(jnp.int32, (8, n_pair), 1)
    forced = (bi == 0) | (n_pair - bi < N_LOCAL)
    score = jnp.where(forced, FORCE, imp)
    rank = jnp.where(forced, 0.0, 1.0)
    for i in range(n_pair):
        col = score[:, i: i + 1]
        ahead = (col > score) | ((col == score) & (bi > i))
        rank = rank + jnp.where(ahead, 1.0, 0.0)
    madd = jnp.where(rank < TOP_N, 0.0, NEG_INF)
    sm_ref[...] = jnp.concatenate([madd] * (rows // 8), 0)

    wp = win_ref[...].astype(BF)
    wt = wt_ref[...]
    qi1 = lax.broadcasted_iota(jnp.int32, (rows, n_win), 0) & (dq - 1)
    t1 = lax.broadcasted_iota(jnp.int32, (rows, n_win), 1)
    ok1 = n_win + qi1 - t1 <= WINDOW
    qi2 = lax.broadcasted_iota(jnp.int32, (rows, LANE), 0) & (dq - 1)
    ok2 = lax.broadcasted_iota(jnp.int32, (rows, LANE), 1) <= qi2
    s1 = jnp.where(ok1, jnp.dot(q, wp, preferred_element_type=F32) + wb1_ref[...], NEG_INF)
    s2 = jnp.where(ok2, _qk(q, wt) + wb2_ref[...], NEG_INF)
    mx = jnp.maximum(jnp.max(s1, -1, keepdims=True), jnp.max(s2, -1, keepdims=True))
    e1 = jnp.where(ok1, jnp.exp(s1 - mx), 0.0)
    e2 = jnp.where(ok2, jnp.exp(s2 - mx), 0.0)
    den = jnp.maximum(jnp.sum(e1, -1, keepdims=True) + jnp.sum(e2, -1, keepdims=True), 1e-30)
    o_ref[:, LANE:] = (_qk((e1 / den).astype(BF), wp)
                       + jnp.dot((e2 / den).astype(BF), wt, preferred_element_type=F32))


def _stream_kernel(dq, nsteps, pt_ref, qml_ref, qmp_ref, qf_ref, qd_ref, qs_ref, sm_ref, *rest):
    npg = 7 * PG
    pages = rest[:npg]
    (tl_ref, tf_ref, td_ref, ts_ref, tlf_ref, dbl_ref, dbt_ref, dbc_ref, sbl_ref, sbt_ref, sbc_ref,
     dl_ref, lam0_ref) = rest[npg: npg + 13]
    om_ref, of_ref, od_ref, os_ref = rest[npg + 13: npg + 17]
    (mm, lm, am, mf, lf, af, md, ld, ad, ms, ls, as_, carry) = rest[npg + 17:]
    t = pl.program_id(1)

    @pl.when(t == 0)
    def _():
        _init_state(mm, lm, am)
        _init_state(mf, lf, af)
        _init_state(md, ld, ad)
        _init_state(ms, ls, as_)
        carry[...] = jnp.zeros(carry.shape, F32)

    qml, qmp, qf, qd, qs = qml_ref[...], qmp_ref[...], qf_ref[...], qd_ref[...], qs_ref[...]
    tlf = tlf_ref[...]
    cs = [tlf[:, 0:1]]
    for j in range(1, dq):
        cs.append(cs[-1] + tlf[:, j: j + 1])
    qrow = lax.broadcasted_iota(jnp.int32, (qf.shape[0], 1), 0) & (dq - 1)
    cq = cs[dq - 1]
    for j in range(dq - 2, -1, -1):
        cq = jnp.where(qrow == j, cs[j], cq)

    lane = lax.broadcasted_iota(jnp.int32, (qs.shape[0], LANE), 1)
    sm = sm_ref[...]
    first = t == 0
    mla_t, fox_t, diff_t, sel_t = [], [], [], []
    run = carry[...]
    for g in range(PG):
        lat_r, kpe_r, fox_r, sfx_r, tot_r, diff_r, sel_r = pages[7 * g: 7 * g + 7]
        lat = lat_r[...].astype(BF)
        kpe = kpe_r[...].astype(BF)
        s = _qk(qml, lat) + jnp.dot(qmp[:, :D_ROPE], kpe, preferred_element_type=F32)
        mla_t.append((s, lat, False))
        fkv = fox_r[...].astype(BF)
        s = jnp.dot(qf, fkv[:LANE], preferred_element_type=F32) + (cq + run + sfx_r[...])
        fox_t.append((s, fkv[LANE:], True))
        run = run + tot_r[:, 0:1]
        dkv = diff_r[...].astype(BF)
        dbias = jnp.where(first, dbl_ref[...], dbc_ref[...]) if g == 0 else dbc_ref[...]
        diff_t.append((jnp.dot(qd, dkv[:LANE], preferred_element_type=F32) + dbias, dkv[LANE:], True))
        skv = sel_r[...].astype(BF)
        sbias = jnp.where(first, sbl_ref[...], sbc_ref[...]) if g == 0 else sbc_ref[...]
        madd = jnp.where(lane < SEL_BLK, sm[:, 2 * g: 2 * g + 1], sm[:, 2 * g + 1: 2 * g + 2])
        sel_t.append((jnp.dot(qs, skv, preferred_element_type=F32) + sbias + madd, skv, True))
    carry[...] = run
    _online_multi(mla_t, mm, lm, am)
    _online_multi(fox_t, mf, lf, af)
    _online_multi(diff_t, md, ld, ad)
    _online_multi(sel_t, ms, ls, as_)

    @pl.when(t == nsteps - 1)
    def _():
        def causal(rows):
            qi = lax.broadcasted_iota(jnp.int32, (rows, LANE), 0) & (dq - 1)
            return lax.broadcasted_iota(jnp.int32, (rows, LANE), 1) <= qi

        tl, tf, td, ts = tl_ref[...], tf_ref[...], td_ref[...], ts_ref[...]
        s = _qk(qml, tl[:, :D_C]) + _qk(qmp, tl[:, D_C:])
        _online_multi([(jnp.where(causal(qml.shape[0]), s, NEG_INF), tl[:, :D_C], False)], mm, lm, am)
        ctile = jnp.zeros((qf.shape[0], LANE), F32)
        for j in range(dq):
            ctile = jnp.where(lane == j, cs[j], ctile)
        s = _qk(qf, tf[:, :LANE]) + (cq - ctile)
        _online_multi([(jnp.where(causal(qf.shape[0]), s, NEG_INF), tf[:, LANE:], False)], mf, lf, af)
        s = _qk(qd, td[:, :LANE]) + dbt_ref[...]
        _online_multi([(jnp.where(causal(qd.shape[0]), s, NEG_INF), td[:, LANE:], False)], md, ld, ad)
        s = _qk(qs, ts) + sbt_ref[...]
        _online_multi([(jnp.where(causal(qs.shape[0]), s, NEG_INF), ts, False)], ms, ls, as_)
        om_ref[...] = am[...] / lm[...]
        of_ref[...] = af[...] / lf[...]
        os_ref[...] = as_[...] / ls[...]
        dl = dl_ref[...]
        lam = (jnp.exp(jnp.sum(dl[0:1] * dl[1:2], -1, keepdims=True))
               - jnp.exp(jnp.sum(dl[2:3] * dl[3:4], -1, keepdims=True)) + lam0_ref[...])
        od = ad[...] / ld[...]
        half = G_DIFF * dq
        for kvh in range(KV_DIFF):
            o0 = od[2 * half * kvh: 2 * half * kvh + half]
            o1 = od[2 * half * kvh + half: 2 * half * (kvh + 1)]
            od_ref[half * kvh: half * (kvh + 1), :] = o0 - lam * o1


def _sample_bias_tables(rel_bias, past_len, dq, n_win):
    tbl_a, tbl_d = rel_bias[:, :H_NSA], rel_bias[:, H_NSA:]
    q = jnp.arange(dq)[:, None]

    def rows_hq(tbl, dist):
        b = tbl[_t5_bucket(dist)]
        return jnp.transpose(b, (2, 0, 1)).reshape(tbl.shape[1] * dq, dist.shape[1])

    n_cmp = past_len // CMP_BLK
    ends = (jnp.arange(n_cmp) + 1) * CMP_BLK - 1
    cb = rows_hq(tbl_a, past_len + q - ends[None, :]).reshape(H_NSA * dq, n_cmp // 2, 2)
    cb = jnp.moveaxis(cb, -1, 0)
    key = jnp.arange(LANE)[None, :]
    wb1 = rows_hq(tbl_a, n_win + q - jnp.arange(n_win)[None, :])
    near = rows_hq(tbl_a, PAGE_SIZE + q - key)
    tail = rows_hq(tbl_a, q - key)
    far = rows_hq(tbl_a, 2 * PAGE_SIZE + q - key)[:, :1]

    def diff_rows(x):
        x = x.reshape(KV_DIFF, 1, G_DIFF, dq, x.shape[-1])
        return jnp.broadcast_to(x, (KV_DIFF, 2, G_DIFF, dq, x.shape[-1])).reshape(2 * H_DIFF * dq, x.shape[-1])

    dnear = diff_rows(rows_hq(tbl_d, PAGE_SIZE + q - key))
    dtail = diff_rows(rows_hq(tbl_d, q - key))
    dfar = diff_rows(rows_hq(tbl_d, 2 * PAGE_SIZE + q - key)[:, :1])
    return cb, wb1, near, tail, far, dnear, dtail, dfar


def _rows_hq(a, DB, dq, H):
    W = a.shape[1] // H
    return jnp.transpose(a.reshape(DB, dq, H, W), (0, 2, 1, 3)).reshape(DB, H * dq, W)


def _rows_token(a, DB, dq, H):
    W = a.shape[-1]
    return jnp.transpose(a.reshape(DB, H, dq, W), (0, 2, 1, 3)).reshape(DB * dq, H * W)


def _pad_tail(a, DB, dq):
    a = a.reshape(DB, dq, a.shape[-1])
    return jnp.pad(a, ((0, 0), (0, LANE - dq), (0, 0)))


def _sample_layer(x, l, lw, tabs, bias, page_table, caches, sfx, tot):
    (c_cmp, c_sel, st_win, win_prev, c_lat, c_kpe, c_fox, c_diff) = caches
    cb, wb1, near, tail, far, dnear, dtail, dfar = bias
    DB, dq, _ = x.shape
    M = DB * dq
    NP = page_table.shape[1]
    n_pool = c_lat.shape[1]
    n_win = st_win.shape[3]
    assert dq & (dq - 1) == 0 and H_NSA * dq == 16 and NP % PG == 0
    x2 = x.reshape(M, D_MODEL)
    (qa, cmp_s, sel_s, win_s, selb, winb, misc, lat, kcat, qcat, fq, fkv, fkvb, dq_, dkv, dkvb, z) = _proj(
        x2, lw['wp'], lw['wuq'], lw['wukbd'], lw['gq'], lw['gkv'], lw['fb'], tabs, M)

    kvc_pool = _cmp_pool_summary(c_cmp, l, lw['cpos'], lw['wc'], lw['w2'],
                                 CMP_PAGES if n_pool % CMP_PAGES == 0 else n_pool)
    gg = min(GATHER_G, NP)
    kvc = _gather_rows(kvc_pool.reshape(n_pool, 1, (PAGE_SIZE // CMP_BLK) * LANE), page_table, gg)
    n_cmp = NP * (PAGE_SIZE // CMP_BLK)
    kvc = jnp.swapaxes(kvc.reshape(DB, n_cmp // 2, 2, LANE), 1, 2)

    q_nsa = _rows_hq(qa, DB, dq, H_NSA)
    per_b = lambda *blk: pl.BlockSpec((None,) + blk, lambda b: (b,) + (0,) * len(blk))
    cst = lambda a: pl.BlockSpec(a.shape, lambda b: (0,) * a.ndim)
    wt = _pad_tail(winb, DB, dq)
    ocw, selmask = pl.pallas_call(
        lambda *r: _nsa_sample_kernel(dq, n_win, *r),
        grid=(DB,),
        in_specs=[per_b(16, LANE), per_b(2, n_cmp // 2, LANE), cst(cb),
                  pl.BlockSpec((None, None, LANE, n_win), lambda b: (l, b, 0, 0)), per_b(LANE, LANE),
                  cst(wb1), cst(tail)],
        out_specs=[per_b(16, 2 * LANE), per_b(16, n_cmp // 2)],
        out_shape=[jax.ShapeDtypeStruct((DB, 16, 2 * LANE), F32), jax.ShapeDtypeStruct((DB, 16, n_cmp // 2), F32)],
        compiler_params=pltpu.CompilerParams(dimension_semantics=("arbitrary",), vmem_limit_bytes=VMEM_LIMIT),
        name="nsa_sample",
    )(q_nsa, kvc, cb, st_win, wt, wb1, tail)

    nsteps = NP // PG
    sm = selmask.reshape(DB, 16, NP, 2)[:, :, ::-1]
    sm = jnp.transpose(sm.reshape(DB, 16, nsteps, PG * 2), (0, 2, 1, 3))

    qm = _rows_hq(qcat, DB, dq, H_MLA)
    qml, qmp = qm[..., :D_C], qm[..., D_C:]
    qf = _rows_hq(fq, DB, dq, H_FOX)
    qd = _rows_hq(dq_, DB, dq, 2 * H_DIFF)
    logf_new = misc[:, MISC_F: MISC_F + H_FOX].reshape(DB, dq, H_FOX)
    tlf = jnp.broadcast_to(jnp.swapaxes(logf_new, 1, 2)[:, :, None, :], (DB, H_FOX, dq, dq)).reshape(DB, 16, dq)
    tlf = jnp.pad(tlf, ((0, 0), (0, 0), (0, LANE - dq)))

    def page(rows, cols):
        return [pl.BlockSpec((None, None, rows, cols),
                             lambda b, t, pt, g=g: (l, pt[b, NP - 1 - (t * PG + g)], 0, 0)) for g in range(PG)]

    lat_s, kpe_s, fox_s, sfx_s, tot_s, diff_s, sel_s_ = (
        page(PAGE_SIZE, D_C), page(D_ROPE, PAGE_SIZE), page(256, PAGE_SIZE), page(16, LANE), page(16, LANE),
        page(256, PAGE_SIZE), page(LANE, PAGE_SIZE))
    page_specs, page_args = [], []
    for g in range(PG):
        page_specs += [lat_s[g], kpe_s[g], fox_s[g], sfx_s[g], tot_s[g], diff_s[g], sel_s_[g]]
        page_args += [c_lat, c_kpe, c_fox, sfx, tot, c_diff, c_sel]
    pb = lambda *blk: pl.BlockSpec((None,) + blk, lambda b, t, pt: (b,) + (0,) * len(blk))
    cs = lambda a: pl.BlockSpec(a.shape, lambda b, t, pt: (0,) * a.ndim)
    tails = (_pad_tail(kcat, DB, dq), _pad_tail(fkvb, DB, dq), _pad_tail(dkvb, DB, dq), _pad_tail(selb, DB, dq))
    consts = (dnear, dtail, dfar, near, tail, far, lw['dl'], lw['lam0'])
    st = lambda rows, dv: [pltpu.VMEM((rows, 1), F32), pltpu.VMEM((rows, 1), F32), pltpu.VMEM((rows, dv), F32)]
    om, of, od, os_ = pl.pallas_call(
        lambda *r: _stream_kernel(dq, nsteps, *r),
        grid_spec=pltpu.PrefetchScalarGridSpec(
            num_scalar_prefetch=1, grid=(DB, nsteps),
            in_specs=[pb(16, D_C), pb(16, LANE), pb(16, LANE), pb(32, LANE), pb(16, LANE),
                      pl.BlockSpec((None, None, 16, 2 * PG), lambda b, t, pt: (b, t, 0, 0))]
            + page_specs
            + [pb(LANE, D_C + LANE), pb(LANE, 256), pb(LANE, 256), pb(LANE, LANE), pb(16, LANE)]
            + [cs(a) for a in consts],
            out_specs=[pb(16, D_C), pb(16, LANE), pb(16, LANE), pb(16, LANE)],
            scratch_shapes=st(16, D_C) + st(16, LANE) + st(32, LANE) + st(16, LANE) + [pltpu.VMEM((16, 1), F32)]),
        out_shape=[jax.ShapeDtypeStruct((DB, 16, D_C), F32), jax.ShapeDtypeStruct((DB, 16, LANE), F32),
                   jax.ShapeDtypeStruct((DB, 16, LANE), F32), jax.ShapeDtypeStruct((DB, 16, LANE), F32)],
        compiler_params=pltpu.CompilerParams(dimension_semantics=("arbitrary", "arbitrary"),
                                             vmem_limit_bytes=VMEM_LIMIT),
        name="sample_stream",
    )(page_table, qml, qmp, qf, qd, q_nsa, sm, *page_args, *tails, tlf, *consts)

    olat = _rows_token(om, DB, dq, H_MLA)

    def pick_kv(o, H, G):
        o = o.reshape(DB, H, dq, 2, DH)
        o = jnp.stack([o[:, h, :, h // G] for h in range(H)], 1)
        return _rows_token(o.reshape(DB, H * dq, DH), DB, dq, H)

    ofox = pick_kv(of, H_FOX, G_FOX)
    odiff = pick_kv(od, H_DIFF, G_DIFF)
    upper = lambda o: _rows_token(o[..., DH:], DB, dq, H_NSA)
    onsa = jnp.concatenate([upper(ocw[..., :LANE]), upper(os_), upper(ocw[..., LANE:])], -1)
    x_new = _post(x2, olat, onsa, ofox, odiff, misc, z, lw['wuvbd'], lw['dg'], lw['ds'], lw['wbr'], lw['wm'],
                  lw['wo'], lw['lg'], lw['lb'], M).reshape(DB, dq, D_MODEL)
    win_all = jnp.concatenate([win_prev[l], win_s.reshape(DB, dq, LANE)], 1)
    wlen = min(WINDOW, win_all.shape[1])
    states = (cmp_s.reshape(DB, dq, 2, DH), sel_s.reshape(DB, dq, 2, DH),
              win_all[:, win_all.shape[1] - wlen:].reshape(DB, wlen, 2, DH),
              lat.reshape(DB, dq, D_C), misc[:, MISC_KPE: MISC_KPE + D_ROPE].reshape(DB, dq, D_ROPE),
              fkv.reshape(DB, dq, 2, KV_FOX, DH), logf_new, dkv.reshape(DB, dq, 2, KV_DIFF, 2 * DD))
    return x_new, states


def kernel(x_prompt, x_sample, cache_nsa_cmp_kv, cache_nsa_sel_kv, state_nsa_win_kv, cache_mla_latent,
           cache_mla_kpe, cache_fox_kv, cache_fox_logf, cache_diff_kv, page_table, rel_bias, w_in,
           nsa_cmp_pos, nsa_cmp_w1, nsa_cmp_w2, mla_q_norm, mla_kv_norm, mla_w_uq, mla_w_uk, mla_w_uv,
           fox_forget_bias, diff_lambda, diff_subln, w_branch, w_out, ln_g, ln_b):
    past_len = page_table.shape[1] * PAGE_SIZE
    S = x_prompt.shape[1]
    DB, dq = x_sample.shape[:2]
    depth, n_pool = cache_mla_latent.shape[:2]
    tabs_s = _rope_tables(jnp.tile(past_len + jnp.arange(dq), DB))
    bias_s = _sample_bias_tables(rel_bias, past_len, dq, state_nsa_win_kv.shape[2])
    lf = jnp.swapaxes(cache_fox_logf, 2, 3)
    lf = jnp.broadcast_to(lf[:, :, :, None, :], (depth, n_pool, H_FOX, dq, PAGE_SIZE))
    n_rows = depth * n_pool * H_FOX * dq
    sfx, tot = _page_suffix(lf.reshape(n_rows, PAGE_SIZE), 2048 if n_rows % 2048 == 0 else n_rows)
    sfx = sfx.reshape(depth, n_pool, H_FOX * dq, PAGE_SIZE)
    tot = tot.reshape(depth, n_pool, H_FOX * dq, PAGE_SIZE)
    fm = lambda a, feat: jnp.swapaxes(a.reshape(a.shape[0], a.shape[1], a.shape[2], feat), 2, 3)
    win_prev = state_nsa_win_kv.reshape(depth, DB, state_nsa_win_kv.shape[2], 2 * DH)
    caches = (fm(cache_nsa_cmp_kv, 2 * DH), fm(cache_nsa_sel_kv, 2 * DH), fm(state_nsa_win_kv, 2 * DH), win_prev,
              cache_mla_latent, fm(cache_mla_kpe, D_ROPE),
              fm(cache_fox_kv, 2 * KV_FOX * DH), fm(cache_diff_kv, 2 * KV_DIFF * 2 * DD))
    tabs_p = _rope_tables(jnp.arange(S))
    tb_a, tb_d, cb = _prompt_bias_tables(rel_bias, S)
    u_tri = (jnp.arange(LANE)[:, None] <= jnp.arange(LANE)[None, :]).astype(BF)
    xp, xs = x_prompt, x_sample
    st_p, st_s = [], []
    for l in range(depth):
        lwp = _layer_weights(l, w_in, nsa_cmp_pos, nsa_cmp_w1, nsa_cmp_w2, mla_q_norm, mla_kv_norm, mla_w_uq,
                             mla_w_uk, mla_w_uv, fox_forget_bias, diff_lambda, diff_subln, w_branch, w_out,
                             ln_g, ln_b)
        xp, sp = _prompt_layer(xp, lwp, tabs_p, tb_a, tb_d, cb, u_tri)
        xs, ss = _sample_layer(xs, l, lwp, tabs_s, bias_s, page_table, caches, sfx, tot)
        st_p.append(sp)
        st_s.append(ss)
    (p_cmp, p_sel, p_win, p_lat, p_kpe, p_fkv, p_flf, p_dkv) = [jnp.stack(z) for z in zip(*st_p)]
    (s_cmp, s_sel, s_win, s_lat, s_kpe, s_fkv, s_flf, s_dkv) = [jnp.stack(z) for z in zip(*st_s)]
    return (xp, xs, p_cmp, s_cmp, p_sel, s_sel, p_win, s_win, p_lat, s_lat, p_kpe, s_kpe,
            p_fkv, s_fkv, p_flf, s_flf, p_dkv, s_dkv)
```

```python
import functools
import math
import numpy as np
import jax
import jax.numpy as jnp
from jax import lax
from jax.experimental import pallas as pl
from jax.experimental.pallas import tpu as pltpu


D_MODEL = 1024
DEPTH = 4
PAGE_SIZE = 128

N_BRANCH = 4
BR_W = D_MODEL // N_BRANCH
DH = 64
QBLK = 128
H_NSA = 4
CMP_BLK = 32
SEL_BLK = 64
CMP_PER_SEL = SEL_BLK // CMP_BLK
TOP_N = 16
N_LOCAL = 2
WINDOW = 512
H_MLA = 4
D_CQ = 256
D_C = 256
D_NOPE = 64
D_ROPE = 32
D_V = 64
ROPE_BASE = 10000.0
MLA_SCALE = (D_NOPE + D_ROPE) ** -0.5
H_FOX = 4
KV_FOX = 2
G_FOX = H_FOX // KV_FOX
H_DIFF = 4
KV_DIFF = 2
G_DIFF = H_DIFF // KV_DIFF
DD = 32
N_BUCKETS = 32
MAX_DIST = 128
DEEPNORM_ALPHA = (2 * DEPTH) ** 0.25
NEG_INF = -1e30
FORCE = 1e9

IN_SIZES = (
    H_NSA * DH, 2 * DH, 2 * DH, 2 * DH, 3 * H_NSA, BR_W,
    D_CQ, D_C, D_ROPE, BR_W,
    H_FOX * DH, KV_FOX * DH, KV_FOX * DH, H_FOX, BR_W,
    H_DIFF * 2 * DD, KV_DIFF * 2 * DD, KV_DIFF * 2 * DD, BR_W,
    N_BRANCH * D_MODEL,
)
N_IN = sum(IN_SIZES)


def _mm_kernel(x_ref, w_ref, o_ref):
    o_ref[...] = jnp.dot(x_ref[...].astype(jnp.bfloat16), w_ref[...].astype(jnp.bfloat16),
                         preferred_element_type=jnp.float32)


def _pmatmul(x, w):
    lead = x.shape[:-1]
    K = x.shape[-1]
    N = w.shape[-1]
    x2 = x.reshape(-1, K)
    M = x2.shape[0]
    tn = 256
    n_pad = -(-N // tn) * tn
    if n_pad != N:
        w = jnp.pad(w, ((0, 0), (0, n_pad - N)))
    tm = min(M, 1024)
    out = pl.pallas_call(
        _mm_kernel,
        grid=(M // tm, n_pad // tn),
        in_specs=[pl.BlockSpec((tm, K), lambda i, j: (i, 0)),
                  pl.BlockSpec((K, tn), lambda i, j: (0, j))],
        out_specs=pl.BlockSpec((tm, tn), lambda i, j: (i, j)),
        out_shape=jax.ShapeDtypeStruct((M, n_pad), jnp.float32),
        name="matmul",
    )(x2, w)
    return out[:, :N].reshape(*lead, N)


def _split(h, sizes):
    out, start = [], 0
    for n in sizes:
        out.append(h[..., start:start + n])
        start += n
    return out


def _rmsnorm(x, g, eps=1e-6):
    xf = x.astype(jnp.float32)
    y = xf * lax.rsqrt(jnp.mean(xf * xf, -1, keepdims=True) + eps)
    return (y * g).astype(x.dtype)


def _layernorm(x, g, b, eps=1e-5):
    xf = x.astype(jnp.float32)
    mu = jnp.mean(xf, -1, keepdims=True)
    var = jnp.mean(jnp.square(xf - mu), -1, keepdims=True)
    return ((xf - mu) * lax.rsqrt(var + eps) * g + b).astype(x.dtype)


def _masked_softmax(s, mask):
    s = jnp.where(mask, s.astype(jnp.float32), NEG_INF)
    e = jnp.where(mask, jnp.exp(s - jnp.max(s, -1, keepdims=True)), 0.0)
    return e / jnp.maximum(jnp.sum(e, -1, keepdims=True), 1e-30)


def _t5_bucket(dist):
    n = jnp.maximum(dist, 0)
    exact = N_BUCKETS // 2
    nf = jnp.maximum(n, 1).astype(jnp.float32)
    large = exact + (jnp.log(nf / exact) / math.log(MAX_DIST / exact) * (N_BUCKETS - exact)).astype(jnp.int32)
    return jnp.where(n < exact, n, jnp.minimum(large, N_BUCKETS - 1))


def _rope(x, pos):
    half = D_ROPE // 2
    inv = ROPE_BASE ** (-jnp.arange(half, dtype=jnp.float32) / half)
    ang = pos.astype(jnp.float32)[:, None] * inv
    ang = ang.reshape(ang.shape[0], *([1] * (x.ndim - 3)), half)
    cos, sin = jnp.cos(ang), jnp.sin(ang)
    x1 = x[..., :half].astype(jnp.float32)
    x2 = x[..., half:].astype(jnp.float32)
    return jnp.concatenate([x1 * cos - x2 * sin, x2 * cos + x1 * sin], -1).astype(x.dtype)


def _sweep(fn, q_pos, *q_arrays):
    sq = q_pos.shape[0]
    if sq > QBLK and sq % QBLK == 0:
        nb = sq // QBLK
        blocks = tuple(jnp.moveaxis(a.reshape(a.shape[0], nb, QBLK, *a.shape[2:]), 1, 0) for a in q_arrays)
        out = lax.map(lambda args: fn(args[0], *args[1]), (q_pos.reshape(nb, QBLK), blocks))
        out = jnp.moveaxis(out, 0, 1)
        return out.reshape(out.shape[0], sq, *out.shape[3:])
    return fn(q_pos, *q_arrays)


def _block_rows(rows):
    B, T = rows.shape[:2]
    n_blk = -(-T // SEL_BLK)
    pad = ((0, 0), (0, n_blk * SEL_BLK - T)) + ((0, 0),) * (rows.ndim - 2)
    return jnp.pad(rows, pad).reshape(B, n_blk, SEL_BLK, *rows.shape[2:])


def _take_blocks(blocks, idx):
    return jax.vmap(lambda blk, i: blk[i])(blocks, idx)


def _local_block_gather(rows):
    blocks = _block_rows(rows)
    return lambda idx: _take_blocks(blocks, idx)


def _paged_block_gather(pool, page_table, new_rows):
    B = new_rows.shape[0]
    bpp = PAGE_SIZE // SEL_BLK
    n_past = page_table.shape[1] * bpp
    pool_blocks = pool.reshape(-1, SEL_BLK, *pool.shape[2:])
    tail = _block_rows(new_rows)
    n_tail = tail.shape[1]

    def gather(idx):
        ip = jnp.clip(idx, 0, n_past - 1)
        page = jnp.take_along_axis(page_table, (ip // bpp).reshape(B, -1), axis=1).reshape(ip.shape)
        from_pool = pool_blocks[page * bpp + ip % bpp]
        from_tail = _take_blocks(tail, jnp.clip(idx - n_past, 0, n_tail - 1))
        return jnp.where((idx < n_past)[..., None, None, None], from_pool, from_tail)
    return gather


def _nsa_compressed(q, qpos, rows, cmp_pos, cmp_w1, cmp_w2, tbl):
    B, T = rows.shape[:2]
    Q = q.shape[1]
    n_cmp = T // CMP_BLK
    blocks = rows[:, :n_cmp * CMP_BLK].reshape(B, n_cmp, CMP_BLK, 2, DH)
    hid = jnp.einsum('bnlcd,cldh->bnch', blocks + jnp.swapaxes(cmp_pos, 0, 1),
                     cmp_w1.reshape(2, CMP_BLK, DH, DH))
    kv_c = jnp.einsum('bnch,che->bnce', jax.nn.silu(hid), cmp_w2)
    ends = (jnp.arange(n_cmp) + 1) * CMP_BLK - 1
    dist = qpos[:, None] - ends[None, :]
    s = jnp.einsum('bqhd,bnd->bhqn', q, kv_c[:, :, 0]).astype(jnp.float32)
    s = s + jnp.moveaxis(tbl[_t5_bucket(dist)], -1, 0)
    p = _masked_softmax(s, dist >= 0)
    o = jnp.einsum('bhqn,bnd->bqhd', p.astype(q.dtype), kv_c[:, :, 1])
    n_sel = -(-T // SEL_BLK)
    imp = jnp.pad(jnp.sum(p, axis=1), ((0, 0), (0, 0), (0, n_sel * CMP_PER_SEL - n_cmp)))
    imp = imp.reshape(B, Q, n_sel, CMP_PER_SEL).sum(-1)
    blk = jnp.arange(n_sel)[None, :]
    cur = (qpos // SEL_BLK)[:, None]
    valid = blk <= cur
    forced = valid & ((blk == 0) | (cur - blk < N_LOCAL))
    score = jnp.where(valid, jnp.where(forced, FORCE, imp), -FORCE)
    _, idx = lax.top_k(score, min(TOP_N, n_sel))
    return o, idx


def _nsa_selected(qpos, q, idx, gather, tbl):
    kv = gather(idx)
    kpos = idx[..., None] * SEL_BLK + jnp.arange(SEL_BLK)
    dist = qpos[None, :, None, None] - kpos
    s = jnp.einsum('bqhd,bqnld->bqhnl', q, kv[..., 0, :]).astype(jnp.float32)
    s = s + jnp.moveaxis(tbl[_t5_bucket(dist)], -1, 2)
    B, Q, H, n, L = s.shape
    p = _masked_softmax(s.reshape(B, Q, H, n * L), (dist >= 0).reshape(B, Q, 1, n * L))
    return jnp.einsum('bqhnl,bqnld->bqhd', p.reshape(B, Q, H, n, L).astype(q.dtype), kv[..., 1, :])


def _window_attend(q, kv, qpos, kpos, tbl):
    dist = qpos[:, None] - kpos[None, :]
    s = jnp.einsum('bqhd,bkd->bhqk', q, kv[:, :, 0]).astype(jnp.float32)
    s = s + jnp.moveaxis(tbl[_t5_bucket(dist)], -1, 0)
    mask = (dist >= 0) & (dist <= WINDOW) & (kpos >= 0)[None, :]
    p = _masked_softmax(s, mask)
    return jnp.einsum('bhqk,bkd->bqhd', p.astype(q.dtype), kv[:, :, 1])


def _window_banded(q, kv, pos, tbl):
    B, S = q.shape[:2]
    qb = QBLK if S % QBLK == 0 else S
    nb = S // qb
    padded = jnp.pad(kv, ((0, 0), (WINDOW, 0), (0, 0), (0, 0)))
    kidx = jnp.arange(nb)[:, None] * qb + jnp.arange(WINDOW + qb)[None, :]
    kv_b = padded[:, kidx]
    kpos_b = pos[0] + kidx - WINDOW
    q_b = q.reshape(B, nb, qb, H_NSA, DH)
    o = jax.vmap(_window_attend, in_axes=(1, 1, 0, 0, None), out_axes=1)(
        q_b, kv_b, pos.reshape(nb, qb), kpos_b, tbl)
    return o.reshape(B, S, H_NSA, DH)


def _mla_attend(qpos, q_lat, q_pe, lat, kpe, kpos):
    s = (jnp.einsum('bqhc,bkc->bhqk', q_lat, lat) + jnp.einsum('bqhr,bkr->bhqk', q_pe, kpe)).astype(jnp.float32)
    p = _masked_softmax(s, kpos[None, :] <= qpos[:, None])
    return jnp.einsum('bhqk,bkc->bqhc', p.astype(lat.dtype), lat)


def _fox_attend(qpos, q, r_q, kv, r_k, kpos):
    B, Q = q.shape[:2]
    T = kv.shape[1]
    qg = q.reshape(B, Q, KV_FOX, G_FOX, DH)
    s = jnp.einsum('bqkgd,btkd->bkgqt', qg, kv[:, :, 0]).astype(jnp.float32)
    rk = r_k.reshape(B, T, KV_FOX, G_FOX).transpose(0, 2, 3, 1)[:, :, :, None, :]
    rq = r_q.reshape(B, Q, KV_FOX, G_FOX).transpose(0, 2, 3, 1)[..., None]
    p = _masked_softmax(s + rk - rq, kpos[None, :] <= qpos[:, None])
    o = jnp.einsum('bkgqt,btkd->bqkgd', p.astype(kv.dtype), kv[:, :, 1])
    return o.reshape(B, Q, H_FOX, DH)


def _diff_attend(qpos, q, kv, kpos, tbl, lam):
    B, Q = q.shape[:2]
    T = kv.shape[1]
    k = kv[:, :, 0].reshape(B, T, KV_DIFF, 2, DD)
    v = kv[:, :, 1]
    qg = q.reshape(B, Q, KV_DIFF, G_DIFF, 2, DD)
    s = jnp.einsum('bqkgid,btkid->ibkgqt', qg, k).astype(jnp.float32)
    dist = qpos[:, None] - kpos[None, :]
    bias = jnp.moveaxis(tbl[_t5_bucket(dist)], -1, 0).reshape(KV_DIFF, G_DIFF, Q, T)
    p = _masked_softmax(s + bias, dist >= 0)
    a = (p[0] - lam * p[1]).astype(v.dtype)
    o = jnp.einsum('bkgqt,btkv->bqkgv', a, v)
    return o.reshape(B, Q, H_DIFF, 2 * DD)


def _gather_pages(pool, page_table):
    g = pool[page_table]
    return g.reshape(g.shape[0], -1, *g.shape[3:])


def _layer(x, pos, lidx, past, rel_bias, w_in, cmp_pos, cmp_w1, cmp_w2, mla_gq, mla_gkv,
           mla_wuq, mla_wuk, mla_wuv, fox_bf, diff_lam, diff_g, w_branch, w_out, ln_g, ln_b):
    B, S, _ = x.shape
    dt = x.dtype
    (a_q, a_cmp, a_sel, a_win, a_gate, a_z,
     b_cq, b_ckv, b_kpe, b_z,
     c_q, c_k, c_v, c_f, c_z,
     d_q, d_k, d_v, d_z, merge_logit) = _split(_pmatmul(x, w_in), IN_SIZES)
    tbl_a, tbl_d = rel_bias[:, :H_NSA], rel_bias[:, H_NSA:]

    def with_past(name, new):
        return new if past is None else jnp.concatenate([past[name], new], axis=1)

    q_a = a_q.reshape(B, S, H_NSA, DH) * DH ** -0.5
    cmp_new = a_cmp.reshape(B, S, 2, DH)
    sel_new = a_sel.reshape(B, S, 2, DH)
    win_new = a_win.reshape(B, S, 2, DH)
    o_cmp, sel_idx = _nsa_compressed(q_a, pos, with_past('nsa_cmp', cmp_new), cmp_pos, cmp_w1, cmp_w2, tbl_a)
    if past is None:
        gather = _local_block_gather(sel_new)
        win_all = win_new
        o_win = _window_banded(q_a, win_new, pos, tbl_a)
    else:
        gather = _paged_block_gather(past['nsa_sel_pool'], past['page_table'], sel_new)
        win_all = jnp.concatenate([past['nsa_win'], win_new], axis=1)
        kpos_w = pos[0] - past['nsa_win'].shape[1] + jnp.arange(win_all.shape[1])
        o_win = _window_attend(q_a, win_all, pos, kpos_w, tbl_a)
    win_state = win_all[:, win_all.shape[1] - min(WINDOW, win_all.shape[1]):]
    o_sel = _sweep(lambda qp, qb, ib: _nsa_selected(qp, qb, ib, gather, tbl_a), pos, q_a, sel_idx)
    g_a = jax.nn.sigmoid(a_gate.reshape(B, S, H_NSA, 3))
    o_a = (g_a[..., 0:1] * o_cmp + g_a[..., 1:2] * o_sel + g_a[..., 2:3] * o_win).reshape(B, S, BR_W)

    cq = _rmsnorm(b_cq, mla_gq)
    qf = (cq @ mla_wuq).reshape(B, S, H_MLA, D_NOPE + D_ROPE)
    q_lat = jnp.einsum('bshn,chn->bshc', qf[..., :D_NOPE], mla_wuk) * MLA_SCALE
    q_pe = _rope(qf[..., D_NOPE:], pos) * MLA_SCALE
    lat_new = _rmsnorm(b_ckv, mla_gkv)
    kpe_new = _rope(b_kpe, pos)
    lat, kpe = with_past('mla_lat', lat_new), with_past('mla_kpe', kpe_new)
    kpos_all = jnp.arange(lat.shape[1])
    o_lat = _sweep(lambda qp, ql, qr: _mla_attend(qp, ql, qr, lat, kpe, kpos_all), pos, q_lat, q_pe)
    o_b = jnp.einsum('bshc,chv->bshv', o_lat, mla_wuv).reshape(B, S, BR_W)

    q_c = c_q.reshape(B, S, H_FOX, DH) * DH ** -0.5
    fkv_new = jnp.stack([c_k.reshape(B, S, KV_FOX, DH), c_v.reshape(B, S, KV_FOX, DH)], axis=2)
    logf_new = jax.nn.log_sigmoid((c_f + fox_bf).astype(jnp.float32))
    fkv = with_past('fox_kv', fkv_new)
    logf = with_past('fox_logf', logf_new).astype(jnp.float32)
    r = lax.cumsum(logf, axis=1, reverse=True) - logf
    o_c = _sweep(lambda qp, qb, rq: _fox_attend(qp, qb, rq, fkv, r, kpos_all),
                 pos, q_c, r[:, r.shape[1] - S:]).reshape(B, S, BR_W)

    q_d = d_q.reshape(B, S, H_DIFF, 2, DD) * DD ** -0.5
    dkv_new = jnp.stack([d_k.reshape(B, S, KV_DIFF, 2 * DD), d_v.reshape(B, S, KV_DIFF, 2 * DD)], axis=2)
    dkv = with_past('diff_kv', dkv_new)
    lam_init = 0.8 - 0.6 * math.exp(-0.3 * lidx)
    lam = (jnp.exp(jnp.sum(diff_lam[0] * diff_lam[1]).astype(jnp.float32))
           - jnp.exp(jnp.sum(diff_lam[2] * diff_lam[3]).astype(jnp.float32)) + lam_init)
    o_d = _sweep(lambda qp, qb: _diff_attend(qp, qb, dkv, kpos_all, tbl_d, lam), pos, q_d)
    o_d = (_rmsnorm(o_d, diff_g, 1e-5) * (1.0 - lam_init)).reshape(B, S, BR_W)

    outs = jnp.stack([o_a * jax.nn.silu(a_z), o_b * jax.nn.silu(b_z),
                      o_c * jax.nn.silu(c_z), o_d * jax.nn.silu(d_z)], axis=2)
    branch = jnp.einsum('bsnc,ncd->bsnd', outs, w_branch)
    gates = jax.nn.sigmoid(merge_logit.reshape(B, S, N_BRANCH, D_MODEL))
    y = _pmatmul(jnp.sum(gates * branch, axis=2), w_out)
    x_new = _layernorm(DEEPNORM_ALPHA * x + y, ln_g, ln_b)
    states = (cmp_new, sel_new, win_state, lat_new, kpe_new, fkv_new, logf_new.astype(dt), dkv_new)
    return x_new, states


TQ = 128
TK = 128
ROW_CHUNK = 256
LANE = 128
BF = jnp.bfloat16
F32 = jnp.float32
VMEM_LIMIT = 56 * 1024 * 1024

PG_AQ, PG_CMP, PG_SEL, PG_WIN, PG_MISC = 0, 512, 640, 768, 896
PG_CQ, PG_CKV, PG_FQ, PG_FKV, PG_DQ, PG_DKV, PG_Z, PG_END = 1024, 1280, 1536, 2048, 2304, 3328, 3584, 4608
MISC_KPE, MISC_GATE, MISC_F = 0, 32, 48
N_SMALL = N_IN - N_BRANCH * D_MODEL


def _in_offsets():
    offs, s = [], 0
    for n in IN_SIZES:
        offs.append(s)
        s += n
    return offs


def _small_proj_columns():
    import numpy as np
    (o_aq, o_cmp, o_sel, o_win, o_gate, o_az, o_cq, o_ckv, o_kpe, o_bz,
     o_fq, o_fk, o_fv, o_ff, o_cz, o_dq, o_dk, o_dv, o_dz, _) = _in_offsets()
    src = np.full((PG_END,), -1, np.int32)
    for h in range(H_NSA):
        src[PG_AQ + LANE * h: PG_AQ + LANE * h + DH] = o_aq + DH * h + np.arange(DH)
    src[PG_CMP:PG_CMP + 2 * DH] = o_cmp + np.arange(2 * DH)
    src[PG_SEL:PG_SEL + 2 * DH] = o_sel + np.arange(2 * DH)
    src[PG_WIN:PG_WIN + 2 * DH] = o_win + np.arange(2 * DH)
    src[PG_MISC + MISC_KPE: PG_MISC + MISC_KPE + D_ROPE] = o_kpe + np.arange(D_ROPE)
    src[PG_MISC + MISC_GATE: PG_MISC + MISC_GATE + 3 * H_NSA] = o_gate + np.arange(3 * H_NSA)
    src[PG_MISC + MISC_F: PG_MISC + MISC_F + H_FOX] = o_ff + np.arange(H_FOX)
    src[PG_CQ:PG_CQ + D_CQ] = o_cq + np.arange(D_CQ)
    src[PG_CKV:PG_CKV + D_C] = o_ckv + np.arange(D_C)
    for h in range(H_FOX):
        kvh = h // G_FOX
        base = PG_FQ + LANE * h + DH * kvh
        src[base: base + DH] = o_fq + DH * h + np.arange(DH)
    src[PG_FKV:PG_FKV + 2 * KV_FOX * DH] = o_fk + np.arange(2 * KV_FOX * DH)
    for kvh in range(KV_DIFF):
        for mp in range(2):
            for g in range(G_DIFF):
                h = kvh * G_DIFF + g
                r = (kvh * 2 + mp) * G_DIFF + g
                base = PG_DQ + LANE * r + 2 * DD * kvh + DD * mp
                src[base: base + DD] = o_dq + 2 * DD * h + DD * mp + np.arange(DD)
    src[PG_DKV:PG_DKV + 2 * KV_DIFF * 2 * DD] = o_dk + np.arange(2 * KV_DIFF * 2 * DD)
    src[PG_Z + 0 * BR_W: PG_Z + 1 * BR_W] = o_az + np.arange(BR_W)
    src[PG_Z + 1 * BR_W: PG_Z + 2 * BR_W] = o_bz + np.arange(BR_W)
    src[PG_Z + 2 * BR_W: PG_Z + 3 * BR_W] = o_cz + np.arange(BR_W)
    src[PG_Z + 3 * BR_W: PG_Z + 4 * BR_W] = o_dz + np.arange(BR_W)
    return src


def _permute_cols(w, src):
    parts, i, n = [], 0, len(src)
    while i < n:
        j = i + 1
        if src[i] < 0:
            while j < n and src[j] < 0:
                j += 1
            parts.append(jnp.zeros(w.shape[:-1] + (j - i,), w.dtype))
        else:
            while j < n and src[j] == src[j - 1] + 1:
                j += 1
            parts.append(w[..., int(src[i]): int(src[i]) + (j - i)])
        i = j
    return jnp.concatenate(parts, -1)


def _uq_columns():
    import numpy as np
    src = np.full((H_MLA * D_NOPE + H_MLA * LANE,), -1, np.int32)
    per = D_NOPE + D_ROPE
    for h in range(H_MLA):
        src[D_NOPE * h: D_NOPE * (h + 1)] = per * h + np.arange(D_NOPE)
        src[H_MLA * D_NOPE + LANE * h: H_MLA * D_NOPE + LANE * h + D_ROPE] = per * h + D_NOPE + np.arange(D_ROPE)
    return src


def _rope_tables(pos):
    half = D_ROPE // 2
    inv = ROPE_BASE ** (-jnp.arange(half, dtype=F32) / half)
    ang = pos.astype(F32)[:, None] * inv
    cos, sin = jnp.cos(ang), jnp.sin(ang)
    n = pos.shape[0]
    one = jnp.ones((n, LANE - D_ROPE), F32)
    zer = jnp.zeros((n, LANE - D_ROPE), F32)
    zh = jnp.zeros((n, half), F32)
    c = jnp.concatenate([cos, cos, one], -1)
    s1 = jnp.concatenate([zh, sin, zer], -1)
    s2 = jnp.concatenate([-sin, zh, zer], -1)
    return c, s1, s2


def _proj_kernel(x_ref, wp_ref, wuq_ref, wuk_ref, gq_ref, gkv_ref, fb_ref, c_ref, s1_ref, s2_ref,
                 qa_ref, cmp_ref, sel_ref, win_ref, selb_ref, winb_ref, misc_ref, lat_ref, kcat_ref,
                 qcat_ref, fq_ref, fkv_ref, fkvb_ref, dq_ref, dkv_ref, dkvb_ref, z_ref):
    xb = x_ref[...].astype(BF)

    def mm(a, b):
        return jnp.dot(xb, wp_ref[:, a:b], preferred_element_type=F32)

    cos, s1, s2 = c_ref[...], s1_ref[...], s2_ref[...]

    def rope(v):
        return v * cos + pltpu.roll(v, D_ROPE // 2, 1) * s1 + pltpu.roll(v, LANE - D_ROPE // 2, 1) * s2

    qa_ref[...] = (mm(PG_AQ, PG_CMP) * DH ** -0.5).astype(BF)
    cmp_ref[...] = mm(PG_CMP, PG_SEL)
    t = mm(PG_SEL, PG_WIN)
    sel_ref[...] = t
    selb_ref[...] = t.astype(BF)
    t = mm(PG_WIN, PG_MISC)
    win_ref[...] = t
    winb_ref[...] = t.astype(BF)

    misc = mm(PG_MISC, PG_CQ)
    lane = lax.broadcasted_iota(jnp.int32, misc.shape, 1)
    roped = rope(misc)
    xf = misc + fb_ref[...]
    logf = jnp.minimum(xf, 0.0) - jnp.log1p(jnp.exp(-jnp.abs(xf)))
    misc_ref[...] = jnp.where((lane >= MISC_F) & (lane < MISC_F + H_FOX), logf, roped)

    cq = mm(PG_CQ, PG_CKV)
    cq = cq * lax.rsqrt(jnp.mean(cq * cq, -1, keepdims=True) + 1e-6) * gq_ref[...]
    qf = jnp.dot(cq.astype(BF), wuq_ref[...], preferred_element_type=F32)
    nn = H_MLA * D_NOPE
    ql = jnp.dot(qf[:, :nn].astype(BF), wuk_ref[...], preferred_element_type=F32) * MLA_SCALE
    wq = D_C + LANE
    for h in range(H_MLA):
        qcat_ref[:, wq * h: wq * h + D_C] = ql[:, D_C * h: D_C * (h + 1)].astype(BF)
        pe = rope(qf[:, nn + LANE * h: nn + LANE * (h + 1)]) * MLA_SCALE
        qcat_ref[:, wq * h + D_C: wq * (h + 1)] = pe.astype(BF)

    ckv = mm(PG_CKV, PG_FQ)
    lat = ckv * lax.rsqrt(jnp.mean(ckv * ckv, -1, keepdims=True) + 1e-6) * gkv_ref[...]
    lat_ref[...] = lat
    kcat_ref[:, :D_C] = lat.astype(BF)
    kcat_ref[:, D_C:] = jnp.where(lane < D_ROPE, roped, 0.0).astype(BF)

    fq_ref[...] = (mm(PG_FQ, PG_FKV) * DH ** -0.5).astype(BF)
    t = mm(PG_FKV, PG_DQ)
    fkv_ref[...] = t
    fkvb_ref[...] = t.astype(BF)
    dq_ref[...] = (mm(PG_DQ, PG_DKV) * DD ** -0.5).astype(BF)
    t = mm(PG_DKV, PG_Z)
    dkv_ref[...] = t
    dkvb_ref[...] = t.astype(BF)
    z_ref[...] = mm(PG_Z, PG_END)


def _proj(x2, wp, wuq, wukbd, gq, gkv, fb, tabs, tm):
    M = x2.shape[0]
    npos = tabs[0].shape[0] // tm
    row = lambda i: (i, 0)
    cst = lambda i: (0, 0)
    tab = lambda i: (i % npos, 0)
    widths = [(512, BF), (128, F32), (128, F32), (128, F32), (128, BF), (128, BF), (128, F32), (D_C, F32),
              (D_C + LANE, BF), (H_MLA * (D_C + LANE), BF), (512, BF), (256, F32), (256, BF), (1024, BF),
              (256, F32), (256, BF), (1024, F32)]
    return pl.pallas_call(
        _proj_kernel,
        grid=(M // tm,),
        in_specs=[pl.BlockSpec((tm, D_MODEL), row),
                  pl.BlockSpec(wp.shape, cst), pl.BlockSpec(wuq.shape, cst), pl.BlockSpec(wukbd.shape, cst),
                  pl.BlockSpec((1, D_CQ), cst), pl.BlockSpec((1, D_C), cst), pl.BlockSpec((1, LANE), cst),
                  pl.BlockSpec((tm, LANE), tab), pl.BlockSpec((tm, LANE), tab), pl.BlockSpec((tm, LANE), tab)],
        out_specs=[pl.BlockSpec((tm, w), row) for w, _ in widths],
        out_shape=[jax.ShapeDtypeStruct((M, w), d) for w, d in widths],
        compiler_params=pltpu.CompilerParams(dimension_semantics=("arbitrary",), vmem_limit_bytes=VMEM_LIMIT),
        name="proj",
    )(x2, wp, wuq, wukbd, gq, gkv, fb, *tabs)


def _cmp_summary_kernel(x_ref, pos_ref, wc_ref, w2_ref, o_ref):
    xb = (x_ref[...] + pos_ref[...]).astype(BF)
    hid = jnp.dot(xb, wc_ref[...], preferred_element_type=F32)
    act = hid * jax.nn.sigmoid(hid)
    o_ref[...] = jnp.dot(act.astype(BF), w2_ref[...], preferred_element_type=F32)


def _cmp_summary(x3, layer, pos_row, wc, w2, tb):
    _, N, W = x3.shape
    return pl.pallas_call(
        _cmp_summary_kernel,
        grid=(N // tb,),
        in_specs=[pl.BlockSpec((None, tb, W), lambda i: (layer, i, 0)),
                  pl.BlockSpec((1, W), lambda i: (0, 0)),
                  pl.BlockSpec(wc.shape, lambda i: (0, 0)),
                  pl.BlockSpec(w2.shape, lambda i: (0, 0))],
        out_specs=pl.BlockSpec((tb, LANE), lambda i: (i, 0)),
        out_shape=jax.ShapeDtypeStruct((N, LANE), F32),
        compiler_params=pltpu.CompilerParams(dimension_semantics=("arbitrary",), vmem_limit_bytes=VMEM_LIMIT),
        name="cmp_summary",
    )(x3, pos_row, wc, w2)


CMP_PAGES = 64


def _cmp_paged_kernel(n_pages, pt_ref, *refs):
    pages = refs[:n_pages]
    pos_ref, wc_ref, w2_ref, o_ref, xs = refs[n_pages:]
    per = PAGE_SIZE // CMP_BLK
    for p in range(n_pages):
        xs[PAGE_SIZE * p: PAGE_SIZE * (p + 1), :] = pages[p][...].T
    w = 2 * DH
    hid = jnp.zeros((n_pages * per, w), F32)
    for i in range(CMP_BLK):
        rows = xs[pl.ds(i, n_pages * per, stride=CMP_BLK), :] + pos_ref[:, w * i: w * (i + 1)]
        hid = hid + jnp.dot(rows.astype(BF), wc_ref[w * i: w * (i + 1), :], preferred_element_type=F32)
    act = hid * jax.nn.sigmoid(hid)
    o_ref[...] = jnp.dot(act.astype(BF), w2_ref[...], preferred_element_type=F32)


def _cmp_paged_summary(xt, layer, page_table, pos_row, wc, w2, pp):
    DB, NP = page_table.shape
    per = PAGE_SIZE // CMP_BLK
    cst = lambda a: pl.BlockSpec(a.shape, lambda b, t, pt: (0,) * a.ndim)
    return pl.pallas_call(
        functools.partial(_cmp_paged_kernel, pp),
        grid_spec=pltpu.PrefetchScalarGridSpec(
            num_scalar_prefetch=1, grid=(DB, NP // pp),
            in_specs=[pl.BlockSpec((None, None, 2 * DH, PAGE_SIZE),
                                   lambda b, t, pt, g=g: (layer, pt[b, t * pp + g], 0, 0)) for g in range(pp)]
            + [cst(pos_row), cst(wc), cst(w2)],
            out_specs=pl.BlockSpec((None, pp * per, LANE), lambda b, t, pt: (b, t, 0)),
            scratch_shapes=[pltpu.VMEM((pp * PAGE_SIZE, 2 * DH), F32)]),
        out_shape=jax.ShapeDtypeStruct((DB, NP * per, LANE), F32),
        compiler_params=pltpu.CompilerParams(dimension_semantics=("arbitrary", "arbitrary"),
                                             vmem_limit_bytes=VMEM_LIMIT),
        name="cmp_paged_summary",
    )(page_table, *([xt] * pp), pos_row, wc, w2)


def _split3(x):
    hi = x.astype(BF)
    r = x - hi.astype(F32)
    mid = r.astype(BF)
    lo = (r - mid.astype(F32)).astype(BF)
    return hi, mid, lo


def _dot3(x, u):
    hi, mid, lo = _split3(x)
    d = lambda a: jnp.dot(a, u, preferred_element_type=F32)
    return d(hi) + d(mid) + d(lo)


def _prefix_kernel(x_ref, u_ref, o_ref):
    u = u_ref[...]
    carry = jnp.zeros((x_ref.shape[0], 1), F32)
    for c in range(x_ref.shape[1] // LANE):
        p = _dot3(x_ref[:, LANE * c: LANE * (c + 1)], u)
        o_ref[:, LANE * c: LANE * (c + 1)] = p + carry
        carry = carry + p[:, LANE - 1: LANE]


def _prefix(x, u):
    return pl.pallas_call(_prefix_kernel, out_shape=jax.ShapeDtypeStruct(x.shape, F32), name="fox_prefix")(x, u)


def _online(s, v, m_ref, l_ref, acc_ref):
    m_prev = m_ref[...]
    m_new = jnp.maximum(m_prev, jnp.max(s, -1, keepdims=True))
    alpha = jnp.exp(m_prev - m_new)
    p = jnp.exp(s - m_new)
    l_ref[...] = alpha * l_ref[...] + jnp.sum(p, -1, keepdims=True)
    acc_ref[...] = alpha * acc_ref[...] + jnp.dot(p.astype(BF), v, preferred_element_type=F32)
    m_ref[...] = m_new


def _init_state(m_ref, l_ref, acc_ref):
    m_ref[...] = jnp.full(m_ref.shape, NEG_INF, F32)
    l_ref[...] = jnp.zeros(l_ref.shape, F32)
    acc_ref[...] = jnp.zeros(acc_ref.shape, F32)


def _qk(q, k):
    return lax.dot_general(q, k, (((1,), (1,)), ((), ())), preferred_element_type=F32)


def _key_tiles(S):
    nkb = S // TK
    return 4 if nkb % 4 == 0 else (2 if nkb % 2 == 0 else 1)


def _causal_sweep(qb, W, rows, tile, m_ref, l_ref, acc_ref):
    nfull = lax.shift_right_logical(qb, int(math.log2(W)))
    chunks = [slice(r0, r0 + ROW_CHUNK) for r0 in range(0, rows, ROW_CHUNK)]

    def body(j, c):
        for rs in chunks:
            s, v = tile(j, rs)
            _online(s, v, m_ref.at[rs], l_ref.at[rs], acc_ref.at[rs])
        return c

    lax.fori_loop(0, nfull, body, 0)
    r = lax.broadcasted_iota(jnp.int32, (ROW_CHUNK, W * TK), 0) & (TQ - 1)
    c = lax.broadcasted_iota(jnp.int32, (ROW_CHUNK, W * TK), 1)
    ok = c <= r + TQ * (qb - W * nfull)
    for rs in chunks:
        s, v = tile(nfull, rs)
        _online(jnp.where(ok, s, NEG_INF), v, m_ref.at[rs], l_ref.at[rs], acc_ref.at[rs])


def _bias_tiles(tb_ref, heads, qb, W, j):
    cols = []
    for u in range(W):
        d = jnp.clip(qb - (W * j + u), 0, 2)
        cols.append(jnp.concatenate([tb_ref[h, d] for h in heads], 0))
    return cols[0] if W == 1 else jnp.concatenate(cols, 1)


def _mla_kernel(W, q_ref, k_ref, o_ref, qs, m_ref, l_ref, acc_ref):
    qb = pl.program_id(1)
    wq = D_C + LANE
    kw = W * TK
    for h in range(H_MLA):
        qs[TQ * h: TQ * (h + 1), :] = q_ref[:, wq * h: wq * (h + 1)]
    _init_state(m_ref, l_ref, acc_ref)

    def tile(j, rs):
        k = k_ref[pl.ds(pl.multiple_of(j * kw, kw), kw), :]
        return _qk(qs[rs, :], k), k[:, :D_C]

    _causal_sweep(qb, W, H_MLA * TQ, tile, m_ref, l_ref, acc_ref)
    for h in range(H_MLA):
        blk = slice(TQ * h, TQ * (h + 1))
        o_ref[:, D_C * h: D_C * (h + 1)] = acc_ref[blk, :] / l_ref[blk, :]


def _fox_kernel(W, q_ref, kv_ref, pq_ref, pk_ref, o_ref, qs, m_ref, l_ref, acc_ref):
    qb = pl.program_id(1)
    kw = W * TK
    for h in range(H_FOX):
        qs[TQ * h: TQ * (h + 1), :] = q_ref[:, LANE * h: LANE * (h + 1)]
    _init_state(m_ref, l_ref, acc_ref)
    pq = pq_ref[...]

    def tile(j, rs):
        kv = kv_ref[pl.ds(pl.multiple_of(j * kw, kw), kw), :]
        pk = pk_ref[j]
        decay = jnp.concatenate([pq[:, h: h + 1] - pk[h: h + 1, :]
                                 for h in range(rs.start // TQ, rs.stop // TQ)], 0)
        return _qk(qs[rs, :], kv[:, :LANE]) + decay, kv[:, LANE:]

    _causal_sweep(qb, W, H_FOX * TQ, tile, m_ref, l_ref, acc_ref)
    for h in range(H_FOX):
        blk = slice(TQ * h, TQ * (h + 1))
        kvh = h // G_FOX
        o_ref[:, DH * h: DH * (h + 1)] = acc_ref[blk, DH * kvh: DH * (kvh + 1)] / l_ref[blk, :]


def _diff_rows():
    out = []
    for kvh in range(KV_DIFF):
        for mp in range(2):
            for g in range(G_DIFF):
                out.append((kvh * G_DIFF + g, mp))
    return out


def _diff_kernel(W, q_ref, kv_ref, tb_ref, dl_ref, lam0_ref, o_ref, qs, m_ref, l_ref, acc_ref):
    qb = pl.program_id(1)
    kw = W * TK
    rows = _diff_rows()
    for r in range(len(rows)):
        qs[TQ * r: TQ * (r + 1), :] = q_ref[:, LANE * r: LANE * (r + 1)]
    _init_state(m_ref, l_ref, acc_ref)

    def tile(j, rs):
        kv = kv_ref[pl.ds(pl.multiple_of(j * kw, kw), kw), :]
        heads = [h for h, _ in rows[rs.start // TQ: rs.stop // TQ]]
        s = _qk(qs[rs, :], kv[:, :LANE]) + _bias_tiles(tb_ref, heads, qb, W, j)
        return s, kv[:, LANE:]

    _causal_sweep(qb, W, len(rows) * TQ, tile, m_ref, l_ref, acc_ref)

    dl = dl_ref[...]
    lam = (jnp.exp(jnp.sum(dl[0:1] * dl[1:2], -1, keepdims=True))
           - jnp.exp(jnp.sum(dl[2:3] * dl[3:4], -1, keepdims=True)) + lam0_ref[...])
    for kvh in range(KV_DIFF):
        for g in range(G_DIFF):
            h = kvh * G_DIFF + g
            b0 = slice(TQ * rows.index((h, 0)), TQ * (rows.index((h, 0)) + 1))
            b1 = slice(TQ * rows.index((h, 1)), TQ * (rows.index((h, 1)) + 1))
            vs = slice(2 * DD * kvh, 2 * DD * (kvh + 1))
            o0 = acc_ref[b0, vs] / l_ref[b0, :]
            o1 = acc_ref[b1, vs] / l_ref[b1, :]
            o_ref[:, 2 * DD * h: 2 * DD * (h + 1)] = o0 - lam * o1


def _nsa_kernel(W, q_ref, kvc_ref, sel_ref, win_ref, cb_ref, tb_ref, o_ref, qs, m_ref, l_ref, acc_ref, smask):
    qb = pl.program_id(1)
    R = H_NSA * TQ
    kw = W * TK
    nkb = smask.shape[0] * W
    n_sel = nkb * (TK // SEL_BLK)
    n_pair = kvc_ref.shape[1]
    qs[...] = jnp.zeros(qs.shape, BF)
    for h in range(H_NSA):
        qs[TQ * h: TQ * (h + 1), :] = q_ref[:, LANE * h: LANE * (h + 1)]
    q = qs[...]

    ke = kvc_ref[0].astype(BF)
    ko = kvc_ref[1].astype(BF)
    ri = lax.broadcasted_iota(jnp.int32, (R, n_pair), 0) & (TQ - 1)
    li = lax.broadcasted_iota(jnp.int32, (R, n_pair), 1)
    qpos = qb * TQ + ri
    me = qpos >= SEL_BLK * li + (CMP_BLK - 1)
    mo = qpos >= SEL_BLK * li + (SEL_BLK - 1)
    se = jnp.where(me, _qk(q, ke) + cb_ref[0], NEG_INF)
    so = jnp.where(mo, _qk(q, ko) + cb_ref[1], NEG_INF)
    mx = jnp.maximum(jnp.max(se, -1, keepdims=True), jnp.max(so, -1, keepdims=True))
    ee = jnp.where(me, jnp.exp(se - mx), 0.0)
    eo = jnp.where(mo, jnp.exp(so - mx), 0.0)
    den = jnp.maximum(jnp.sum(ee, -1, keepdims=True) + jnp.sum(eo, -1, keepdims=True), 1e-30)
    pe = ee / den
    po = eo / den
    ocmp = (jnp.dot(pe.astype(BF), ke, preferred_element_type=F32)
            + jnp.dot(po.astype(BF), ko, preferred_element_type=F32))
    for h in range(H_NSA):
        o_ref[:, DH * h: DH * (h + 1)] = ocmp[TQ * h: TQ * (h + 1), DH:]
    pp = pe + po
    imp = pp[0:TQ]
    for h in range(1, H_NSA):
        imp = imp + pp[TQ * h: TQ * (h + 1)]

    bi = lax.broadcasted_iota(jnp.int32, (TQ, n_pair), 1)
    qi = qb * TQ + lax.broadcasted_iota(jnp.int32, (TQ, n_pair), 0)
    cur = lax.shift_right_logical(qi, int(math.log2(SEL_BLK)))
    valid = bi <= cur
    forced = valid & ((bi == 0) | (cur - bi < N_LOCAL))
    score = jnp.where(valid, jnp.where(forced, FORCE, imp), -FORCE)
    rank = jnp.zeros((TQ, n_pair), F32)
    for i in range(n_sel):
        col = score[:, i: i + 1]
        ahead = (col > score) | ((col == score) & (bi > i))
        rank = rank + jnp.where(ahead, 1.0, 0.0)
    madd = jnp.where(rank < TOP_N, 0.0, NEG_INF)
    lane = lax.broadcasted_iota(jnp.int32, (TQ, TK), 1)
    for j in range(nkb // W):
        cols = [jnp.where(lane < SEL_BLK, madd[:, 2 * kb: 2 * kb + 1], madd[:, 2 * kb + 1: 2 * kb + 2])
                for kb in range(W * j, W * (j + 1))]
        smask[j] = cols[0] if W == 1 else jnp.concatenate(cols, 1)

    _init_state(m_ref, l_ref, acc_ref)
    heads = list(range(H_NSA))

    def sel_tile(j, rs):
        kv = sel_ref[pl.ds(pl.multiple_of(j * kw, kw), kw), :]
        hs = heads[rs.start // TQ: rs.stop // TQ]
        s = _qk(qs[rs, :], kv) + _bias_tiles(tb_ref, hs, qb, W, j) + jnp.concatenate([smask[j]] * len(hs), 0)
        return s, kv

    _causal_sweep(qb, W, R, sel_tile, m_ref, l_ref, acc_ref)
    for h in range(H_NSA):
        blk = slice(TQ * h, TQ * (h + 1))
        o_ref[:, BR_W + DH * h: BR_W + DH * (h + 1)] = acc_ref[blk, DH:] / l_ref[blk, :]

    _init_state(m_ref, l_ref, acc_ref)
    rr = lax.broadcasted_iota(jnp.int32, (R, TK), 0) & (TQ - 1)
    cc = lax.broadcasted_iota(jnp.int32, (R, TK), 1)
    back = WINDOW // TK
    tiles = []
    for u in range(back + 1):
        kb = qb - (back - u)
        kv = win_ref[pl.ds(pl.multiple_of(jnp.maximum(kb, 0) * TK, TK), TK), :]
        dist = (back - u) * TQ + rr - cc
        ok = (dist >= 0) & (dist <= WINDOW) & (kb >= 0)
        bias = jnp.concatenate([tb_ref[h, min(back - u, 2)] for h in heads], 0)
        tiles.append((jnp.where(ok, _qk(q, kv) + bias, NEG_INF), kv, False))
    _online_multi(tiles, m_ref, l_ref, acc_ref)
    for h in range(H_NSA):
        blk = slice(TQ * h, TQ * (h + 1))
        o_ref[:, 2 * BR_W + DH * h: 2 * BR_W + DH * (h + 1)] = acc_ref[blk, DH:] / l_ref[blk, :]


def _attn_call(kern, B, S, ins, specs, out_w, scratch, name):
    nq = S // TQ
    return pl.pallas_call(
        kern,
        grid=(B, nq),
        in_specs=specs,
        out_specs=pl.BlockSpec((TQ, out_w), lambda b, i: (b * nq + i, 0)),
        out_shape=jax.ShapeDtypeStruct((B * S, out_w), F32),
        scratch_shapes=scratch,
        compiler_params=pltpu.CompilerParams(dimension_semantics=("arbitrary", "arbitrary"),
                                             vmem_limit_bytes=VMEM_LIMIT),
        name=name,
    )(*ins)


def _state_scratch(rows, qw, dv):
    return [pltpu.VMEM((rows, qw), BF), pltpu.VMEM((rows, 1), F32), pltpu.VMEM((rows, 1), F32),
            pltpu.VMEM((rows, dv), F32)]


def _post_kernel(x_ref, olat_ref, onsa_ref, ofox_ref, odiff_ref, misc_ref, z_ref, wuv_ref, dg_ref, ds_ref,
                 wbr_ref, wm_ref, wo_ref, lg_ref, lb_ref, o_ref):
    x = x_ref[...]
    xb = x.astype(BF)
    gate = jax.nn.sigmoid(misc_ref[:, MISC_GATE: MISC_GATE + 3 * H_NSA])
    onsa = onsa_ref[...]
    pieces = []
    for h in range(H_NSA):
        hs = slice(DH * h, DH * (h + 1))
        pieces.append(gate[:, 3 * h: 3 * h + 1] * onsa[:, hs]
                      + gate[:, 3 * h + 1: 3 * h + 2] * onsa[:, BR_W + DH * h: BR_W + DH * (h + 1)]
                      + gate[:, 3 * h + 2: 3 * h + 3] * onsa[:, 2 * BR_W + DH * h: 2 * BR_W + DH * (h + 1)])
    oa = jnp.concatenate(pieces, -1)
    ob = jnp.dot(olat_ref[...].astype(BF), wuv_ref[...], preferred_element_type=F32)
    oc = ofox_ref[...]
    od_raw = odiff_ref[...]
    pieces = []
    for h in range(H_DIFF):
        v = od_raw[:, 2 * DD * h: 2 * DD * (h + 1)]
        pieces.append(v * lax.rsqrt(jnp.mean(v * v, -1, keepdims=True) + 1e-5))
    od = jnp.concatenate(pieces, -1) * dg_ref[...] * ds_ref[...]
    z = z_ref[...]
    zs = z * jax.nn.sigmoid(z)
    acc = jnp.zeros(x.shape, F32)
    for n, o in enumerate((oa, ob, oc, od)):
        on = (o * zs[:, BR_W * n: BR_W * (n + 1)]).astype(BF)
        br = jnp.dot(on, wbr_ref[n], preferred_element_type=F32)
        g = jax.nn.sigmoid(jnp.dot(xb, wm_ref[:, D_MODEL * n: D_MODEL * (n + 1)], preferred_element_type=F32))
        acc = acc + g * br
    y = jnp.dot(acc.astype(BF), wo_ref[...], preferred_element_type=F32)
    t = DEEPNORM_ALPHA * x + y
    mu = jnp.mean(t, -1, keepdims=True)
    var = jnp.mean(jnp.square(t - mu), -1, keepdims=True)
    o_ref[...] = (t - mu) * lax.rsqrt(var + 1e-5) * lg_ref[...] + lb_ref[...]


def _post(x2, olat, onsa, ofox, odiff, misc, z, wuv, dg, ds, wbr, wm, wo, lg, lb, tm):
    M = x2.shape[0]
    row = lambda i: (i, 0)
    cst2 = lambda i: (0, 0)
    cst3 = lambda i: (0, 0, 0)
    acts = (x2, olat, onsa, ofox, odiff, misc, z)
    return pl.pallas_call(
        _post_kernel,
        grid=(M // tm,),
        in_specs=[pl.BlockSpec((tm, a.shape[1]), row) for a in acts]
        + [pl.BlockSpec(wuv.shape, cst2), pl.BlockSpec(dg.shape, cst2), pl.BlockSpec(ds.shape, cst2),
           pl.BlockSpec(wbr.shape, cst3), pl.BlockSpec(wm.shape, cst2), pl.BlockSpec(wo.shape, cst2),
           pl.BlockSpec(lg.shape, cst2), pl.BlockSpec(lb.shape, cst2)],
        out_specs=pl.BlockSpec((tm, D_MODEL), row),
        out_shape=jax.ShapeDtypeStruct((M, D_MODEL), F32),
        compiler_params=pltpu.CompilerParams(dimension_semantics=("arbitrary",), vmem_limit_bytes=VMEM_LIMIT),
        name="post_merge",
    )(*acts, wuv, dg, ds, wbr, wm, wo, lg, lb)


def _layer_weights(l, w_in, cmp_pos, cmp_w1, cmp_w2, mla_gq, mla_gkv, mla_wuq, mla_wuk, mla_wuv,
                   fox_bf, diff_lam, diff_g, w_branch, w_out, ln_g, ln_b):
    import numpy as np
    wp = _permute_cols(w_in[l, :, :N_SMALL], _small_proj_columns()).astype(BF)
    wm = w_in[l, :, N_SMALL:].astype(BF)
    wuq = _permute_cols(mla_wuq[l], _uq_columns()).astype(BF)
    eye = jnp.eye(H_MLA, dtype=F32)
    wukbd = jnp.einsum('chn,hg->hngc', mla_wuk[l], eye).reshape(H_MLA * D_NOPE, H_MLA * D_C).astype(BF)
    wuvbd = jnp.einsum('chv,hg->hcgv', mla_wuv[l], eye).reshape(H_MLA * D_C, H_MLA * D_V).astype(BF)
    fb = jnp.zeros((1, LANE), F32).at[0, MISC_F: MISC_F + H_FOX].set(fox_bf[l])
    w1 = cmp_w1[l].reshape(2, CMP_BLK, DH, DH)
    wc = jnp.einsum('cldh,ce->lcdeh', w1, jnp.eye(2, dtype=F32)).reshape(CMP_BLK * 2 * DH, 2 * DH).astype(BF)
    w2 = jnp.einsum('che,cf->chfe', cmp_w2[l], jnp.eye(2, dtype=F32)).reshape(2 * DH, 2 * DH).astype(BF)
    cpos = jnp.swapaxes(cmp_pos[l], 0, 1).reshape(1, CMP_BLK * 2 * DH)
    lam_init = 0.8 - 0.6 * math.exp(-0.3 * l)
    return dict(
        wp=wp, wm=wm, wuq=wuq, wukbd=wukbd, wuvbd=wuvbd, fb=fb, wc=wc, w2=w2, cpos=cpos,
        gq=mla_gq[l][None], gkv=mla_gkv[l][None], dl=diff_lam[l],
        lam0=jnp.full((1, 1), lam_init, F32), ds=jnp.full((1, 1), 1.0 - lam_init, F32),
        dg=jnp.tile(diff_g[l], H_DIFF)[None], wbr=w_branch[l].astype(BF), wo=w_out[l].astype(BF),
        lg=ln_g[l][None], lb=ln_b[l][None])


def _bucket_table(dist):
    n = np.maximum(dist, 0)
    exact = N_BUCKETS // 2
    nf = np.maximum(n, 1).astype(np.float32)
    ratio = np.log(nf / np.float32(exact)) / np.float32(math.log(MAX_DIST / exact))
    large = exact + (ratio * np.float32(N_BUCKETS - exact)).astype(np.int32)
    return np.where(n < exact, n, np.minimum(large, N_BUCKETS - 1)).astype(np.int32)


def _bias_lookup(tbl, dist):
    onehot = (jnp.asarray(_bucket_table(dist))[..., None] == jnp.arange(N_BUCKETS)).astype(F32)
    return jnp.einsum('...b,bh->...h', onehot, tbl, precision=lax.Precision.HIGHEST)


def _prompt_bias_tables(rel_bias, S):
    nq = S // TQ
    i = np.arange(TQ)[:, None]
    j = np.arange(TK)[None, :]
    tiles = jnp.stack([_bias_lookup(rel_bias, d * TQ + i - j) for d in range(3)])
    tb = jnp.moveaxis(tiles, -1, 0)
    n_cmp = S // CMP_BLK
    ends = (np.arange(n_cmp) + 1) * CMP_BLK - 1
    dist = np.arange(S)[:, None] - ends[None, :]
    cb = _bias_lookup(rel_bias[:, :H_NSA], dist)
    cb = cb.reshape(nq, TQ, n_cmp // 2, 2, H_NSA)
    cb = jnp.transpose(cb, (0, 3, 4, 1, 2)).reshape(nq, 2, H_NSA * TQ, n_cmp // 2)
    return tb[:H_NSA], tb[H_NSA:], cb


def _prompt_layer(x, lw, tabs, tb_a, tb_d, cb, u_tri):
    B, S, _ = x.shape
    M = B * S
    nq = S // TQ
    W = _key_tiles(S)
    kw = W * TK
    x2 = x.reshape(M, D_MODEL)
    (qa, cmp_s, sel_s, win_s, selb, winb, misc, lat, kcat, qcat, fq, fkv, fkvb, dq, dkv, dkvb, z) = _proj(
        x2, lw['wp'], lw['wuq'], lw['wukbd'], lw['gq'], lw['gkv'], lw['fb'], tabs, 256)

    n_cmp = S // CMP_BLK
    kvc = _cmp_summary(cmp_s.reshape(1, M // CMP_BLK, CMP_BLK * 2 * DH), 0, lw['cpos'], lw['wc'], lw['w2'],
                       min(256, M // CMP_BLK))
    kvc = jnp.swapaxes(kvc.reshape(B, n_cmp // 2, 2, LANE), 1, 2)

    qrow = lambda w: pl.BlockSpec((TQ, w), lambda b, i: (b * nq + i, 0))
    kvrow = lambda w: pl.BlockSpec((S, w), lambda b, i: (b, 0))
    whole = lambda a: pl.BlockSpec(a.shape, lambda b, i: (0,) * a.ndim)

    onsa = _attn_call(
        functools.partial(_nsa_kernel, W), B, S, (qa, kvc, selb, winb, cb, tb_a),
        [qrow(512), pl.BlockSpec((None, 2, n_cmp // 2, LANE), lambda b, i: (b, 0, 0, 0)), kvrow(LANE), kvrow(LANE),
         pl.BlockSpec((None, 2, H_NSA * TQ, n_cmp // 2), lambda b, i: (i, 0, 0, 0)), whole(tb_a)],
        3 * BR_W, _state_scratch(H_NSA * TQ, LANE, LANE) + [pltpu.VMEM((S // kw, TQ, kw), F32)], "nsa_prompt")

    olat = _attn_call(functools.partial(_mla_kernel, W), B, S, (qcat, kcat), [qrow(H_MLA * (D_C + LANE)), kvrow(D_C + LANE)],
                      H_MLA * D_C, _state_scratch(H_MLA * TQ, D_C + LANE, D_C), "mla_prompt")

    logf = misc[:, MISC_F: MISC_F + H_FOX].reshape(B, S, H_FOX)
    pre = _prefix(jnp.swapaxes(logf, 1, 2).reshape(B * H_FOX, S), u_tri).reshape(B, H_FOX, S)
    pq = jnp.swapaxes(pre, 1, 2).reshape(M, H_FOX)
    pk = jnp.swapaxes(pre.reshape(B, H_FOX, S // kw, kw), 1, 2)
    pk = jnp.pad(pk, ((0, 0), (0, 0), (0, 8 - H_FOX), (0, 0)))
    ofox = _attn_call(
        functools.partial(_fox_kernel, W), B, S, (fq, fkvb, pq, pk),
        [qrow(512), kvrow(256), qrow(H_FOX), pl.BlockSpec((None, S // kw, 8, kw), lambda b, i: (b, 0, 0, 0))],
        BR_W, _state_scratch(H_FOX * TQ, LANE, LANE), "fox_prompt")

    odiff = _attn_call(
        functools.partial(_diff_kernel, W), B, S, (dq, dkvb, tb_d, lw['dl'], lw['lam0']),
        [qrow(1024), kvrow(256), whole(tb_d), whole(lw['dl']), whole(lw['lam0'])],
        BR_W, _state_scratch(2 * H_DIFF * TQ, LANE, LANE), "diff_prompt")

    x_new = _post(x2, olat, onsa, ofox, odiff, misc, z, lw['wuvbd'], lw['dg'], lw['ds'], lw['wbr'], lw['wm'],
                  lw['wo'], lw['lg'], lw['lb'], 256).reshape(B, S, D_MODEL)
    wlen = min(WINDOW, S)
    states = (cmp_s.reshape(B, S, 2, DH), sel_s.reshape(B, S, 2, DH),
              win_s.reshape(B, S, 2, DH)[:, S - wlen:],
              lat.reshape(B, S, D_C), misc[:, MISC_KPE: MISC_KPE + D_ROPE].reshape(B, S, D_ROPE),
              fkv.reshape(B, S, 2, KV_FOX, DH), logf, dkv.reshape(B, S, 2, KV_DIFF, 2 * DD))
    return x_new, states


PG = 8


def _suffix_kernel(x_ref, ugt_ref, one_ref, sfx_ref, tot_ref):
    x = x_ref[...]
    sfx_ref[...] = _dot3(x, ugt_ref[...])
    tot_ref[...] = _dot3(x, one_ref[...])


def _page_suffix(x, tr):
    R = x.shape[0]
    ugt = (jnp.arange(LANE)[:, None] > jnp.arange(LANE)[None, :]).astype(BF)
    one = jnp.ones((LANE, LANE), BF)
    return pl.pallas_call(
        _suffix_kernel,
        grid=(R // tr,),
        in_specs=[pl.BlockSpec((tr, LANE), lambda i: (i, 0)), pl.BlockSpec((LANE, LANE), lambda i: (0, 0)),
                  pl.BlockSpec((LANE, LANE), lambda i: (0, 0))],
        out_specs=[pl.BlockSpec((tr, LANE), lambda i: (i, 0))] * 2,
        out_shape=[jax.ShapeDtypeStruct((R, LANE), F32)] * 2,
        compiler_params=pltpu.CompilerParams(dimension_semantics=("arbitrary",)),
        name="fox_page_suffix",
    )(x, ugt, one)


def _online_multi(tiles, m_ref, l_ref, acc_ref):
    m_prev = m_ref[...]
    m_new = m_prev
    for s, _, _ in tiles:
        m_new = jnp.maximum(m_new, jnp.max(s, -1, keepdims=True))
    alpha = jnp.exp(m_prev - m_new)
    l_new = alpha * l_ref[...]
    acc = alpha * acc_ref[...]
    for s, v, v_t in tiles:
        p = jnp.exp(s - m_new)
        l_new = l_new + jnp.sum(p, -1, keepdims=True)
        pb = p.astype(BF)
        acc = acc + (_qk(pb, v) if v_t else jnp.dot(pb, v, preferred_element_type=F32))
    l_ref[...] = l_new
    acc_ref[...] = acc
    m_ref[...] = m_new


def _nsa_sample_kernel(dq, n_win, q_ref, kvc_ref, cb_ref, win_ref, wt_ref, wb1_ref, wb2_ref, o_ref, sm_ref):
    q = q_ref[...]
    rows = q.shape[0]
    n_pair = kvc_ref.shape[1]
    ke = kvc_ref[0].astype(BF)
    ko = kvc_ref[1].astype(BF)
    se = _qk(q, ke) + cb_ref[0]
    so = _qk(q, ko) + cb_ref[1]
    mx = jnp.maximum(jnp.max(se, -1, keepdims=True), jnp.max(so, -1, keepdims=True))
    ee = jnp.exp(se - mx)
    eo = jnp.exp(so - mx)
    den = jnp.maximum(jnp.sum(ee, -1, keepdims=True) + jnp.sum(eo, -1, keepdims=True), 1e-30)
    pe = ee / den
    po = eo / den
    o_ref[:, :LANE] = (jnp.dot(pe.astype(BF), ke, preferred_element_type=F32)
                       + jnp.dot(po.astype(BF), ko, preferred_element_type=F32))
    pp = pe + po
    a = pp[0:8] + pp[8:16]
    imp = a + pltpu.roll(a, dq, 0)
    bi = lax.broadcasted_iota(jnp.int32, (8, n_pair), 1)
    forced = (bi == 0) | (n_pair - bi < N_LOCAL)
    score = jnp.where(forced, FORCE, imp)
    rank = jnp.where(forced, 0.0, 1.0)
    for i in range(n_pair):
        col = score[:, i: i + 1]
        ahead = (col > score) | ((col == score) & (bi > i))
        rank = rank + jnp.where(ahead, 1.0, 0.0)
    madd = jnp.where(rank < TOP_N, 0.0, NEG_INF)
    sm_ref[...] = jnp.concatenate([madd] * (rows // 8), 0)

    wp = win_ref[...].astype(BF)
    wt = wt_ref[...]
    qi1 = lax.broadcasted_iota(jnp.int32, (rows, n_win), 0) & (dq - 1)
    t1 = lax.broadcasted_iota(jnp.int32, (rows, n_win), 1)
    ok1 = n_win + qi1 - t1 <= WINDOW
    qi2 = lax.broadcasted_iota(jnp.int32, (rows, LANE), 0) & (dq - 1)
    ok2 = lax.broadcasted_iota(jnp.int32, (rows, LANE), 1) <= qi2
    s1 = jnp.where(ok1, jnp.dot(q, wp, preferred_element_type=F32) + wb1_ref[...], NEG_INF)
    s2 = jnp.where(ok2, _qk(q, wt) + wb2_ref[...], NEG_INF)
    mx = jnp.maximum(jnp.max(s1, -1, keepdims=True), jnp.max(s2, -1, keepdims=True))
    e1 = jnp.where(ok1, jnp.exp(s1 - mx), 0.0)
    e2 = jnp.where(ok2, jnp.exp(s2 - mx), 0.0)
    den = jnp.maximum(jnp.sum(e1, -1, keepdims=True) + jnp.sum(e2, -1, keepdims=True), 1e-30)
    o_ref[:, LANE:] = (_qk((e1 / den).astype(BF), wp)
                       + jnp.dot((e2 / den).astype(BF), wt, preferred_element_type=F32))


def _stream_kernel(dq, nsteps, pt_ref, qml_ref, qmp_ref, qf_ref, qd_ref, qs_ref, sm_ref, *rest):
    npg = 7 * PG
    pages = rest[:npg]
    (tl_ref, tf_ref, td_ref, ts_ref, tlf_ref, dbl_ref, dbt_ref, dbc_ref, sbl_ref, sbt_ref, sbc_ref,
     dl_ref, lam0_ref) = rest[npg: npg + 13]
    om_ref, of_ref, od_ref, os_ref = rest[npg + 13: npg + 17]
    (mm, lm, am, mf, lf, af, md, ld, ad, ms, ls, as_, carry) = rest[npg + 17:]
    t = pl.program_id(1)

    @pl.when(t == 0)
    def _():
        _init_state(mm, lm, am)
        _init_state(mf, lf, af)
        _init_state(md, ld, ad)
        _init_state(ms, ls, as_)
        carry[...] = jnp.zeros(carry.shape, F32)

    qml, qmp, qf, qd, qs = qml_ref[...], qmp_ref[...], qf_ref[...], qd_ref[...], qs_ref[...]
    tlf = tlf_ref[...]
    cs = [tlf[:, 0:1]]
    for j in range(1, dq):
        cs.append(cs[-1] + tlf[:, j: j + 1])
    qrow = lax.broadcasted_iota(jnp.int32, (qf.shape[0], 1), 0) & (dq - 1)
    cq = cs[dq - 1]
    for j in range(dq - 2, -1, -1):
        cq = jnp.where(qrow == j, cs[j], cq)

    lane = lax.broadcasted_iota(jnp.int32, (qs.shape[0], LANE), 1)
    sm = sm_ref[...]
    first = t == 0
    mla_t, fox_t, diff_t, sel_t = [], [], [], []
    run = carry[...]
    for g in range(PG):
        lat_r, kpe_r, fox_r, sfx_r, tot_r, diff_r, sel_r = pages[7 * g: 7 * g + 7]
        lat = lat_r[...].astype(BF)
        kpe = kpe_r[...].astype(BF)
        s = _qk(qml, lat) + jnp.dot(qmp[:, :D_ROPE], kpe, preferred_element_type=F32)
        mla_t.append((s, lat, False))
        fkv = fox_r[...].astype(BF)
        s = jnp.dot(qf, fkv[:LANE], preferred_element_type=F32) + (cq + run + sfx_r[...])
        fox_t.append((s, fkv[LANE:], True))
        run = run + tot_r[:, 0:1]
        dkv = diff_r[...].astype(BF)
        dbias = jnp.where(first, dbl_ref[...], dbc_ref[...]) if g == 0 else dbc_ref[...]
        diff_t.append((jnp.dot(qd, dkv[:LANE], preferred_element_type=F32) + dbias, dkv[LANE:], True))
        skv = sel_r[...].astype(BF)
        sbias = jnp.where(first, sbl_ref[...], sbc_ref[...]) if g == 0 else sbc_ref[...]
        madd = jnp.where(lane < SEL_BLK, sm[:, 2 * g: 2 * g + 1], sm[:, 2 * g + 1: 2 * g + 2])
        sel_t.append((jnp.dot(qs, skv, preferred_element_type=F32) + sbias + madd, skv, True))
    carry[...] = run
    _online_multi(mla_t, mm, lm, am)
    _online_multi(fox_t, mf, lf, af)
    _online_multi(diff_t, md, ld, ad)
    _online_multi(sel_t, ms, ls, as_)

    @pl.when(t == nsteps - 1)
    def _():
        def causal(rows):
            qi = lax.broadcasted_iota(jnp.int32, (rows, LANE), 0) & (dq - 1)
            return lax.broadcasted_iota(jnp.int32, (rows, LANE), 1) <= qi

        tl, tf, td, ts = tl_ref[...], tf_ref[...], td_ref[...], ts_ref[...]
        s = _qk(qml, tl[:, :D_C]) + _qk(qmp, tl[:, D_C:])
        _online_multi([(jnp.where(causal(qml.shape[0]), s, NEG_INF), tl[:, :D_C], False)], mm, lm, am)
        ctile = jnp.zeros((qf.shape[0], LANE), F32)
        for j in range(dq):
            ctile = jnp.where(lane == j, cs[j], ctile)
        s = _qk(qf, tf[:, :LANE]) + (cq - ctile)
        _online_multi([(jnp.where(causal(qf.shape[0]), s, NEG_INF), tf[:, LANE:], False)], mf, lf, af)
        s = _qk(qd, td[:, :LANE]) + dbt_ref[...]
        _online_multi([(jnp.where(causal(qd.shape[0]), s, NEG_INF), td[:, LANE:], False)], md, ld, ad)
        s = _qk(qs, ts) + sbt_ref[...]
        _online_multi([(jnp.where(causal(qs.shape[0]), s, NEG_INF), ts, False)], ms, ls, as_)
        om_ref[...] = am[...] / lm[...]
        of_ref[...] = af[...] / lf[...]
        os_ref[...] = as_[...] / ls[...]
        dl = dl_ref[...]
        lam = (jnp.exp(jnp.sum(dl[0:1] * dl[1:2], -1, keepdims=True))
               - jnp.exp(jnp.sum(dl[2:3] * dl[3:4], -1, keepdims=True)) + lam0_ref[...])
        od = ad[...] / ld[...]
        half = G_DIFF * dq
        for kvh in range(KV_DIFF):
            o0 = od[2 * half * kvh: 2 * half * kvh + half]
            o1 = od[2 * half * kvh + half: 2 * half * (kvh + 1)]
            od_ref[half * kvh: half * (kvh + 1), :] = o0 - lam * o1


def _sample_bias_tables(rel_bias, past_len, dq, n_win):
    tbl_a, tbl_d = rel_bias[:, :H_NSA], rel_bias[:, H_NSA:]
    q = np.arange(dq)[:, None]

    def rows_hq(tbl, dist):
        b = _bias_lookup(tbl, dist)
        return jnp.transpose(b, (2, 0, 1)).reshape(tbl.shape[1] * dq, dist.shape[1])

    n_cmp = past_len // CMP_BLK
    ends = (np.arange(n_cmp) + 1) * CMP_BLK - 1
    cb = rows_hq(tbl_a, past_len + q - ends[None, :]).reshape(H_NSA * dq, n_cmp // 2, 2)
    cb = jnp.moveaxis(cb, -1, 0)
    key = np.arange(LANE)[None, :]
    wb1 = rows_hq(tbl_a, n_win + q - np.arange(n_win)[None, :])
    near = rows_hq(tbl_a, PAGE_SIZE + q - key)
    tail = rows_hq(tbl_a, q - key)
    far = rows_hq(tbl_a, 2 * PAGE_SIZE + q - key)[:, :1]

    def diff_rows(x):
        x = x.reshape(KV_DIFF, 1, G_DIFF, dq, x.shape[-1])
        return jnp.broadcast_to(x, (KV_DIFF, 2, G_DIFF, dq, x.shape[-1])).reshape(2 * H_DIFF * dq, x.shape[-1])

    dnear = diff_rows(rows_hq(tbl_d, PAGE_SIZE + q - key))
    dtail = diff_rows(rows_hq(tbl_d, q - key))
    dfar = diff_rows(rows_hq(tbl_d, 2 * PAGE_SIZE + q - key)[:, :1])
    return cb, wb1, near, tail, far, dnear, dtail, dfar


def _rows_hq(a, DB, dq, H):
    W = a.shape[1] // H
    return jnp.transpose(a.reshape(DB, dq, H, W), (0, 2, 1, 3)).reshape(DB, H * dq, W)


def _rows_token(a, DB, dq, H):
    W = a.shape[-1]
    return jnp.transpose(a.reshape(DB, H, dq, W), (0, 2, 1, 3)).reshape(DB * dq, H * W)


def _pad_tail(a, DB, dq):
    a = a.reshape(DB, dq, a.shape[-1])
    return jnp.pad(a, ((0, 0), (0, LANE - dq), (0, 0)))


def _sample_layer(x, l, lw, tabs, bias, page_table, caches, sfx, tot):
    (c_cmp, c_sel, st_win, win_prev, c_lat, c_kpe, c_fox, c_diff) = caches
    cb, wb1, near, tail, far, dnear, dtail, dfar = bias
    DB, dq, _ = x.shape
    M = DB * dq
    NP = page_table.shape[1]
    n_pool = c_lat.shape[1]
    n_win = st_win.shape[3]
    assert dq & (dq - 1) == 0 and H_NSA * dq == 16 and NP % PG == 0
    x2 = x.reshape(M, D_MODEL)
    (qa, cmp_s, sel_s, win_s, selb, winb, misc, lat, kcat, qcat, fq, fkv, fkvb, dq_, dkv, dkvb, z) = _proj(
        x2, lw['wp'], lw['wuq'], lw['wukbd'], lw['gq'], lw['gkv'], lw['fb'], tabs, M)

    kvc = _cmp_paged_summary(c_cmp, l, page_table, lw['cpos'], lw['wc'], lw['w2'],
                             CMP_PAGES if NP % CMP_PAGES == 0 else NP)
    n_cmp = NP * (PAGE_SIZE // CMP_BLK)
    kvc = jnp.swapaxes(kvc.reshape(DB, n_cmp // 2, 2, LANE), 1, 2)

    q_nsa = _rows_hq(qa, DB, dq, H_NSA)
    per_b = lambda *blk: pl.BlockSpec((None,) + blk, lambda b: (b,) + (0,) * len(blk))
    cst = lambda a: pl.BlockSpec(a.shape, lambda b: (0,) * a.ndim)
    wt = _pad_tail(winb, DB, dq)
    ocw, selmask = pl.pallas_call(
        lambda *r: _nsa_sample_kernel(dq, n_win, *r),
        grid=(DB,),
        in_specs=[per_b(16, LANE), per_b(2, n_cmp // 2, LANE), cst(cb),
                  pl.BlockSpec((None, None, LANE, n_win), lambda b: (l, b, 0, 0)), per_b(LANE, LANE),
                  cst(wb1), cst(tail)],
        out_specs=[per_b(16, 2 * LANE), per_b(16, n_cmp // 2)],
        out_shape=[jax.ShapeDtypeStruct((DB, 16, 2 * LANE), F32), jax.ShapeDtypeStruct((DB, 16, n_cmp // 2), F32)],
        compiler_params=pltpu.CompilerParams(dimension_semantics=("arbitrary",), vmem_limit_bytes=VMEM_LIMIT),
        name="nsa_sample",
    )(q_nsa, kvc, cb, st_win, wt, wb1, tail)

    nsteps = NP // PG
    sm = selmask.reshape(DB, 16, NP, 2)[:, :, ::-1]
    sm = jnp.transpose(sm.reshape(DB, 16, nsteps, PG * 2), (0, 2, 1, 3))

    qm = _rows_hq(qcat, DB, dq, H_MLA)
    qml, qmp = qm[..., :D_C], qm[..., D_C:]
    qf = _rows_hq(fq, DB, dq, H_FOX)
    qd = _rows_hq(dq_, DB, dq, 2 * H_DIFF)
    logf_new = misc[:, MISC_F: MISC_F + H_FOX].reshape(DB, dq, H_FOX)
    tlf = jnp.broadcast_to(jnp.swapaxes(logf_new, 1, 2)[:, :, None, :], (DB, H_FOX, dq, dq)).reshape(DB, 16, dq)
    tlf = jnp.pad(tlf, ((0, 0), (0, 0), (0, LANE - dq)))

    def page(rows, cols):
        return [pl.BlockSpec((None, None, rows, cols),
                             lambda b, t, pt, g=g: (l, pt[b, NP - 1 - (t * PG + g)], 0, 0)) for g in range(PG)]

    lat_s, kpe_s, fox_s, sfx_s, tot_s, diff_s, sel_s_ = (
        page(PAGE_SIZE, D_C), page(D_ROPE, PAGE_SIZE), page(256, PAGE_SIZE), page(16, LANE), page(16, LANE),
        page(256, PAGE_SIZE), page(LANE, PAGE_SIZE))
    page_specs, page_args = [], []
    for g in range(PG):
        page_specs += [lat_s[g], kpe_s[g], fox_s[g], sfx_s[g], tot_s[g], diff_s[g], sel_s_[g]]
        page_args += [c_lat, c_kpe, c_fox, sfx, tot, c_diff, c_sel]
    pb = lambda *blk: pl.BlockSpec((None,) + blk, lambda b, t, pt: (b,) + (0,) * len(blk))
    cs = lambda a: pl.BlockSpec(a.shape, lambda b, t, pt: (0,) * a.ndim)
    tails = (_pad_tail(kcat, DB, dq), _pad_tail(fkvb, DB, dq), _pad_tail(dkvb, DB, dq), _pad_tail(selb, DB, dq))
    consts = (dnear, dtail, dfar, near, tail, far, lw['dl'], lw['lam0'])
    st = lambda rows, dv: [pltpu.VMEM((rows, 1), F32), pltpu.VMEM((rows, 1), F32), pltpu.VMEM((rows, dv), F32)]
    om, of, od, os_ = pl.pallas_call(
        lambda *r: _stream_kernel(dq, nsteps, *r),
        grid_spec=pltpu.PrefetchScalarGridSpec(
            num_scalar_prefetch=1, grid=(DB, nsteps),
            in_specs=[pb(16, D_C), pb(16, LANE), pb(16, LANE), pb(32, LANE), pb(16, LANE),
                      pl.BlockSpec((None, None, 16, 2 * PG), lambda b, t, pt: (b, t, 0, 0))]
            + page_specs
            + [pb(LANE, D_C + LANE), pb(LANE, 256), pb(LANE, 256), pb(LANE, LANE), pb(16, LANE)]
            + [cs(a) for a in consts],
            out_specs=[pb(16, D_C), pb(16, LANE), pb(16, LANE), pb(16, LANE)],
            scratch_shapes=st(16, D_C) + st(16, LANE) + st(32, LANE) + st(16, LANE) + [pltpu.VMEM((16, 1), F32)]),
        out_shape=[jax.ShapeDtypeStruct((DB, 16, D_C), F32), jax.ShapeDtypeStruct((DB, 16, LANE), F32),
                   jax.ShapeDtypeStruct((DB, 16, LANE), F32), jax.ShapeDtypeStruct((DB, 16, LANE), F32)],
        compiler_params=pltpu.CompilerParams(dimension_semantics=("arbitrary", "arbitrary"),
                                             vmem_limit_bytes=VMEM_LIMIT),
        name="sample_stream",
    )(page_table, qml, qmp, qf, qd, q_nsa, sm, *page_args, *tails, tlf, *consts)

    olat = _rows_token(om, DB, dq, H_MLA)

    def pick_kv(o, H, G):
        o = o.reshape(DB, H, dq, 2, DH)
        o = jnp.stack([o[:, h, :, h // G] for h in range(H)], 1)
        return _rows_token(o.reshape(DB, H * dq, DH), DB, dq, H)

    ofox = pick_kv(of, H_FOX, G_FOX)
    odiff = pick_kv(od, H_DIFF, G_DIFF)
    upper = lambda o: _rows_token(o[..., DH:], DB, dq, H_NSA)
    onsa = jnp.concatenate([upper(ocw[..., :LANE]), upper(os_), upper(ocw[..., LANE:])], -1)
    x_new = _post(x2, olat, onsa, ofox, odiff, misc, z, lw['wuvbd'], lw['dg'], lw['ds'], lw['wbr'], lw['wm'],
                  lw['wo'], lw['lg'], lw['lb'], M).reshape(DB, dq, D_MODEL)
    win_all = jnp.concatenate([win_prev[l], win_s.reshape(DB, dq, LANE)], 1)
    wlen = min(WINDOW, win_all.shape[1])
    states = (cmp_s.reshape(DB, dq, 2, DH), sel_s.reshape(DB, dq, 2, DH),
              win_all[:, win_all.shape[1] - wlen:].reshape(DB, wlen, 2, DH),
              lat.reshape(DB, dq, D_C), misc[:, MISC_KPE: MISC_KPE + D_ROPE].reshape(DB, dq, D_ROPE),
              fkv.reshape(DB, dq, 2, KV_FOX, DH), logf_new, dkv.reshape(DB, dq, 2, KV_DIFF, 2 * DD))
    return x_new, states


def kernel(x_prompt, x_sample, cache_nsa_cmp_kv, cache_nsa_sel_kv, state_nsa_win_kv, cache_mla_latent,
           cache_mla_kpe, cache_fox_kv, cache_fox_logf, cache_diff_kv, page_table, rel_bias, w_in,
           nsa_cmp_pos, nsa_cmp_w1, nsa_cmp_w2, mla_q_norm, mla_kv_norm, mla_w_uq, mla_w_uk, mla_w_uv,
           fox_forget_bias, diff_lambda, diff_subln, w_branch, w_out, ln_g, ln_b):
    past_len = page_table.shape[1] * PAGE_SIZE
    S = x_prompt.shape[1]
    DB, dq = x_sample.shape[:2]
    depth, n_pool = cache_mla_latent.shape[:2]
    tabs_s = _rope_tables(jnp.tile(past_len + jnp.arange(dq), DB))
    bias_s = _sample_bias_tables(rel_bias, past_len, dq, state_nsa_win_kv.shape[2])
    lf = jnp.swapaxes(cache_fox_logf, 2, 3)
    lf = jnp.broadcast_to(lf[:, :, :, None, :], (depth, n_pool, H_FOX, dq, PAGE_SIZE))
    n_rows = depth * n_pool * H_FOX * dq
    sfx, tot = _page_suffix(lf.reshape(n_rows, PAGE_SIZE), 2048 if n_rows % 2048 == 0 else n_rows)
    sfx = sfx.reshape(depth, n_pool, H_FOX * dq, PAGE_SIZE)
    tot = tot.reshape(depth, n_pool, H_FOX * dq, PAGE_SIZE)
    fm = lambda a, feat: jnp.swapaxes(a.reshape(a.shape[0], a.shape[1], a.shape[2], feat), 2, 3)
    win_prev = state_nsa_win_kv.reshape(depth, DB, state_nsa_win_kv.shape[2], 2 * DH)
    caches = (fm(cache_nsa_cmp_kv, 2 * DH), fm(cache_nsa_sel_kv, 2 * DH), fm(state_nsa_win_kv, 2 * DH), win_prev,
              cache_mla_latent, fm(cache_mla_kpe, D_ROPE),
              fm(cache_fox_kv, 2 * KV_FOX * DH), fm(cache_diff_kv, 2 * KV_DIFF * 2 * DD))
    tabs_p = _rope_tables(jnp.arange(S))
    tb_a, tb_d, cb = _prompt_bias_tables(rel_bias, S)
    u_tri = (jnp.arange(LANE)[:, None] <= jnp.arange(LANE)[None, :]).astype(BF)
    xp, xs = x_prompt, x_sample
    st_p, st_s = [], []
    for l in range(depth):
        lwp = _layer_weights(l, w_in, nsa_cmp_pos, nsa_cmp_w1, nsa_cmp_w2, mla_q_norm, mla_kv_norm, mla_w_uq,
                             mla_w_uk, mla_w_uv, fox_forget_bias, diff_lambda, diff_subln, w_branch, w_out,
                             ln_g, ln_b)
        xp, sp = _prompt_layer(xp, lwp, tabs_p, tb_a, tb_d, cb, u_tri)
        xs, ss = _sample_layer(xs, l, lwp, tabs_s, bias_s, page_table, caches, sfx, tot)
        st_p.append(sp)
        st_s.append(ss)
    (p_cmp, p_sel, p_win, p_lat, p_kpe, p_fkv, p_flf, p_dkv) = [jnp.stack(z) for z in zip(*st_p)]
    (s_cmp, s_sel, s_win, s_lat, s_kpe, s_fkv, s_flf, s_dkv) = [jnp.stack(z) for z in zip(*st_s)]
    return (xp, xs, p_cmp, s_cmp, p_sel, s_sel, p_win, s_win, p_lat, s_lat, p_kpe, s_kpe,
            p_fkv, s_fkv, p_flf, s_flf, p_dkv, s_dkv)
```

```python
import functools
import math
import numpy as np
import jax
import jax.numpy as jnp
from jax import lax
from jax.experimental import pallas as pl
from jax.experimental.pallas import tpu as pltpu


D_MODEL = 1024
DEPTH = 4
PAGE_SIZE = 128

N_BRANCH = 4
BR_W = D_MODEL // N_BRANCH
DH = 64
QBLK = 128
H_NSA = 4
CMP_BLK = 32
SEL_BLK = 64
CMP_PER_SEL = SEL_BLK // CMP_BLK
TOP_N = 16
N_LOCAL = 2
WINDOW = 512
H_MLA = 4
D_CQ = 256
D_C = 256
D_NOPE = 64
D_ROPE = 32
D_V = 64
ROPE_BASE = 10000.0
MLA_SCALE = (D_NOPE + D_ROPE) ** -0.5
H_FOX = 4
KV_FOX = 2
G_FOX = H_FOX // KV_FOX
H_DIFF = 4
KV_DIFF = 2
G_DIFF = H_DIFF // KV_DIFF
DD = 32
N_BUCKETS = 32
MAX_DIST = 128
DEEPNORM_ALPHA = (2 * DEPTH) ** 0.25
NEG_INF = -1e30
FORCE = 1e9

IN_SIZES = (
    H_NSA * DH, 2 * DH, 2 * DH, 2 * DH, 3 * H_NSA, BR_W,
    D_CQ, D_C, D_ROPE, BR_W,
    H_FOX * DH, KV_FOX * DH, KV_FOX * DH, H_FOX, BR_W,
    H_DIFF * 2 * DD, KV_DIFF * 2 * DD, KV_DIFF * 2 * DD, BR_W,
    N_BRANCH * D_MODEL,
)
N_IN = sum(IN_SIZES)


def _mm_kernel(x_ref, w_ref, o_ref):
    o_ref[...] = jnp.dot(x_ref[...].astype(jnp.bfloat16), w_ref[...].astype(jnp.bfloat16),
                         preferred_element_type=jnp.float32)


def _pmatmul(x, w):
    lead = x.shape[:-1]
    K = x.shape[-1]
    N = w.shape[-1]
    x2 = x.reshape(-1, K)
    M = x2.shape[0]
    tn = 256
    n_pad = -(-N // tn) * tn
    if n_pad != N:
        w = jnp.pad(w, ((0, 0), (0, n_pad - N)))
    tm = min(M, 1024)
    out = pl.pallas_call(
        _mm_kernel,
        grid=(M // tm, n_pad // tn),
        in_specs=[pl.BlockSpec((tm, K), lambda i, j: (i, 0)),
                  pl.BlockSpec((K, tn), lambda i, j: (0, j))],
        out_specs=pl.BlockSpec((tm, tn), lambda i, j: (i, j)),
        out_shape=jax.ShapeDtypeStruct((M, n_pad), jnp.float32),
        name="matmul",
    )(x2, w)
    return out[:, :N].reshape(*lead, N)


def _split(h, sizes):
    out, start = [], 0
    for n in sizes:
        out.append(h[..., start:start + n])
        start += n
    return out


def _rmsnorm(x, g, eps=1e-6):
    xf = x.astype(jnp.float32)
    y = xf * lax.rsqrt(jnp.mean(xf * xf, -1, keepdims=True) + eps)
    return (y * g).astype(x.dtype)


def _layernorm(x, g, b, eps=1e-5):
    xf = x.astype(jnp.float32)
    mu = jnp.mean(xf, -1, keepdims=True)
    var = jnp.mean(jnp.square(xf - mu), -1, keepdims=True)
    return ((xf - mu) * lax.rsqrt(var + eps) * g + b).astype(x.dtype)


def _masked_softmax(s, mask):
    s = jnp.where(mask, s.astype(jnp.float32), NEG_INF)
    e = jnp.where(mask, jnp.exp(s - jnp.max(s, -1, keepdims=True)), 0.0)
    return e / jnp.maximum(jnp.sum(e, -1, keepdims=True), 1e-30)


def _t5_bucket(dist):
    n = jnp.maximum(dist, 0)
    exact = N_BUCKETS // 2
    nf = jnp.maximum(n, 1).astype(jnp.float32)
    large = exact + (jnp.log(nf / exact) / math.log(MAX_DIST / exact) * (N_BUCKETS - exact)).astype(jnp.int32)
    return jnp.where(n < exact, n, jnp.minimum(large, N_BUCKETS - 1))


def _rope(x, pos):
    half = D_ROPE // 2
    inv = ROPE_BASE ** (-jnp.arange(half, dtype=jnp.float32) / half)
    ang = pos.astype(jnp.float32)[:, None] * inv
    ang = ang.reshape(ang.shape[0], *([1] * (x.ndim - 3)), half)
    cos, sin = jnp.cos(ang), jnp.sin(ang)
    x1 = x[..., :half].astype(jnp.float32)
    x2 = x[..., half:].astype(jnp.float32)
    return jnp.concatenate([x1 * cos - x2 * sin, x2 * cos + x1 * sin], -1).astype(x.dtype)


def _sweep(fn, q_pos, *q_arrays):
    sq = q_pos.shape[0]
    if sq > QBLK and sq % QBLK == 0:
        nb = sq // QBLK
        blocks = tuple(jnp.moveaxis(a.reshape(a.shape[0], nb, QBLK, *a.shape[2:]), 1, 0) for a in q_arrays)
        out = lax.map(lambda args: fn(args[0], *args[1]), (q_pos.reshape(nb, QBLK), blocks))
        out = jnp.moveaxis(out, 0, 1)
        return out.reshape(out.shape[0], sq, *out.shape[3:])
    return fn(q_pos, *q_arrays)


def _block_rows(rows):
    B, T = rows.shape[:2]
    n_blk = -(-T // SEL_BLK)
    pad = ((0, 0), (0, n_blk * SEL_BLK - T)) + ((0, 0),) * (rows.ndim - 2)
    return jnp.pad(rows, pad).reshape(B, n_blk, SEL_BLK, *rows.shape[2:])


def _take_blocks(blocks, idx):
    return jax.vmap(lambda blk, i: blk[i])(blocks, idx)


def _local_block_gather(rows):
    blocks = _block_rows(rows)
    return lambda idx: _take_blocks(blocks, idx)


def _paged_block_gather(pool, page_table, new_rows):
    B = new_rows.shape[0]
    bpp = PAGE_SIZE // SEL_BLK
    n_past = page_table.shape[1] * bpp
    pool_blocks = pool.reshape(-1, SEL_BLK, *pool.shape[2:])
    tail = _block_rows(new_rows)
    n_tail = tail.shape[1]

    def gather(idx):
        ip = jnp.clip(idx, 0, n_past - 1)
        page = jnp.take_along_axis(page_table, (ip // bpp).reshape(B, -1), axis=1).reshape(ip.shape)
        from_pool = pool_blocks[page * bpp + ip % bpp]
        from_tail = _take_blocks(tail, jnp.clip(idx - n_past, 0, n_tail - 1))
        return jnp.where((idx < n_past)[..., None, None, None], from_pool, from_tail)
    return gather


def _nsa_compressed(q, qpos, rows, cmp_pos, cmp_w1, cmp_w2, tbl):
    B, T = rows.shape[:2]
    Q = q.shape[1]
    n_cmp = T // CMP_BLK
    blocks = rows[:, :n_cmp * CMP_BLK].reshape(B, n_cmp, CMP_BLK, 2, DH)
    hid = jnp.einsum('bnlcd,cldh->bnch', blocks + jnp.swapaxes(cmp_pos, 0, 1),
                     cmp_w1.reshape(2, CMP_BLK, DH, DH))
    kv_c = jnp.einsum('bnch,che->bnce', jax.nn.silu(hid), cmp_w2)
    ends = (jnp.arange(n_cmp) + 1) * CMP_BLK - 1
    dist = qpos[:, None] - ends[None, :]
    s = jnp.einsum('bqhd,bnd->bhqn', q, kv_c[:, :, 0]).astype(jnp.float32)
    s = s + jnp.moveaxis(tbl[_t5_bucket(dist)], -1, 0)
    p = _masked_softmax(s, dist >= 0)
    o = jnp.einsum('bhqn,bnd->bqhd', p.astype(q.dtype), kv_c[:, :, 1])
    n_sel = -(-T // SEL_BLK)
    imp = jnp.pad(jnp.sum(p, axis=1), ((0, 0), (0, 0), (0, n_sel * CMP_PER_SEL - n_cmp)))
    imp = imp.reshape(B, Q, n_sel, CMP_PER_SEL).sum(-1)
    blk = jnp.arange(n_sel)[None, :]
    cur = (qpos // SEL_BLK)[:, None]
    valid = blk <= cur
    forced = valid & ((blk == 0) | (cur - blk < N_LOCAL))
    score = jnp.where(valid, jnp.where(forced, FORCE, imp), -FORCE)
    _, idx = lax.top_k(score, min(TOP_N, n_sel))
    return o, idx


def _nsa_selected(qpos, q, idx, gather, tbl):
    kv = gather(idx)
    kpos = idx[..., None] * SEL_BLK + jnp.arange(SEL_BLK)
    dist = qpos[None, :, None, None] - kpos
    s = jnp.einsum('bqhd,bqnld->bqhnl', q, kv[..., 0, :]).astype(jnp.float32)
    s = s + jnp.moveaxis(tbl[_t5_bucket(dist)], -1, 2)
    B, Q, H, n, L = s.shape
    p = _masked_softmax(s.reshape(B, Q, H, n * L), (dist >= 0).reshape(B, Q, 1, n * L))
    return jnp.einsum('bqhnl,bqnld->bqhd', p.reshape(B, Q, H, n, L).astype(q.dtype), kv[..., 1, :])


def _window_attend(q, kv, qpos, kpos, tbl):
    dist = qpos[:, None] - kpos[None, :]
    s = jnp.einsum('bqhd,bkd->bhqk', q, kv[:, :, 0]).astype(jnp.float32)
    s = s + jnp.moveaxis(tbl[_t5_bucket(dist)], -1, 0)
    mask = (dist >= 0) & (dist <= WINDOW) & (kpos >= 0)[None, :]
    p = _masked_softmax(s, mask)
    return jnp.einsum('bhqk,bkd->bqhd', p.astype(q.dtype), kv[:, :, 1])


def _window_banded(q, kv, pos, tbl):
    B, S = q.shape[:2]
    qb = QBLK if S % QBLK == 0 else S
    nb = S // qb
    padded = jnp.pad(kv, ((0, 0), (WINDOW, 0), (0, 0), (0, 0)))
    kidx = jnp.arange(nb)[:, None] * qb + jnp.arange(WINDOW + qb)[None, :]
    kv_b = padded[:, kidx]
    kpos_b = pos[0] + kidx - WINDOW
    q_b = q.reshape(B, nb, qb, H_NSA, DH)
    o = jax.vmap(_window_attend, in_axes=(1, 1, 0, 0, None), out_axes=1)(
        q_b, kv_b, pos.reshape(nb, qb), kpos_b, tbl)
    return o.reshape(B, S, H_NSA, DH)


def _mla_attend(qpos, q_lat, q_pe, lat, kpe, kpos):
    s = (jnp.einsum('bqhc,bkc->bhqk', q_lat, lat) + jnp.einsum('bqhr,bkr->bhqk', q_pe, kpe)).astype(jnp.float32)
    p = _masked_softmax(s, kpos[None, :] <= qpos[:, None])
    return jnp.einsum('bhqk,bkc->bqhc', p.astype(lat.dtype), lat)


def _fox_attend(qpos, q, r_q, kv, r_k, kpos):
    B, Q = q.shape[:2]
    T = kv.shape[1]
    qg = q.reshape(B, Q, KV_FOX, G_FOX, DH)
    s = jnp.einsum('bqkgd,btkd->bkgqt', qg, kv[:, :, 0]).astype(jnp.float32)
    rk = r_k.reshape(B, T, KV_FOX, G_FOX).transpose(0, 2, 3, 1)[:, :, :, None, :]
    rq = r_q.reshape(B, Q, KV_FOX, G_FOX).transpose(0, 2, 3, 1)[..., None]
    p = _masked_softmax(s + rk - rq, kpos[None, :] <= qpos[:, None])
    o = jnp.einsum('bkgqt,btkd->bqkgd', p.astype(kv.dtype), kv[:, :, 1])
    return o.reshape(B, Q, H_FOX, DH)


def _diff_attend(qpos, q, kv, kpos, tbl, lam):
    B, Q = q.shape[:2]
    T = kv.shape[1]
    k = kv[:, :, 0].reshape(B, T, KV_DIFF, 2, DD)
    v = kv[:, :, 1]
    qg = q.reshape(B, Q, KV_DIFF, G_DIFF, 2, DD)
    s = jnp.einsum('bqkgid,btkid->ibkgqt', qg, k).astype(jnp.float32)
    dist = qpos[:, None] - kpos[None, :]
    bias = jnp.moveaxis(tbl[_t5_bucket(dist)], -1, 0).reshape(KV_DIFF, G_DIFF, Q, T)
    p = _masked_softmax(s + bias, dist >= 0)
    a = (p[0] - lam * p[1]).astype(v.dtype)
    o = jnp.einsum('bkgqt,btkv->bqkgv', a, v)
    return o.reshape(B, Q, H_DIFF, 2 * DD)


def _gather_pages(pool, page_table):
    g = pool[page_table]
    return g.reshape(g.shape[0], -1, *g.shape[3:])


def _layer(x, pos, lidx, past, rel_bias, w_in, cmp_pos, cmp_w1, cmp_w2, mla_gq, mla_gkv,
           mla_wuq, mla_wuk, mla_wuv, fox_bf, diff_lam, diff_g, w_branch, w_out, ln_g, ln_b):
    B, S, _ = x.shape
    dt = x.dtype
    (a_q, a_cmp, a_sel, a_win, a_gate, a_z,
     b_cq, b_ckv, b_kpe, b_z,
     c_q, c_k, c_v, c_f, c_z,
     d_q, d_k, d_v, d_z, merge_logit) = _split(_pmatmul(x, w_in), IN_SIZES)
    tbl_a, tbl_d = rel_bias[:, :H_NSA], rel_bias[:, H_NSA:]

    def with_past(name, new):
        return new if past is None else jnp.concatenate([past[name], new], axis=1)

    q_a = a_q.reshape(B, S, H_NSA, DH) * DH ** -0.5
    cmp_new = a_cmp.reshape(B, S, 2, DH)
    sel_new = a_sel.reshape(B, S, 2, DH)
    win_new = a_win.reshape(B, S, 2, DH)
    o_cmp, sel_idx = _nsa_compressed(q_a, pos, with_past('nsa_cmp', cmp_new), cmp_pos, cmp_w1, cmp_w2, tbl_a)
    if past is None:
        gather = _local_block_gather(sel_new)
        win_all = win_new
        o_win = _window_banded(q_a, win_new, pos, tbl_a)
    else:
        gather = _paged_block_gather(past['nsa_sel_pool'], past['page_table'], sel_new)
        win_all = jnp.concatenate([past['nsa_win'], win_new], axis=1)
        kpos_w = pos[0] - past['nsa_win'].shape[1] + jnp.arange(win_all.shape[1])
        o_win = _window_attend(q_a, win_all, pos, kpos_w, tbl_a)
    win_state = win_all[:, win_all.shape[1] - min(WINDOW, win_all.shape[1]):]
    o_sel = _sweep(lambda qp, qb, ib: _nsa_selected(qp, qb, ib, gather, tbl_a), pos, q_a, sel_idx)
    g_a = jax.nn.sigmoid(a_gate.reshape(B, S, H_NSA, 3))
    o_a = (g_a[..., 0:1] * o_cmp + g_a[..., 1:2] * o_sel + g_a[..., 2:3] * o_win).reshape(B, S, BR_W)

    cq = _rmsnorm(b_cq, mla_gq)
    qf = (cq @ mla_wuq).reshape(B, S, H_MLA, D_NOPE + D_ROPE)
    q_lat = jnp.einsum('bshn,chn->bshc', qf[..., :D_NOPE], mla_wuk) * MLA_SCALE
    q_pe = _rope(qf[..., D_NOPE:], pos) * MLA_SCALE
    lat_new = _rmsnorm(b_ckv, mla_gkv)
    kpe_new = _rope(b_kpe, pos)
    lat, kpe = with_past('mla_lat', lat_new), with_past('mla_kpe', kpe_new)
    kpos_all = jnp.arange(lat.shape[1])
    o_lat = _sweep(lambda qp, ql, qr: _mla_attend(qp, ql, qr, lat, kpe, kpos_all), pos, q_lat, q_pe)
    o_b = jnp.einsum('bshc,chv->bshv', o_lat, mla_wuv).reshape(B, S, BR_W)

    q_c = c_q.reshape(B, S, H_FOX, DH) * DH ** -0.5
    fkv_new = jnp.stack([c_k.reshape(B, S, KV_FOX, DH), c_v.reshape(B, S, KV_FOX, DH)], axis=2)
    logf_new = jax.nn.log_sigmoid((c_f + fox_bf).astype(jnp.float32))
    fkv = with_past('fox_kv', fkv_new)
    logf = with_past('fox_logf', logf_new).astype(jnp.float32)
    r = lax.cumsum(logf, axis=1, reverse=True) - logf
    o_c = _sweep(lambda qp, qb, rq: _fox_attend(qp, qb, rq, fkv, r, kpos_all),
                 pos, q_c, r[:, r.shape[1] - S:]).reshape(B, S, BR_W)

    q_d = d_q.reshape(B, S, H_DIFF, 2, DD) * DD ** -0.5
    dkv_new = jnp.stack([d_k.reshape(B, S, KV_DIFF, 2 * DD), d_v.reshape(B, S, KV_DIFF, 2 * DD)], axis=2)
    dkv = with_past('diff_kv', dkv_new)
    lam_init = 0.8 - 0.6 * math.exp(-0.3 * lidx)
    lam = (jnp.exp(jnp.sum(diff_lam[0] * diff_lam[1]).astype(jnp.float32))
           - jnp.exp(jnp.sum(diff_lam[2] * diff_lam[3]).astype(jnp.float32)) + lam_init)
    o_d = _sweep(lambda qp, qb: _diff_attend(qp, qb, dkv, kpos_all, tbl_d, lam), pos, q_d)
    o_d = (_rmsnorm(o_d, diff_g, 1e-5) * (1.0 - lam_init)).reshape(B, S, BR_W)

    outs = jnp.stack([o_a * jax.nn.silu(a_z), o_b * jax.nn.silu(b_z),
                      o_c * jax.nn.silu(c_z), o_d * jax.nn.silu(d_z)], axis=2)
    branch = jnp.einsum('bsnc,ncd->bsnd', outs, w_branch)
    gates = jax.nn.sigmoid(merge_logit.reshape(B, S, N_BRANCH, D_MODEL))
    y = _pmatmul(jnp.sum(gates * branch, axis=2), w_out)
    x_new = _layernorm(DEEPNORM_ALPHA * x + y, ln_g, ln_b)
    states = (cmp_new, sel_new, win_state, lat_new, kpe_new, fkv_new, logf_new.astype(dt), dkv_new)
    return x_new, states


TQ = 128
TK = 128
ROW_CHUNK = 256
LANE = 128
BF = jnp.bfloat16
F32 = jnp.float32
VMEM_LIMIT = 56 * 1024 * 1024

PG_AQ, PG_CMP, PG_SEL, PG_WIN, PG_MISC = 0, 512, 640, 768, 896
PG_CQ, PG_CKV, PG_FQ, PG_FKV, PG_DQ, PG_DKV, PG_Z, PG_END = 1024, 1280, 1536, 2048, 2304, 3328, 3584, 4608
MISC_KPE, MISC_GATE, MISC_F = 0, 32, 48
N_SMALL = N_IN - N_BRANCH * D_MODEL


def _in_offsets():
    offs, s = [], 0
    for n in IN_SIZES:
        offs.append(s)
        s += n
    return offs


def _small_proj_columns():
    import numpy as np
    (o_aq, o_cmp, o_sel, o_win, o_gate, o_az, o_cq, o_ckv, o_kpe, o_bz,
     o_fq, o_fk, o_fv, o_ff, o_cz, o_dq, o_dk, o_dv, o_dz, _) = _in_offsets()
    src = np.full((PG_END,), -1, np.int32)
    for h in range(H_NSA):
        src[PG_AQ + LANE * h: PG_AQ + LANE * h + DH] = o_aq + DH * h + np.arange(DH)
    src[PG_CMP:PG_CMP + 2 * DH] = o_cmp + np.arange(2 * DH)
    src[PG_SEL:PG_SEL + 2 * DH] = o_sel + np.arange(2 * DH)
    src[PG_WIN:PG_WIN + 2 * DH] = o_win + np.arange(2 * DH)
    src[PG_MISC + MISC_KPE: PG_MISC + MISC_KPE + D_ROPE] = o_kpe + np.arange(D_ROPE)
    src[PG_MISC + MISC_GATE: PG_MISC + MISC_GATE + 3 * H_NSA] = o_gate + np.arange(3 * H_NSA)
    src[PG_MISC + MISC_F: PG_MISC + MISC_F + H_FOX] = o_ff + np.arange(H_FOX)
    src[PG_CQ:PG_CQ + D_CQ] = o_cq + np.arange(D_CQ)
    src[PG_CKV:PG_CKV + D_C] = o_ckv + np.arange(D_C)
    for h in range(H_FOX):
        kvh = h // G_FOX
        base = PG_FQ + LANE * h + DH * kvh
        src[base: base + DH] = o_fq + DH * h + np.arange(DH)
    src[PG_FKV:PG_FKV + 2 * KV_FOX * DH] = o_fk + np.arange(2 * KV_FOX * DH)
    for kvh in range(KV_DIFF):
        for mp in range(2):
            for g in range(G_DIFF):
                h = kvh * G_DIFF + g
                r = (kvh * 2 + mp) * G_DIFF + g
                base = PG_DQ + LANE * r + 2 * DD * kvh + DD * mp
                src[base: base + DD] = o_dq + 2 * DD * h + DD * mp + np.arange(DD)
    src[PG_DKV:PG_DKV + 2 * KV_DIFF * 2 * DD] = o_dk + np.arange(2 * KV_DIFF * 2 * DD)
    src[PG_Z + 0 * BR_W: PG_Z + 1 * BR_W] = o_az + np.arange(BR_W)
    src[PG_Z + 1 * BR_W: PG_Z + 2 * BR_W] = o_bz + np.arange(BR_W)
    src[PG_Z + 2 * BR_W: PG_Z + 3 * BR_W] = o_cz + np.arange(BR_W)
    src[PG_Z + 3 * BR_W: PG_Z + 4 * BR_W] = o_dz + np.arange(BR_W)
    return src


def _permute_cols(w, src):
    parts, i, n = [], 0, len(src)
    while i < n:
        j = i + 1
        if src[i] < 0:
            while j < n and src[j] < 0:
                j += 1
            parts.append(jnp.zeros(w.shape[:-1] + (j - i,), w.dtype))
        else:
            while j < n and src[j] == src[j - 1] + 1:
                j += 1
            parts.append(w[..., int(src[i]): int(src[i]) + (j - i)])
        i = j
    return jnp.concatenate(parts, -1)


def _uq_columns():
    import numpy as np
    src = np.full((H_MLA * D_NOPE + H_MLA * LANE,), -1, np.int32)
    per = D_NOPE + D_ROPE
    for h in range(H_MLA):
        src[D_NOPE * h: D_NOPE * (h + 1)] = per * h + np.arange(D_NOPE)
        src[H_MLA * D_NOPE + LANE * h: H_MLA * D_NOPE + LANE * h + D_ROPE] = per * h + D_NOPE + np.arange(D_ROPE)
    return src


def _rope_tables(pos):
    half = D_ROPE // 2
    inv = ROPE_BASE ** (-jnp.arange(half, dtype=F32) / half)
    ang = pos.astype(F32)[:, None] * inv
    cos, sin = jnp.cos(ang), jnp.sin(ang)
    n = pos.shape[0]
    one = jnp.ones((n, LANE - D_ROPE), F32)
    zer = jnp.zeros((n, LANE - D_ROPE), F32)
    zh = jnp.zeros((n, half), F32)
    c = jnp.concatenate([cos, cos, one], -1)
    s1 = jnp.concatenate([zh, sin, zer], -1)
    s2 = jnp.concatenate([-sin, zh, zer], -1)
    return c, s1, s2


def _proj_kernel(x_ref, wp_ref, wuq_ref, wuk_ref, gq_ref, gkv_ref, fb_ref, c_ref, s1_ref, s2_ref,
                 qa_ref, cmp_ref, sel_ref, win_ref, selb_ref, winb_ref, misc_ref, lat_ref, kcat_ref,
                 qcat_ref, fq_ref, fkv_ref, fkvb_ref, dq_ref, dkv_ref, dkvb_ref, z_ref):
    xb = x_ref[...].astype(BF)

    def mm(a, b):
        return jnp.dot(xb, wp_ref[:, a:b], preferred_element_type=F32)

    cos, s1, s2 = c_ref[...], s1_ref[...], s2_ref[...]

    def rope(v):
        return v * cos + pltpu.roll(v, D_ROPE // 2, 1) * s1 + pltpu.roll(v, LANE - D_ROPE // 2, 1) * s2

    qa_ref[...] = (mm(PG_AQ, PG_CMP) * DH ** -0.5).astype(BF)
    cmp_ref[...] = mm(PG_CMP, PG_SEL)
    t = mm(PG_SEL, PG_WIN)
    sel_ref[...] = t
    selb_ref[...] = t.astype(BF)
    t = mm(PG_WIN, PG_MISC)
    win_ref[...] = t
    winb_ref[...] = t.astype(BF)

    misc = mm(PG_MISC, PG_CQ)
    lane = lax.broadcasted_iota(jnp.int32, misc.shape, 1)
    roped = rope(misc)
    xf = misc + fb_ref[...]
    logf = jnp.minimum(xf, 0.0) - jnp.log1p(jnp.exp(-jnp.abs(xf)))
    misc_ref[...] = jnp.where((lane >= MISC_F) & (lane < MISC_F + H_FOX), logf, roped)

    cq = mm(PG_CQ, PG_CKV)
    cq = cq * lax.rsqrt(jnp.mean(cq * cq, -1, keepdims=True) + 1e-6) * gq_ref[...]
    qf = jnp.dot(cq.astype(BF), wuq_ref[...], preferred_element_type=F32)
    nn = H_MLA * D_NOPE
    ql = jnp.dot(qf[:, :nn].astype(BF), wuk_ref[...], preferred_element_type=F32) * MLA_SCALE
    wq = D_C + LANE
    for h in range(H_MLA):
        qcat_ref[:, wq * h: wq * h + D_C] = ql[:, D_C * h: D_C * (h + 1)].astype(BF)
        pe = rope(qf[:, nn + LANE * h: nn + LANE * (h + 1)]) * MLA_SCALE
        qcat_ref[:, wq * h + D_C: wq * (h + 1)] = pe.astype(BF)

    ckv = mm(PG_CKV, PG_FQ)
    lat = ckv * lax.rsqrt(jnp.mean(ckv * ckv, -1, keepdims=True) + 1e-6) * gkv_ref[...]
    lat_ref[...] = lat
    kcat_ref[:, :D_C] = lat.astype(BF)
    kcat_ref[:, D_C:] = jnp.where(lane < D_ROPE, roped, 0.0).astype(BF)

    fq_ref[...] = (mm(PG_FQ, PG_FKV) * DH ** -0.5).astype(BF)
    t = mm(PG_FKV, PG_DQ)
    fkv_ref[...] = t
    fkvb_ref[...] = t.astype(BF)
    dq_ref[...] = (mm(PG_DQ, PG_DKV) * DD ** -0.5).astype(BF)
    t = mm(PG_DKV, PG_Z)
    dkv_ref[...] = t
    dkvb_ref[...] = t.astype(BF)
    z_ref[...] = mm(PG_Z, PG_END)


def _proj(x2, wp, wuq, wukbd, gq, gkv, fb, tabs, tm):
    M = x2.shape[0]
    npos = tabs[0].shape[0] // tm
    row = lambda i: (i, 0)
    cst = lambda i: (0, 0)
    tab = lambda i: (i % npos, 0)
    widths = [(512, BF), (128, F32), (128, F32), (128, F32), (128, BF), (128, BF), (128, F32), (D_C, F32),
              (D_C + LANE, BF), (H_MLA * (D_C + LANE), BF), (512, BF), (256, F32), (256, BF), (1024, BF),
              (256, F32), (256, BF), (1024, F32)]
    return pl.pallas_call(
        _proj_kernel,
        grid=(M // tm,),
        in_specs=[pl.BlockSpec((tm, D_MODEL), row),
                  pl.BlockSpec(wp.shape, cst), pl.BlockSpec(wuq.shape, cst), pl.BlockSpec(wukbd.shape, cst),
                  pl.BlockSpec((1, D_CQ), cst), pl.BlockSpec((1, D_C), cst), pl.BlockSpec((1, LANE), cst),
                  pl.BlockSpec((tm, LANE), tab), pl.BlockSpec((tm, LANE), tab), pl.BlockSpec((tm, LANE), tab)],
        out_specs=[pl.BlockSpec((tm, w), row) for w, _ in widths],
        out_shape=[jax.ShapeDtypeStruct((M, w), d) for w, d in widths],
        compiler_params=pltpu.CompilerParams(dimension_semantics=("arbitrary",), vmem_limit_bytes=VMEM_LIMIT),
        name="proj",
    )(x2, wp, wuq, wukbd, gq, gkv, fb, *tabs)


def _cmp_summary_kernel(x_ref, pos_ref, wc_ref, w2_ref, o_ref):
    xb = (x_ref[...] + pos_ref[...]).astype(BF)
    hid = jnp.dot(xb, wc_ref[...], preferred_element_type=F32)
    act = hid * jax.nn.sigmoid(hid)
    o_ref[...] = jnp.dot(act.astype(BF), w2_ref[...], preferred_element_type=F32)


def _cmp_summary(x3, layer, pos_row, wc, w2, tb):
    _, N, W = x3.shape
    return pl.pallas_call(
        _cmp_summary_kernel,
        grid=(N // tb,),
        in_specs=[pl.BlockSpec((None, tb, W), lambda i: (layer, i, 0)),
                  pl.BlockSpec((1, W), lambda i: (0, 0)),
                  pl.BlockSpec(wc.shape, lambda i: (0, 0)),
                  pl.BlockSpec(w2.shape, lambda i: (0, 0))],
        out_specs=pl.BlockSpec((tb, LANE), lambda i: (i, 0)),
        out_shape=jax.ShapeDtypeStruct((N, LANE), F32),
        compiler_params=pltpu.CompilerParams(dimension_semantics=("arbitrary",), vmem_limit_bytes=VMEM_LIMIT),
        name="cmp_summary",
    )(x3, pos_row, wc, w2)


CMP_PAGES = 64


def _cmp_paged_kernel(n_pages, pt_ref, *refs):
    pages = refs[:n_pages]
    pos_ref, wc_ref, w2_ref, o_ref, xs = refs[n_pages:]
    per = PAGE_SIZE // CMP_BLK
    for p in range(n_pages):
        xs[PAGE_SIZE * p: PAGE_SIZE * (p + 1), :] = pages[p][...].T
    w = 2 * DH
    hid = jnp.zeros((n_pages * per, w), F32)
    for i in range(CMP_BLK):
        rows = xs[pl.ds(i, n_pages * per, stride=CMP_BLK), :] + pos_ref[:, w * i: w * (i + 1)]
        hid = hid + jnp.dot(rows.astype(BF), wc_ref[w * i: w * (i + 1), :], preferred_element_type=F32)
    act = hid * jax.nn.sigmoid(hid)
    o_ref[...] = jnp.dot(act.astype(BF), w2_ref[...], preferred_element_type=F32)


def _cmp_paged_summary(xt, layer, page_table, pos_row, wc, w2, pp):
    DB, NP = page_table.shape
    per = PAGE_SIZE // CMP_BLK
    cst = lambda a: pl.BlockSpec(a.shape, lambda b, t, pt: (0,) * a.ndim)
    return pl.pallas_call(
        functools.partial(_cmp_paged_kernel, pp),
        grid_spec=pltpu.PrefetchScalarGridSpec(
            num_scalar_prefetch=1, grid=(DB, NP // pp),
            in_specs=[pl.BlockSpec((None, None, 2 * DH, PAGE_SIZE),
                                   lambda b, t, pt, g=g: (layer, pt[b, t * pp + g], 0, 0)) for g in range(pp)]
            + [cst(pos_row), cst(wc), cst(w2)],
            out_specs=pl.BlockSpec((None, pp * per, LANE), lambda b, t, pt: (b, t, 0)),
            scratch_shapes=[pltpu.VMEM((pp * PAGE_SIZE, 2 * DH), F32)]),
        out_shape=jax.ShapeDtypeStruct((DB, NP * per, LANE), F32),
        compiler_params=pltpu.CompilerParams(dimension_semantics=("arbitrary", "arbitrary"),
                                             vmem_limit_bytes=VMEM_LIMIT),
        name="cmp_paged_summary",
    )(page_table, *([xt] * pp), pos_row, wc, w2)


def _split3(x):
    hi = x.astype(BF)
    r = x - hi.astype(F32)
    mid = r.astype(BF)
    lo = (r - mid.astype(F32)).astype(BF)
    return hi, mid, lo


def _dot3(x, u):
    hi, mid, lo = _split3(x)
    d = lambda a: jnp.dot(a, u, preferred_element_type=F32)
    return d(hi) + d(mid) + d(lo)


def _prefix_kernel(x_ref, u_ref, o_ref):
    u = u_ref[...]
    carry = jnp.zeros((x_ref.shape[0], 1), F32)
    for c in range(x_ref.shape[1] // LANE):
        p = _dot3(x_ref[:, LANE * c: LANE * (c + 1)], u)
        o_ref[:, LANE * c: LANE * (c + 1)] = p + carry
        carry = carry + p[:, LANE - 1: LANE]


def _prefix(x, u):
    return pl.pallas_call(_prefix_kernel, out_shape=jax.ShapeDtypeStruct(x.shape, F32), name="fox_prefix")(x, u)


def _online(s, v, m_ref, l_ref, acc_ref):
    m_prev = m_ref[...]
    m_new = jnp.maximum(m_prev, jnp.max(s, -1, keepdims=True))
    alpha = jnp.exp(m_prev - m_new)
    p = jnp.exp(s - m_new)
    l_ref[...] = alpha * l_ref[...] + jnp.sum(p, -1, keepdims=True)
    acc_ref[...] = alpha * acc_ref[...] + jnp.dot(p.astype(BF), v, preferred_element_type=F32)
    m_ref[...] = m_new


def _init_state(m_ref, l_ref, acc_ref):
    m_ref[...] = jnp.full(m_ref.shape, NEG_INF, F32)
    l_ref[...] = jnp.zeros(l_ref.shape, F32)
    acc_ref[...] = jnp.zeros(acc_ref.shape, F32)


def _qk(q, k):
    return lax.dot_general(q, k, (((1,), (1,)), ((), ())), preferred_element_type=F32)


def _key_tiles(S):
    nkb = S // TK
    return 4 if nkb % 4 == 0 else (2 if nkb % 2 == 0 else 1)


def _causal_sweep(qb, W, rows, tile, m_ref, l_ref, acc_ref):
    nfull = lax.shift_right_logical(qb, int(math.log2(W)))
    chunks = [slice(r0, r0 + ROW_CHUNK) for r0 in range(0, rows, ROW_CHUNK)]

    def body(j, c):
        for rs in chunks:
            s, v = tile(j, rs)
            _online(s, v, m_ref.at[rs], l_ref.at[rs], acc_ref.at[rs])
        return c

    lax.fori_loop(0, nfull, body, 0)
    r = lax.broadcasted_iota(jnp.int32, (ROW_CHUNK, W * TK), 0) & (TQ - 1)
    c = lax.broadcasted_iota(jnp.int32, (ROW_CHUNK, W * TK), 1)
    ok = c <= r + TQ * (qb - W * nfull)
    for rs in chunks:
        s, v = tile(nfull, rs)
        _online(jnp.where(ok, s, NEG_INF), v, m_ref.at[rs], l_ref.at[rs], acc_ref.at[rs])


def _bias_tiles(tb_ref, heads, qb, W, j):
    cols = []
    for u in range(W):
        d = jnp.clip(qb - (W * j + u), 0, 2)
        cols.append(jnp.concatenate([tb_ref[h, d] for h in heads], 0))
    return cols[0] if W == 1 else jnp.concatenate(cols, 1)


def _mla_kernel(W, q_ref, k_ref, o_ref, qs, m_ref, l_ref, acc_ref):
    qb = pl.program_id(1)
    wq = D_C + LANE
    kw = W * TK
    for h in range(H_MLA):
        qs[TQ * h: TQ * (h + 1), :] = q_ref[:, wq * h: wq * (h + 1)]
    _init_state(m_ref, l_ref, acc_ref)

    def tile(j, rs):
        k = k_ref[pl.ds(pl.multiple_of(j * kw, kw), kw), :]
        return _qk(qs[rs, :], k), k[:, :D_C]

    _causal_sweep(qb, W, H_MLA * TQ, tile, m_ref, l_ref, acc_ref)
    for h in range(H_MLA):
        blk = slice(TQ * h, TQ * (h + 1))
        o_ref[:, D_C * h: D_C * (h + 1)] = acc_ref[blk, :] / l_ref[blk, :]


def _fox_kernel(W, q_ref, kv_ref, pq_ref, pk_ref, o_ref, qs, m_ref, l_ref, acc_ref):
    qb = pl.program_id(1)
    kw = W * TK
    for h in range(H_FOX):
        qs[TQ * h: TQ * (h + 1), :] = q_ref[:, LANE * h: LANE * (h + 1)]
    _init_state(m_ref, l_ref, acc_ref)
    pq = pq_ref[...]

    def tile(j, rs):
        kv = kv_ref[pl.ds(pl.multiple_of(j * kw, kw), kw), :]
        pk = pk_ref[j]
        decay = jnp.concatenate([pq[:, h: h + 1] - pk[h: h + 1, :]
                                 for h in range(rs.start // TQ, rs.stop // TQ)], 0)
        return _qk(qs[rs, :], kv[:, :LANE]) + decay, kv[:, LANE:]

    _causal_sweep(qb, W, H_FOX * TQ, tile, m_ref, l_ref, acc_ref)
    for h in range(H_FOX):
        blk = slice(TQ * h, TQ * (h + 1))
        kvh = h // G_FOX
        o_ref[:, DH * h: DH * (h + 1)] = acc_ref[blk, DH * kvh: DH * (kvh + 1)] / l_ref[blk, :]


def _diff_rows():
    out = []
    for kvh in range(KV_DIFF):
        for mp in range(2):
            for g in range(G_DIFF):
                out.append((kvh * G_DIFF + g, mp))
    return out


def _diff_kernel(W, q_ref, kv_ref, tb_ref, dl_ref, lam0_ref, o_ref, qs, m_ref, l_ref, acc_ref):
    qb = pl.program_id(1)
    kw = W * TK
    rows = _diff_rows()
    for r in range(len(rows)):
        qs[TQ * r: TQ * (r + 1), :] = q_ref[:, LANE * r: LANE * (r + 1)]
    _init_state(m_ref, l_ref, acc_ref)

    def tile(j, rs):
        kv = kv_ref[pl.ds(pl.multiple_of(j * kw, kw), kw), :]
        heads = [h for h, _ in rows[rs.start // TQ: rs.stop // TQ]]
        s = _qk(qs[rs, :], kv[:, :LANE]) + _bias_tiles(tb_ref, heads, qb, W, j)
        return s, kv[:, LANE:]

    _causal_sweep(qb, W, len(rows) * TQ, tile, m_ref, l_ref, acc_ref)

    dl = dl_ref[...]
    lam = (jnp.exp(jnp.sum(dl[0:1] * dl[1:2], -1, keepdims=True))
           - jnp.exp(jnp.sum(dl[2:3] * dl[3:4], -1, keepdims=True)) + lam0_ref[...])
    for kvh in range(KV_DIFF):
        for g in range(G_DIFF):
            h = kvh * G_DIFF + g
            b0 = slice(TQ * rows.index((h, 0)), TQ * (rows.index((h, 0)) + 1))
            b1 = slice(TQ * rows.index((h, 1)), TQ * (rows.index((h, 1)) + 1))
            vs = slice(2 * DD * kvh, 2 * DD * (kvh + 1))
            o0 = acc_ref[b0, vs] / l_ref[b0, :]
            o1 = acc_ref[b1, vs] / l_ref[b1, :]
            o_ref[:, 2 * DD * h: 2 * DD * (h + 1)] = o0 - lam * o1


def _nsa_kernel(W, q_ref, kvc_ref, sel_ref, win_ref, cb_ref, tb_ref, o_ref, qs, m_ref, l_ref, acc_ref, smask):
    qb = pl.program_id(1)
    R = H_NSA * TQ
    kw = W * TK
    nkb = smask.shape[0] * W
    n_sel = nkb * (TK // SEL_BLK)
    n_pair = kvc_ref.shape[1]
    qs[...] = jnp.zeros(qs.shape, BF)
    for h in range(H_NSA):
        qs[TQ * h: TQ * (h + 1), :] = q_ref[:, LANE * h: LANE * (h + 1)]
    q = qs[...]

    ke = kvc_ref[0].astype(BF)
    ko = kvc_ref[1].astype(BF)
    ri = lax.broadcasted_iota(jnp.int32, (R, n_pair), 0) & (TQ - 1)
    li = lax.broadcasted_iota(jnp.int32, (R, n_pair), 1)
    qpos = qb * TQ + ri
    me = qpos >= SEL_BLK * li + (CMP_BLK - 1)
    mo = qpos >= SEL_BLK * li + (SEL_BLK - 1)
    se = jnp.where(me, _qk(q, ke) + cb_ref[0], NEG_INF)
    so = jnp.where(mo, _qk(q, ko) + cb_ref[1], NEG_INF)
    mx = jnp.maximum(jnp.max(se, -1, keepdims=True), jnp.max(so, -1, keepdims=True))
    ee = jnp.where(me, jnp.exp(se - mx), 0.0)
    eo = jnp.where(mo, jnp.exp(so - mx), 0.0)
    den = jnp.maximum(jnp.sum(ee, -1, keepdims=True) + jnp.sum(eo, -1, keepdims=True), 1e-30)
    pe = ee / den
    po = eo / den
    ocmp = (jnp.dot(pe.astype(BF), ke, preferred_element_type=F32)
            + jnp.dot(po.astype(BF), ko, preferred_element_type=F32))
    for h in range(H_NSA):
        o_ref[:, DH * h: DH * (h + 1)] = ocmp[TQ * h: TQ * (h + 1), DH:]
    pp = pe + po
    imp = pp[0:TQ]
    for h in range(1, H_NSA):
        imp = imp + pp[TQ * h: TQ * (h + 1)]

    bi = lax.broadcasted_iota(jnp.int32, (TQ, n_pair), 1)
    qi = qb * TQ + lax.broadcasted_iota(jnp.int32, (TQ, n_pair), 0)
    cur = lax.shift_right_logical(qi, int(math.log2(SEL_BLK)))
    valid = bi <= cur
    forced = valid & ((bi == 0) | (cur - bi < N_LOCAL))
    score = jnp.where(valid, jnp.where(forced, FORCE, imp), -FORCE)
    rank = jnp.zeros((TQ, n_pair), F32)
    for i in range(n_sel):
        col = score[:, i: i + 1]
        ahead = (col > score) | ((col == score) & (bi > i))
        rank = rank + jnp.where(ahead, 1.0, 0.0)
    madd = jnp.where(rank < TOP_N, 0.0, NEG_INF)
    lane = lax.broadcasted_iota(jnp.int32, (TQ, TK), 1)
    for j in range(nkb // W):
        cols = [jnp.where(lane < SEL_BLK, madd[:, 2 * kb: 2 * kb + 1], madd[:, 2 * kb + 1: 2 * kb + 2])
                for kb in range(W * j, W * (j + 1))]
        smask[j] = cols[0] if W == 1 else jnp.concatenate(cols, 1)

    _init_state(m_ref, l_ref, acc_ref)
    heads = list(range(H_NSA))

    def sel_tile(j, rs):
        kv = sel_ref[pl.ds(pl.multiple_of(j * kw, kw), kw), :]
        hs = heads[rs.start // TQ: rs.stop // TQ]
        s = _qk(qs[rs, :], kv) + _bias_tiles(tb_ref, hs, qb, W, j) + jnp.concatenate([smask[j]] * len(hs), 0)
        return s, kv

    _causal_sweep(qb, W, R, sel_tile, m_ref, l_ref, acc_ref)
    for h in range(H_NSA):
        blk = slice(TQ * h, TQ * (h + 1))
        o_ref[:, BR_W + DH * h: BR_W + DH * (h + 1)] = acc_ref[blk, DH:] / l_ref[blk, :]

    _init_state(m_ref, l_ref, acc_ref)
    rr = lax.broadcasted_iota(jnp.int32, (R, TK), 0) & (TQ - 1)
    cc = lax.broadcasted_iota(jnp.int32, (R, TK), 1)
    back = WINDOW // TK
    tiles = []
    for u in range(back + 1):
        kb = qb - (back - u)
        kv = win_ref[pl.ds(pl.multiple_of(jnp.maximum(kb, 0) * TK, TK), TK), :]
        dist = (back - u) * TQ + rr - cc
        ok = (dist >= 0) & (dist <= WINDOW) & (kb >= 0)
        bias = jnp.concatenate([tb_ref[h, min(back - u, 2)] for h in heads], 0)
        tiles.append((jnp.where(ok, _qk(q, kv) + bias, NEG_INF), kv, False))
    _online_multi(tiles, m_ref, l_ref, acc_ref)
    for h in range(H_NSA):
        blk = slice(TQ * h, TQ * (h + 1))
        o_ref[:, 2 * BR_W + DH * h: 2 * BR_W + DH * (h + 1)] = acc_ref[blk, DH:] / l_ref[blk, :]


def _attn_call(kern, B, S, ins, specs, out_w, scratch, name):
    nq = S // TQ
    return pl.pallas_call(
        kern,
        grid=(B, nq),
        in_specs=specs,
        out_specs=pl.BlockSpec((TQ, out_w), lambda b, i: (b * nq + i, 0)),
        out_shape=jax.ShapeDtypeStruct((B * S, out_w), F32),
        scratch_shapes=scratch,
        compiler_params=pltpu.CompilerParams(dimension_semantics=("arbitrary", "arbitrary"),
                                             vmem_limit_bytes=VMEM_LIMIT),
        name=name,
    )(*ins)


def _state_scratch(rows, qw, dv):
    return [pltpu.VMEM((rows, qw), BF), pltpu.VMEM((rows, 1), F32), pltpu.VMEM((rows, 1), F32),
            pltpu.VMEM((rows, dv), F32)]


def _post_kernel(x_ref, olat_ref, onsa_ref, ofox_ref, odiff_ref, misc_ref, z_ref, wuv_ref, dg_ref, ds_ref,
                 wbr_ref, wm_ref, wo_ref, lg_ref, lb_ref, o_ref):
    x = x_ref[...]
    xb = x.astype(BF)
    gate = jax.nn.sigmoid(misc_ref[:, MISC_GATE: MISC_GATE + 3 * H_NSA])
    onsa = onsa_ref[...]
    pieces = []
    for h in range(H_NSA):
        hs = slice(DH * h, DH * (h + 1))
        pieces.append(gate[:, 3 * h: 3 * h + 1] * onsa[:, hs]
                      + gate[:, 3 * h + 1: 3 * h + 2] * onsa[:, BR_W + DH * h: BR_W + DH * (h + 1)]
                      + gate[:, 3 * h + 2: 3 * h + 3] * onsa[:, 2 * BR_W + DH * h: 2 * BR_W + DH * (h + 1)])
    oa = jnp.concatenate(pieces, -1)
    ob = jnp.dot(olat_ref[...].astype(BF), wuv_ref[...], preferred_element_type=F32)
    oc = ofox_ref[...]
    od_raw = odiff_ref[...]
    pieces = []
    for h in range(H_DIFF):
        v = od_raw[:, 2 * DD * h: 2 * DD * (h + 1)]
        pieces.append(v * lax.rsqrt(jnp.mean(v * v, -1, keepdims=True) + 1e-5))
    od = jnp.concatenate(pieces, -1) * dg_ref[...] * ds_ref[...]
    z = z_ref[...]
    zs = z * jax.nn.sigmoid(z)
    acc = jnp.zeros(x.shape, F32)
    for n, o in enumerate((oa, ob, oc, od)):
        on = (o * zs[:, BR_W * n: BR_W * (n + 1)]).astype(BF)
        br = jnp.dot(on, wbr_ref[n], preferred_element_type=F32)
        g = jax.nn.sigmoid(jnp.dot(xb, wm_ref[:, D_MODEL * n: D_MODEL * (n + 1)], preferred_element_type=F32))
        acc = acc + g * br
    y = jnp.dot(acc.astype(BF), wo_ref[...], preferred_element_type=F32)
    t = DEEPNORM_ALPHA * x + y
    mu = jnp.mean(t, -1, keepdims=True)
    var = jnp.mean(jnp.square(t - mu), -1, keepdims=True)
    o_ref[...] = (t - mu) * lax.rsqrt(var + 1e-5) * lg_ref[...] + lb_ref[...]


def _post(x2, olat, onsa, ofox, odiff, misc, z, wuv, dg, ds, wbr, wm, wo, lg, lb, tm):
    M = x2.shape[0]
    row = lambda i: (i, 0)
    cst2 = lambda i: (0, 0)
    cst3 = lambda i: (0, 0, 0)
    acts = (x2, olat, onsa, ofox, odiff, misc, z)
    return pl.pallas_call(
        _post_kernel,
        grid=(M // tm,),
        in_specs=[pl.BlockSpec((tm, a.shape[1]), row) for a in acts]
        + [pl.BlockSpec(wuv.shape, cst2), pl.BlockSpec(dg.shape, cst2), pl.BlockSpec(ds.shape, cst2),
           pl.BlockSpec(wbr.shape, cst3), pl.BlockSpec(wm.shape, cst2), pl.BlockSpec(wo.shape, cst2),
           pl.BlockSpec(lg.shape, cst2), pl.BlockSpec(lb.shape, cst2)],
        out_specs=pl.BlockSpec((tm, D_MODEL), row),
        out_shape=jax.ShapeDtypeStruct((M, D_MODEL), F32),
        compiler_params=pltpu.CompilerParams(dimension_semantics=("arbitrary",), vmem_limit_bytes=VMEM_LIMIT),
        name="post_merge",
    )(*acts, wuv, dg, ds, wbr, wm, wo, lg, lb)


def _layer_weights(l, w_in, cmp_pos, cmp_w1, cmp_w2, mla_gq, mla_gkv, mla_wuq, mla_wuk, mla_wuv,
                   fox_bf, diff_lam, diff_g, w_branch, w_out, ln_g, ln_b):
    import numpy as np
    wp = _permute_cols(w_in[l, :, :N_SMALL], _small_proj_columns()).astype(BF)
    wm = w_in[l, :, N_SMALL:].astype(BF)
    wuq = _permute_cols(mla_wuq[l], _uq_columns()).astype(BF)
    eye = jnp.eye(H_MLA, dtype=F32)
    wukbd = jnp.einsum('chn,hg->hngc', mla_wuk[l], eye).reshape(H_MLA * D_NOPE, H_MLA * D_C).astype(BF)
    wuvbd = jnp.einsum('chv,hg->hcgv', mla_wuv[l], eye).reshape(H_MLA * D_C, H_MLA * D_V).astype(BF)
    fb = jnp.zeros((1, LANE), F32).at[0, MISC_F: MISC_F + H_FOX].set(fox_bf[l])
    w1 = cmp_w1[l].reshape(2, CMP_BLK, DH, DH)
    wc = jnp.einsum('cldh,ce->lcdeh', w1, jnp.eye(2, dtype=F32)).reshape(CMP_BLK * 2 * DH, 2 * DH).astype(BF)
    w2 = jnp.einsum('che,cf->chfe', cmp_w2[l], jnp.eye(2, dtype=F32)).reshape(2 * DH, 2 * DH).astype(BF)
    cpos = jnp.swapaxes(cmp_pos[l], 0, 1).reshape(1, CMP_BLK * 2 * DH)
    lam_init = 0.8 - 0.6 * math.exp(-0.3 * l)
    return dict(
        wp=wp, wm=wm, wuq=wuq, wukbd=wukbd, wuvbd=wuvbd, fb=fb, wc=wc, w2=w2, cpos=cpos,
        gq=mla_gq[l][None], gkv=mla_gkv[l][None], dl=diff_lam[l],
        lam0=jnp.full((1, 1), lam_init, F32), ds=jnp.full((1, 1), 1.0 - lam_init, F32),
        dg=jnp.tile(diff_g[l], H_DIFF)[None], wbr=w_branch[l].astype(BF), wo=w_out[l].astype(BF),
        lg=ln_g[l][None], lb=ln_b[l][None])


def _bucket_table(dist):
    n = np.maximum(dist, 0)
    exact = N_BUCKETS // 2
    nf = np.maximum(n, 1).astype(np.float32)
    ratio = np.log(nf / np.float32(exact)) / np.float32(math.log(MAX_DIST / exact))
    large = exact + (ratio * np.float32(N_BUCKETS - exact)).astype(np.int32)
    return np.where(n < exact, n, np.minimum(large, N_BUCKETS - 1)).astype(np.int32)


def _bias_lookup(tbl, dist):
    onehot = (jnp.asarray(_bucket_table(dist))[..., None] == jnp.arange(N_BUCKETS)).astype(F32)
    return jnp.einsum('...b,bh->...h', onehot, tbl, precision=lax.Precision.HIGHEST)


def _prompt_bias_tables(rel_bias, S):
    nq = S // TQ
    i = np.arange(TQ)[:, None]
    j = np.arange(TK)[None, :]
    tiles = jnp.stack([_bias_lookup(rel_bias, d * TQ + i - j) for d in range(3)])
    tb = jnp.moveaxis(tiles, -1, 0)
    n_cmp = S // CMP_BLK
    ends = (np.arange(n_cmp) + 1) * CMP_BLK - 1
    dist = np.arange(S)[:, None] - ends[None, :]
    cb = _bias_lookup(rel_bias[:, :H_NSA], dist)
    cb = cb.reshape(nq, TQ, n_cmp // 2, 2, H_NSA)
    cb = jnp.transpose(cb, (0, 3, 4, 1, 2)).reshape(nq, 2, H_NSA * TQ, n_cmp // 2)
    return tb[:H_NSA], tb[H_NSA:], cb


def _prompt_layer(x, lw, tabs, tb_a, tb_d, cb, u_tri):
    B, S, _ = x.shape
    M = B * S
    nq = S // TQ
    W = _key_tiles(S)
    kw = W * TK
    x2 = x.reshape(M, D_MODEL)
    (qa, cmp_s, sel_s, win_s, selb, winb, misc, lat, kcat, qcat, fq, fkv, fkvb, dq, dkv, dkvb, z) = _proj(
        x2, lw['wp'], lw['wuq'], lw['wukbd'], lw['gq'], lw['gkv'], lw['fb'], tabs, 256)

    n_cmp = S // CMP_BLK
    kvc = _cmp_summary(cmp_s.reshape(1, M // CMP_BLK, CMP_BLK * 2 * DH), 0, lw['cpos'], lw['wc'], lw['w2'],
                       min(256, M // CMP_BLK))
    kvc = jnp.swapaxes(kvc.reshape(B, n_cmp // 2, 2, LANE), 1, 2)

    qrow = lambda w: pl.BlockSpec((TQ, w), lambda b, i: (b * nq + i, 0))
    kvrow = lambda w: pl.BlockSpec((S, w), lambda b, i: (b, 0))
    whole = lambda a: pl.BlockSpec(a.shape, lambda b, i: (0,) * a.ndim)

    onsa = _attn_call(
        functools.partial(_nsa_kernel, W), B, S, (qa, kvc, selb, winb, cb, tb_a),
        [qrow(512), pl.BlockSpec((None, 2, n_cmp // 2, LANE), lambda b, i: (b, 0, 0, 0)), kvrow(LANE), kvrow(LANE),
         pl.BlockSpec((None, 2, H_NSA * TQ, n_cmp // 2), lambda b, i: (i, 0, 0, 0)), whole(tb_a)],
        3 * BR_W, _state_scratch(H_NSA * TQ, LANE, LANE) + [pltpu.VMEM((S // kw, TQ, kw), F32)], "nsa_prompt")

    olat = _attn_call(functools.partial(_mla_kernel, W), B, S, (qcat, kcat), [qrow(H_MLA * (D_C + LANE)), kvrow(D_C + LANE)],
                      H_MLA * D_C, _state_scratch(H_MLA * TQ, D_C + LANE, D_C), "mla_prompt")

    logf = misc[:, MISC_F: MISC_F + H_FOX].reshape(B, S, H_FOX)
    pre = _prefix(jnp.swapaxes(logf, 1, 2).reshape(B * H_FOX, S), u_tri).reshape(B, H_FOX, S)
    pq = jnp.swapaxes(pre, 1, 2).reshape(M, H_FOX)
    pk = jnp.swapaxes(pre.reshape(B, H_FOX, S // kw, kw), 1, 2)
    pk = jnp.pad(pk, ((0, 0), (0, 0), (0, 8 - H_FOX), (0, 0)))
    ofox = _attn_call(
        functools.partial(_fox_kernel, W), B, S, (fq, fkvb, pq, pk),
        [qrow(512), kvrow(256), qrow(H_FOX), pl.BlockSpec((None, S // kw, 8, kw), lambda b, i: (b, 0, 0, 0))],
        BR_W, _state_scratch(H_FOX * TQ, LANE, LANE), "fox_prompt")

    odiff = _attn_call(
        functools.partial(_diff_kernel, W), B, S, (dq, dkvb, tb_d, lw['dl'], lw['lam0']),
        [qrow(1024), kvrow(256), whole(tb_d), whole(lw['dl']), whole(lw['lam0'])],
        BR_W, _state_scratch(2 * H_DIFF * TQ, LANE, LANE), "diff_prompt")

    x_new = _post(x2, olat, onsa, ofox, odiff, misc, z, lw['wuvbd'], lw['dg'], lw['ds'], lw['wbr'], lw['wm'],
                  lw['wo'], lw['lg'], lw['lb'], 256).reshape(B, S, D_MODEL)
    wlen = min(WINDOW, S)
    states = (cmp_s.reshape(B, S, 2, DH), sel_s.reshape(B, S, 2, DH),
              win_s.reshape(B, S, 2, DH)[:, S - wlen:],
              lat.reshape(B, S, D_C), misc[:, MISC_KPE: MISC_KPE + D_ROPE].reshape(B, S, D_ROPE),
              fkv.reshape(B, S, 2, KV_FOX, DH), logf, dkv.reshape(B, S, 2, KV_DIFF, 2 * DD))
    return x_new, states


PG = 16


def _suffix_kernel(x_ref, ugt_ref, one_ref, sfx_ref, tot_ref):
    x = x_ref[...]
    sfx_ref[...] = _dot3(x, ugt_ref[...])
    tot_ref[...] = _dot3(x, one_ref[...])


def _page_suffix(x, tr):
    R = x.shape[0]
    ugt = (jnp.arange(LANE)[:, None] > jnp.arange(LANE)[None, :]).astype(BF)
    one = jnp.ones((LANE, LANE), BF)
    return pl.pallas_call(
        _suffix_kernel,
        grid=(R // tr,),
        in_specs=[pl.BlockSpec((tr, LANE), lambda i: (i, 0)), pl.BlockSpec((LANE, LANE), lambda i: (0, 0)),
                  pl.BlockSpec((LANE, LANE), lambda i: (0, 0))],
        out_specs=[pl.BlockSpec((tr, LANE), lambda i: (i, 0))] * 2,
        out_shape=[jax.ShapeDtypeStruct((R, LANE), F32)] * 2,
        compiler_params=pltpu.CompilerParams(dimension_semantics=("arbitrary",)),
        name="fox_page_suffix",
    )(x, ugt, one)


def _online_multi(tiles, m_ref, l_ref, acc_ref):
    m_prev = m_ref[...]
    m_new = m_prev
    for s, _, _ in tiles:
        m_new = jnp.maximum(m_new, jnp.max(s, -1, keepdims=True))
    alpha = jnp.exp(m_prev - m_new)
    l_new = alpha * l_ref[...]
    acc = alpha * acc_ref[...]
    for s, v, v_t in tiles:
        p = jnp.exp(s - m_new)
        l_new = l_new + jnp.sum(p, -1, keepdims=True)
        pb = p.astype(BF)
        acc = acc + (_qk(pb, v) if v_t else jnp.dot(pb, v, preferred_element_type=F32))
    l_ref[...] = l_new
    acc_ref[...] = acc
    m_ref[...] = m_new


def _nsa_sample_kernel(dq, n_win, q_ref, kvc_ref, cb_ref, win_ref, wt_ref, wb1_ref, wb2_ref, o_ref, sm_ref):
    q = q_ref[...]
    rows = q.shape[0]
    n_pair = kvc_ref.shape[1]
    ke = kvc_ref[0].astype(BF)
    ko = kvc_ref[1].astype(BF)
    se = _qk(q, ke) + cb_ref[0]
    so = _qk(q, ko) + cb_ref[1]
    mx = jnp.maximum(jnp.max(se, -1, keepdims=True), jnp.max(so, -1, keepdims=True))
    ee = jnp.exp(se - mx)
    eo = jnp.exp(so - mx)
    den = jnp.maximum(jnp.sum(ee, -1, keepdims=True) + jnp.sum(eo, -1, keepdims=True), 1e-30)
    pe = ee / den
    po = eo / den
    o_ref[:, :LANE] = (jnp.dot(pe.astype(BF), ke, preferred_element_type=F32)
                       + jnp.dot(po.astype(BF), ko, preferred_element_type=F32))
    pp = pe + po
    a = pp[0:8] + pp[8:16]
    imp = a + pltpu.roll(a, dq, 0)
    bi = lax.broadcasted_iota(jnp.int32, (8, n_pair), 1)
    forced = (bi == 0) | (n_pair - bi < N_LOCAL)
    score = jnp.where(forced, FORCE, imp)
    rank = jnp.where(forced, 0.0, 1.0)
    for i in range(n_pair):
        col = score[:, i: i + 1]
        ahead = (col > score) | ((col == score) & (bi > i))
        rank = rank + jnp.where(ahead, 1.0, 0.0)
    madd = jnp.where(rank < TOP_N, 0.0, NEG_INF)
    sm_ref[...] = jnp.concatenate([madd] * (rows // 8), 0)

    wp = win_ref[...].astype(BF)
    wt = wt_ref[...]
    qi1 = lax.broadcasted_iota(jnp.int32, (rows, n_win), 0) & (dq - 1)
    t1 = lax.broadcasted_iota(jnp.int32, (rows, n_win), 1)
    ok1 = n_win + qi1 - t1 <= WINDOW
    qi2 = lax.broadcasted_iota(jnp.int32, (rows, LANE), 0) & (dq - 1)
    ok2 = lax.broadcasted_iota(jnp.int32, (rows, LANE), 1) <= qi2
    s1 = jnp.where(ok1, jnp.dot(q, wp, preferred_element_type=F32) + wb1_ref[...], NEG_INF)
    s2 = jnp.where(ok2, _qk(q, wt) + wb2_ref[...], NEG_INF)
    mx = jnp.maximum(jnp.max(s1, -1, keepdims=True), jnp.max(s2, -1, keepdims=True))
    e1 = jnp.where(ok1, jnp.exp(s1 - mx), 0.0)
    e2 = jnp.where(ok2, jnp.exp(s2 - mx), 0.0)
    den = jnp.maximum(jnp.sum(e1, -1, keepdims=True) + jnp.sum(e2, -1, keepdims=True), 1e-30)
    o_ref[:, LANE:] = (_qk((e1 / den).astype(BF), wp)
                       + jnp.dot((e2 / den).astype(BF), wt, preferred_element_type=F32))


def _stream_kernel(dq, nsteps, pt_ref, qml_ref, qmp_ref, qf_ref, qd_ref, qs_ref, sm_ref, *rest):
    npg = 7 * PG
    pages = rest[:npg]
    (tl_ref, tf_ref, td_ref, ts_ref, tlf_ref, dbl_ref, dbt_ref, dbc_ref, sbl_ref, sbt_ref, sbc_ref,
     dl_ref, lam0_ref) = rest[npg: npg + 13]
    om_ref, of_ref, od_ref, os_ref = rest[npg + 13: npg + 17]
    (mm, lm, am, mf, lf, af, md, ld, ad, ms, ls, as_, carry) = rest[npg + 17:]
    t = pl.program_id(1)

    @pl.when(t == 0)
    def _():
        _init_state(mm, lm, am)
        _init_state(mf, lf, af)
        _init_state(md, ld, ad)
        _init_state(ms, ls, as_)
        carry[...] = jnp.zeros(carry.shape, F32)

    qml, qmp, qf, qd, qs = qml_ref[...], qmp_ref[...], qf_ref[...], qd_ref[...], qs_ref[...]
    tlf = tlf_ref[...]
    cs = [tlf[:, 0:1]]
    for j in range(1, dq):
        cs.append(cs[-1] + tlf[:, j: j + 1])
    qrow = lax.broadcasted_iota(jnp.int32, (qf.shape[0], 1), 0) & (dq - 1)
    cq = cs[dq - 1]
    for j in range(dq - 2, -1, -1):
        cq = jnp.where(qrow == j, cs[j], cq)

    lane = lax.broadcasted_iota(jnp.int32, (qs.shape[0], LANE), 1)
    sm = sm_ref[...]
    first = t == 0
    mla_t, fox_t, diff_t, sel_t = [], [], [], []
    run = carry[...]
    for g in range(PG):
        lat_r, kpe_r, fox_r, sfx_r, tot_r, diff_r, sel_r = pages[7 * g: 7 * g + 7]
        lat = lat_r[...].astype(BF)
        kpe = kpe_r[...].astype(BF)
        s = _qk(qml, lat) + jnp.dot(qmp[:, :D_ROPE], kpe, preferred_element_type=F32)
        mla_t.append((s, lat, False))
        fkv = fox_r[...].astype(BF)
        s = jnp.dot(qf, fkv[:LANE], preferred_element_type=F32) + (cq + run + sfx_r[...])
        fox_t.append((s, fkv[LANE:], True))
        run = run + tot_r[:, 0:1]
        dkv = diff_r[...].astype(BF)
        dbias = jnp.where(first, dbl_ref[...], dbc_ref[...]) if g == 0 else dbc_ref[...]
        diff_t.append((jnp.dot(qd, dkv[:LANE], preferred_element_type=F32) + dbias, dkv[LANE:], True))
        skv = sel_r[...].astype(BF)
        sbias = jnp.where(first, sbl_ref[...], sbc_ref[...]) if g == 0 else sbc_ref[...]
        madd = jnp.where(lane < SEL_BLK, sm[:, 2 * g: 2 * g + 1], sm[:, 2 * g + 1: 2 * g + 2])
        sel_t.append((jnp.dot(qs, skv, preferred_element_type=F32) + sbias + madd, skv, True))
    carry[...] = run
    _online_multi(mla_t, mm, lm, am)
    _online_multi(fox_t, mf, lf, af)
    _online_multi(diff_t, md, ld, ad)
    _online_multi(sel_t, ms, ls, as_)

    @pl.when(t == nsteps - 1)
    def _():
        def causal(rows):
            qi = lax.broadcasted_iota(jnp.int32, (rows, LANE), 0) & (dq - 1)
            return lax.broadcasted_iota(jnp.int32, (rows, LANE), 1) <= qi

        tl, tf, td, ts = tl_ref[...], tf_ref[...], td_ref[...], ts_ref[...]
        s = _qk(qml, tl[:, :D_C]) + _qk(qmp, tl[:, D_C:])
        _online_multi([(jnp.where(causal(qml.shape[0]), s, NEG_INF), tl[:, :D_C], False)], mm, lm, am)
        ctile = jnp.zeros((qf.shape[0], LANE), F32)
        for j in range(dq):
            ctile = jnp.where(lane == j, cs[j], ctile)
        s = _qk(qf, tf[:, :LANE]) + (cq - ctile)
        _online_multi([(jnp.where(causal(qf.shape[0]), s, NEG_INF), tf[:, LANE:], False)], mf, lf, af)
        s = _qk(qd, td[:, :LANE]) + dbt_ref[...]
        _online_multi([(jnp.where(causal(qd.shape[0]), s, NEG_INF), td[:, LANE:], False)], md, ld, ad)
        s = _qk(qs, ts) + sbt_ref[...]
        _online_multi([(jnp.where(causal(qs.shape[0]), s, NEG_INF), ts, False)], ms, ls, as_)
        om_ref[...] = am[...] / lm[...]
        of_ref[...] = af[...] / lf[...]
        os_ref[...] = as_[...] / ls[...]
        dl = dl_ref[...]
        lam = (jnp.exp(jnp.sum(dl[0:1] * dl[1:2], -1, keepdims=True))
               - jnp.exp(jnp.sum(dl[2:3] * dl[3:4], -1, keepdims=True)) + lam0_ref[...])
        od = ad[...] / ld[...]
        half = G_DIFF * dq
        for kvh in range(KV_DIFF):
            o0 = od[2 * half * kvh: 2 * half * kvh + half]
            o1 = od[2 * half * kvh + half: 2 * half * (kvh + 1)]
            od_ref[half * kvh: half * (kvh + 1), :] = o0 - lam * o1


def _sample_bias_tables(rel_bias, past_len, dq, n_win):
    tbl_a, tbl_d = rel_bias[:, :H_NSA], rel_bias[:, H_NSA:]
    q = np.arange(dq)[:, None]

    def rows_hq(tbl, dist):
        b = _bias_lookup(tbl, dist)
        return jnp.transpose(b, (2, 0, 1)).reshape(tbl.shape[1] * dq, dist.shape[1])

    n_cmp = past_len // CMP_BLK
    ends = (np.arange(n_cmp) + 1) * CMP_BLK - 1
    cb = rows_hq(tbl_a, past_len + q - ends[None, :]).reshape(H_NSA * dq, n_cmp // 2, 2)
    cb = jnp.moveaxis(cb, -1, 0)
    key = np.arange(LANE)[None, :]
    wb1 = rows_hq(tbl_a, n_win + q - np.arange(n_win)[None, :])
    near = rows_hq(tbl_a, PAGE_SIZE + q - key)
    tail = rows_hq(tbl_a, q - key)
    far = rows_hq(tbl_a, 2 * PAGE_SIZE + q - key)[:, :1]

    def diff_rows(x):
        x = x.reshape(KV_DIFF, 1, G_DIFF, dq, x.shape[-1])
        return jnp.broadcast_to(x, (KV_DIFF, 2, G_DIFF, dq, x.shape[-1])).reshape(2 * H_DIFF * dq, x.shape[-1])

    dnear = diff_rows(rows_hq(tbl_d, PAGE_SIZE + q - key))
    dtail = diff_rows(rows_hq(tbl_d, q - key))
    dfar = diff_rows(rows_hq(tbl_d, 2 * PAGE_SIZE + q - key)[:, :1])
    return cb, wb1, near, tail, far, dnear, dtail, dfar


def _rows_hq(a, DB, dq, H):
    W = a.shape[1] // H
    return jnp.transpose(a.reshape(DB, dq, H, W), (0, 2, 1, 3)).reshape(DB, H * dq, W)


def _rows_token(a, DB, dq, H):
    W = a.shape[-1]
    return jnp.transpose(a.reshape(DB, H, dq, W), (0, 2, 1, 3)).reshape(DB * dq, H * W)


def _pad_tail(a, DB, dq):
    a = a.reshape(DB, dq, a.shape[-1])
    return jnp.pad(a, ((0, 0), (0, LANE - dq), (0, 0)))


def _sample_layer(x, l, lw, tabs, bias, page_table, caches, sfx, tot):
    (c_cmp, c_sel, st_win, win_prev, c_lat, c_kpe, c_fox, c_diff) = caches
    cb, wb1, near, tail, far, dnear, dtail, dfar = bias
    DB, dq, _ = x.shape
    M = DB * dq
    NP = page_table.shape[1]
    n_pool = c_lat.shape[1]
    n_win = st_win.shape[3]
    assert dq & (dq - 1) == 0 and H_NSA * dq == 16 and NP % PG == 0
    x2 = x.reshape(M, D_MODEL)
    (qa, cmp_s, sel_s, win_s, selb, winb, misc, lat, kcat, qcat, fq, fkv, fkvb, dq_, dkv, dkvb, z) = _proj(
        x2, lw['wp'], lw['wuq'], lw['wukbd'], lw['gq'], lw['gkv'], lw['fb'], tabs, M)

    kvc = _cmp_paged_summary(c_cmp, l, page_table, lw['cpos'], lw['wc'], lw['w2'],
                             CMP_PAGES if NP % CMP_PAGES == 0 else NP)
    n_cmp = NP * (PAGE_SIZE // CMP_BLK)
    kvc = jnp.swapaxes(kvc.reshape(DB, n_cmp // 2, 2, LANE), 1, 2)

    q_nsa = _rows_hq(qa, DB, dq, H_NSA)
    per_b = lambda *blk: pl.BlockSpec((None,) + blk, lambda b: (b,) + (0,) * len(blk))
    cst = lambda a: pl.BlockSpec(a.shape, lambda b: (0,) * a.ndim)
    wt = _pad_tail(winb, DB, dq)
    ocw, selmask = pl.pallas_call(
        lambda *r: _nsa_sample_kernel(dq, n_win, *r),
        grid=(DB,),
        in_specs=[per_b(16, LANE), per_b(2, n_cmp // 2, LANE), cst(cb),
                  pl.BlockSpec((None, None, LANE, n_win), lambda b: (l, b, 0, 0)), per_b(LANE, LANE),
                  cst(wb1), cst(tail)],
        out_specs=[per_b(16, 2 * LANE), per_b(16, n_cmp // 2)],
        out_shape=[jax.ShapeDtypeStruct((DB, 16, 2 * LANE), F32), jax.ShapeDtypeStruct((DB, 16, n_cmp // 2), F32)],
        compiler_params=pltpu.CompilerParams(dimension_semantics=("arbitrary",), vmem_limit_bytes=VMEM_LIMIT),
        name="nsa_sample",
    )(q_nsa, kvc, cb, st_win, wt, wb1, tail)

    nsteps = NP // PG
    sm = selmask.reshape(DB, 16, NP, 2)[:, :, ::-1]
    sm = jnp.transpose(sm.reshape(DB, 16, nsteps, PG * 2), (0, 2, 1, 3))

    qm = _rows_hq(qcat, DB, dq, H_MLA)
    qml, qmp = qm[..., :D_C], qm[..., D_C:]
    qf = _rows_hq(fq, DB, dq, H_FOX)
    qd = _rows_hq(dq_, DB, dq, 2 * H_DIFF)
    logf_new = misc[:, MISC_F: MISC_F + H_FOX].reshape(DB, dq, H_FOX)
    tlf = jnp.broadcast_to(jnp.swapaxes(logf_new, 1, 2)[:, :, None, :], (DB, H_FOX, dq, dq)).reshape(DB, 16, dq)
    tlf = jnp.pad(tlf, ((0, 0), (0, 0), (0, LANE - dq)))

    def page(rows, cols):
        return [pl.BlockSpec((None, None, rows, cols),
                             lambda b, t, pt, g=g: (l, pt[b, NP - 1 - (t * PG + g)], 0, 0)) for g in range(PG)]

    lat_s, kpe_s, fox_s, sfx_s, tot_s, diff_s, sel_s_ = (
        page(PAGE_SIZE, D_C), page(D_ROPE, PAGE_SIZE), page(256, PAGE_SIZE), page(16, LANE), page(16, LANE),
        page(256, PAGE_SIZE), page(LANE, PAGE_SIZE))
    page_specs, page_args = [], []
    for g in range(PG):
        page_specs += [lat_s[g], kpe_s[g], fox_s[g], sfx_s[g], tot_s[g], diff_s[g], sel_s_[g]]
        page_args += [c_lat, c_kpe, c_fox, sfx, tot, c_diff, c_sel]
    pb = lambda *blk: pl.BlockSpec((None,) + blk, lambda b, t, pt: (b,) + (0,) * len(blk))
    cs = lambda a: pl.BlockSpec(a.shape, lambda b, t, pt: (0,) * a.ndim)
    tails = (_pad_tail(kcat, DB, dq), _pad_tail(fkvb, DB, dq), _pad_tail(dkvb, DB, dq), _pad_tail(selb, DB, dq))
    consts = (dnear, dtail, dfar, near, tail, far, lw['dl'], lw['lam0'])
    st = lambda rows, dv: [pltpu.VMEM((rows, 1), F32), pltpu.VMEM((rows, 1), F32), pltpu.VMEM((rows, dv), F32)]
    om, of, od, os_ = pl.pallas_call(
        lambda *r: _stream_kernel(dq, nsteps, *r),
        grid_spec=pltpu.PrefetchScalarGridSpec(
            num_scalar_prefetch=1, grid=(DB, nsteps),
            in_specs=[pb(16, D_C), pb(16, LANE), pb(16, LANE), pb(32, LANE), pb(16, LANE),
                      pl.BlockSpec((None, None, 16, 2 * PG), lambda b, t, pt: (b, t, 0, 0))]
            + page_specs
            + [pb(LANE, D_C + LANE), pb(LANE, 256), pb(LANE, 256), pb(LANE, LANE), pb(16, LANE)]
            + [cs(a) for a in consts],
            out_specs=[pb(16, D_C), pb(16, LANE), pb(16, LANE), pb(16, LANE)],
            scratch_shapes=st(16, D_C) + st(16, LANE) + st(32, LANE) + st(16, LANE) + [pltpu.VMEM((16, 1), F32)]),
        out_shape=[jax.ShapeDtypeStruct((DB, 16, D_C), F32), jax.ShapeDtypeStruct((DB, 16, LANE), F32),
                   jax.ShapeDtypeStruct((DB, 16, LANE), F32), jax.ShapeDtypeStruct((DB, 16, LANE), F32)],
        compiler_params=pltpu.CompilerParams(dimension_semantics=("arbitrary", "arbitrary"),
                                             vmem_limit_bytes=VMEM_LIMIT),
        name="sample_stream",
    )(page_table, qml, qmp, qf, qd, q_nsa, sm, *page_args, *tails, tlf, *consts)

    olat = _rows_token(om, DB, dq, H_MLA)

    def pick_kv(o, H, G):
        o = o.reshape(DB, H, dq, 2, DH)
        o = jnp.stack([o[:, h, :, h // G] for h in range(H)], 1)
        return _rows_token(o.reshape(DB, H * dq, DH), DB, dq, H)

    ofox = pick_kv(of, H_FOX, G_FOX)
    odiff = pick_kv(od, H_DIFF, G_DIFF)
    upper = lambda o: _rows_token(o[..., DH:], DB, dq, H_NSA)
    onsa = jnp.concatenate([upper(ocw[..., :LANE]), upper(os_), upper(ocw[..., LANE:])], -1)
    x_new = _post(x2, olat, onsa, ofox, odiff, misc, z, lw['wuvbd'], lw['dg'], lw['ds'], lw['wbr'], lw['wm'],
                  lw['wo'], lw['lg'], lw['lb'], M).reshape(DB, dq, D_MODEL)
    win_all = jnp.concatenate([win_prev[l], win_s.reshape(DB, dq, LANE)], 1)
    wlen = min(WINDOW, win_all.shape[1])
    states = (cmp_s.reshape(DB, dq, 2, DH), sel_s.reshape(DB, dq, 2, DH),
              win_all[:, win_all.shape[1] - wlen:].reshape(DB, wlen, 2, DH),
              lat.reshape(DB, dq, D_C), misc[:, MISC_KPE: MISC_KPE + D_ROPE].reshape(DB, dq, D_ROPE),
              fkv.reshape(DB, dq, 2, KV_FOX, DH), logf_new, dkv.reshape(DB, dq, 2, KV_DIFF, 2 * DD))
    return x_new, states


def kernel(x_prompt, x_sample, cache_nsa_cmp_kv, cache_nsa_sel_kv, state_nsa_win_kv, cache_mla_latent,
           cache_mla_kpe, cache_fox_kv, cache_fox_logf, cache_diff_kv, page_table, rel_bias, w_in,
           nsa_cmp_pos, nsa_cmp_w1, nsa_cmp_w2, mla_q_norm, mla_kv_norm, mla_w_uq, mla_w_uk, mla_w_uv,
           fox_forget_bias, diff_lambda, diff_subln, w_branch, w_out, ln_g, ln_b):
    past_len = page_table.shape[1] * PAGE_SIZE
    S = x_prompt.shape[1]
    DB, dq = x_sample.shape[:2]
    depth, n_pool = cache_mla_latent.shape[:2]
    tabs_s = _rope_tables(jnp.tile(past_len + jnp.arange(dq), DB))
    bias_s = _sample_bias_tables(rel_bias, past_len, dq, state_nsa_win_kv.shape[2])
    lf = jnp.swapaxes(cache_fox_logf, 2, 3)
    lf = jnp.broadcast_to(lf[:, :, :, None, :], (depth, n_pool, H_FOX, dq, PAGE_SIZE))
    n_rows = depth * n_pool * H_FOX * dq
    sfx, tot = _page_suffix(lf.reshape(n_rows, PAGE_SIZE), 2048 if n_rows % 2048 == 0 else n_rows)
    sfx = sfx.reshape(depth, n_pool, H_FOX * dq, PAGE_SIZE)
    tot = tot.reshape(depth, n_pool, H_FOX * dq, PAGE_SIZE)
    fm = lambda a, feat: jnp.swapaxes(a.reshape(a.shape[0], a.shape[1], a.shape[2], feat), 2, 3)
    win_prev = state_nsa_win_kv.reshape(depth, DB, state_nsa_win_kv.shape[2], 2 * DH)
    caches = (fm(cache_nsa_cmp_kv, 2 * DH), fm(cache_nsa_sel_kv, 2 * DH), fm(state_nsa_win_kv, 2 * DH), win_prev,
              cache_mla_latent, fm(cache_mla_kpe, D_ROPE),
              fm(cache_fox_kv, 2 * KV_FOX * DH), fm(cache_diff_kv, 2 * KV_DIFF * 2 * DD))
    tabs_p = _rope_tables(jnp.arange(S))
    tb_a, tb_d, cb = _prompt_bias_tables(rel_bias, S)
    u_tri = (jnp.arange(LANE)[:, None] <= jnp.arange(LANE)[None, :]).astype(BF)
    xp, xs = x_prompt, x_sample
    st_p, st_s = [], []
    for l in range(depth):
        lwp = _layer_weights(l, w_in, nsa_cmp_pos, nsa_cmp_w1, nsa_cmp_w2, mla_q_norm, mla_kv_norm, mla_w_uq,
                             mla_w_uk, mla_w_uv, fox_forget_bias, diff_lambda, diff_subln, w_branch, w_out,
                             ln_g, ln_b)
        xp, sp = _prompt_layer(xp, lwp, tabs_p, tb_a, tb_d, cb, u_tri)
        xs, ss = _sample_layer(xs, l, lwp, tabs_s, bias_s, page_table, caches, sfx, tot)
        st_p.append(sp)
        st_s.append(ss)
    (p_cmp, p_sel, p_win, p_lat, p_kpe, p_fkv, p_flf, p_dkv) = [jnp.stack(z) for z in zip(*st_p)]
    (s_cmp, s_sel, s_win, s_lat, s_kpe, s_fkv, s_flf, s_dkv) = [jnp.stack(z) for z in zip(*st_s)]
    return (xp, xs, p_cmp, s_cmp, p_sel, s_sel, p_win, s_win, p_lat, s_lat, p_kpe, s_kpe,
            p_fkv, s_fkv, p_flf, s_flf, p_dkv, s_dkv)
```
